```python
import jax, jax.numpy as jnp
from jax import lax
import numpy as np

D_MODEL = 1024
BATCH = 8
SEQ = 2048
DEPTH = 1
DEC_BATCH = 128
DEC_SEQ = 1
PAST_LEN = 16384
PAGE_SIZE = 128

EXPAND = 2
D_INNER = EXPAND * D_MODEL
HEAD_DIM = 64
N_HEADS = D_INNER // HEAD_DIM
N_GROUPS = 8
HEADS_PER_GROUP = N_HEADS // N_GROUPS
D_STATE = 128
CONV_WIDTH = 4
CONV_DIM = D_INNER + 2 * N_GROUPS * D_STATE
CHUNK = 128
POOL_DIM = D_MODEL
POOL_WINDOWS = (2, 4, 8, 16)
N_POOL_GROUPS = len(POOL_WINDOWS)
POOL_GROUP_DIM = POOL_DIM // N_POOL_GROUPS
POOL_BUF = max(POOL_WINDOWS) - 1
D_FF = -(-8 * D_MODEL // (3 * 256)) * 256
IN_WIDTHS = (D_INNER, CONV_DIM, N_HEADS, POOL_DIM, D_MODEL, D_MODEL)
IN_DIM = sum(IN_WIDTHS)
EPS = 1e-6

kernel_name = 'hybrid_ssd_pool_gated_decoder_step'


def rms_norm(x, g):
    xf = x.astype(jnp.float32)
    y = xf * lax.rsqrt(jnp.mean(xf * xf, axis=-1, keepdims=True) + EPS)
    return (y * g.astype(jnp.float32)).astype(x.dtype)


def causal_conv(u, prev, conv_w, conv_b):
    L = u.shape[1]
    cat = jnp.concatenate([prev.astype(u.dtype), u], axis=1)
    out = conv_b
    for k in range(CONV_WIDTH):
        out = out + cat[:, k:k + L] * conv_w[k]
    return jax.nn.silu(out), cat[:, -(CONV_WIDTH - 1):]


def ssd_scan(x, dt, a, bmat, cmat, h0):
    bsz, L = x.shape[:2]
    q = min(CHUNK, L)
    n_c = -(-L // q)
    pad = n_c * q - L

    def prep(t):
        t = jnp.pad(t.astype(jnp.float32), [(0, 0), (0, pad)] + [(0, 0)] * (t.ndim - 2))
        return jnp.moveaxis(t.reshape((bsz, n_c, q) + t.shape[2:]), 1, 0)

    xs = prep(x.reshape(bsz, L, N_GROUPS, HEADS_PER_GROUP, HEAD_DIM))
    dts = prep(dt.reshape(bsz, L, N_GROUPS, HEADS_PER_GROUP))
    bs = prep(bmat)
    cs = prep(cmat)
    a_gr = a.astype(jnp.float32).reshape(N_GROUPS, HEADS_PER_GROUP)
    causal = jnp.tril(jnp.ones((q, q), dtype=bool))[None, :, :, None, None]

    def step(h, inp):
        xc, dtc, bc, cc = inp
        acum = jnp.cumsum(dtc * a_gr, axis=1)
        seg = acum[:, :, None] - acum[:, None, :]
        decay = jnp.exp(jnp.where(causal, seg, -jnp.inf))
        cb = jnp.einsum('bign,bjgn->bijg', cc, bc)
        m = cb[..., None] * decay * dtc[:, None]
        y = jnp.einsum('bijgr,bjgrp->bigrp', m, xc)
        y = y + jnp.einsum('bign,bgrpn->bigrp', cc, h) * jnp.exp(acum)[..., None]
        w_end = jnp.exp(acum[:, -1:] - acum) * dtc
        h_new = h * jnp.exp(acum[:, -1])[..., None, None] + jnp.einsum('bjgr,bjgn,bjgrp->bgrpn', w_end, bc, xc)
        return h_new, y

    h0g = h0.astype(jnp.float32).reshape(bsz, N_GROUPS, HEADS_PER_GROUP, HEAD_DIM, D_STATE)
    h_fin, ys = lax.scan(step, h0g, (xs, dts, bs, cs))
    y = jnp.moveaxis(ys, 0, 1).reshape(bsz, n_c * q, N_HEADS, HEAD_DIM)[:, :L]
    return y, h_fin.reshape(bsz, N_HEADS, HEAD_DIM, D_STATE)


def pool_mix(u, prev, start_pos, w_pool_group, pool_scale):
    bsz, L = u.shape[:2]
    cat = jnp.concatenate([prev.astype(u.dtype), u], axis=1)
    csum = jnp.pad(jnp.cumsum(cat.astype(jnp.float32), axis=1), ((0, 0), (1, 0), (0, 0)))
    pos = start_pos + jnp.arange(L)
    outs = []
    for gi, w in enumerate(POOL_WINDOWS):
        sl = slice(gi * POOL_GROUP_DIM, (gi + 1) * POOL_GROUP_DIM)
        end = csum[:, POOL_BUF + 1:POOL_BUF + 1 + L, sl]
        beg = csum[:, POOL_BUF + 1 - w:POOL_BUF + 1 - w + L, sl]
        cnt = jnp.minimum(pos + 1, w).astype(jnp.float32)[None, :, None]
        outs.append((end - beg) / cnt - u[..., sl].astype(jnp.float32))
    d = jnp.stack(outs, axis=2).astype(u.dtype)
    mixed = jnp.einsum('blgc,gcd->blgd', d, w_pool_group).reshape(bsz, L, POOL_DIM)
    return mixed * pool_scale, cat[:, -POOL_BUF:]


def hybrid_layer(x, conv_prev, ssm_prev, pool_prev, start_pos, p):
    bsz, L = x.shape[:2]
    xn = rms_norm(x, p['norm_mix_pre'])
    proj = xn @ p['w_in']
    idx = np.cumsum(IN_WIDTHS)[:-1].tolist()
    z, xbc, dt_raw, u_pool, gate_a, gate_b = jnp.split(proj, idx, axis=-1)
    xbc_c, conv_new = causal_conv(xbc, conv_prev, p['conv_w'], p['conv_b'])
    xs, bm, cm = jnp.split(xbc_c, [D_INNER, D_INNER + N_GROUPS * D_STATE], axis=-1)
    xs = xs.reshape(bsz, L, N_HEADS, HEAD_DIM)
    dt = jax.nn.softplus(dt_raw.astype(jnp.float32) + p['dt_bias'].astype(jnp.float32))
    a = -jnp.exp(p['a_log'].astype(jnp.float32))
    y, ssm_new = ssd_scan(xs, dt, a, bm.reshape(bsz, L, N_GROUPS, D_STATE),
                          cm.reshape(bsz, L, N_GROUPS, D_STATE), ssm_prev)
    y = y + p['d_skip'].astype(jnp.float32)[:, None] * xs.astype(jnp.float32)
    y = y.reshape(bsz, L, D_INNER).astype(x.dtype)
    y_a = rms_norm(y * jax.nn.silu(z), p['ssm_norm'])
    y_b, pool_new = pool_mix(u_pool, pool_prev, start_pos, p['w_pool_group'], p['pool_scale'])
    merged = jax.nn.sigmoid(gate_a) * (y_a @ p['w_branch_a']) + jax.nn.sigmoid(gate_b) * (y_b @ p['w_branch_b'])
    h = x + rms_norm(merged @ p['w_out'], p['norm_mix_post'])
    gu = rms_norm(h, p['norm_ffn_pre']) @ p['w_ffn_in']
    g, up = jnp.split(gu, [D_FF], axis=-1)
    f = (jax.nn.silu(g) * up) @ p['w_ffn_out']
    out = h + rms_norm(f, p['norm_ffn_post'])
    return out, conv_new, ssm_new, pool_new


def setup_inputs(seed: int = 0) -> dict:
    key = jax.random.key(seed)
    ks = jax.random.split(key, 24)
    f32 = jnp.float32
    nrm = lambda k, s, sc: jax.random.normal(k, s, f32) * sc
    gain = lambda k, n: 1.0 + nrm(k, (DEPTH, n), 0.05)
    dt0 = jnp.exp(jax.random.uniform(ks[10], (DEPTH, N_HEADS), f32, np.log(1e-3), np.log(1e-1)))
    return {
        'x_prompt': nrm(ks[0], (BATCH, SEQ, D_MODEL), 1.0),
        'x_sample': nrm(ks[1], (DEC_BATCH, DEC_SEQ, D_MODEL), 1.0),
        'state_conv': nrm(ks[2], (DEPTH, DEC_BATCH, CONV_WIDTH - 1, CONV_DIM), 1.0),
        'state_ssm': nrm(ks[3], (DEPTH, DEC_BATCH, N_HEADS, HEAD_DIM, D_STATE), 0.1),
        'state_pool': nrm(ks[4], (DEPTH, DEC_BATCH, POOL_BUF, POOL_DIM), 1.0),
        'norm_mix_pre': gain(ks[5], D_MODEL),
        'norm_mix_post': gain(ks[6], D_MODEL),
        'norm_ffn_pre': gain(ks[7], D_MODEL),
        'norm_ffn_post': gain(ks[8], D_MODEL),
        'w_in': nrm(ks[9], (DEPTH, D_MODEL, IN_DIM), D_MODEL ** -0.5),
        'conv_w': nrm(ks[11], (DEPTH, CONV_WIDTH, CONV_DIM), CONV_WIDTH ** -0.5),
        'conv_b': nrm(ks[12], (DEPTH, CONV_DIM), 0.01),
        'dt_bias': dt0 + jnp.log(-jnp.expm1(-dt0)),
        'a_log': jnp.log(jax.random.uniform(ks[13], (DEPTH, N_HEADS), f32, 1.0, 16.0)),
        'd_skip': 1.0 + nrm(ks[14], (DEPTH, N_HEADS), 0.1),
        'ssm_norm': gain(ks[15], D_INNER),
        'w_pool_group': nrm(ks[16], (DEPTH, N_POOL_GROUPS, POOL_GROUP_DIM, POOL_GROUP_DIM), POOL_GROUP_DIM ** -0.5),
        'pool_scale': 1.0 + nrm(ks[17], (DEPTH, POOL_DIM), 0.1),
        'w_branch_a': nrm(ks[18], (DEPTH, D_INNER, D_MODEL), D_INNER ** -0.5),
        'w_branch_b': nrm(ks[19], (DEPTH, POOL_DIM, D_MODEL), POOL_DIM ** -0.5),
        'w_out': nrm(ks[20], (DEPTH, D_MODEL, D_MODEL), D_MODEL ** -0.5),
        'w_ffn_in': nrm(ks[21], (DEPTH, D_MODEL, 2 * D_FF), D_MODEL ** -0.5),
        'w_ffn_out': nrm(ks[22], (DEPTH, D_FF, D_MODEL), D_FF ** -0.5),
    }


def reference(x_prompt, x_sample, state_conv, state_ssm, state_pool,
              norm_mix_pre, norm_mix_post, norm_ffn_pre, norm_ffn_post,
              w_in, conv_w, conv_b, dt_bias, a_log, d_skip, ssm_norm,
              w_pool_group, pool_scale, w_branch_a, w_branch_b, w_out,
              w_ffn_in, w_ffn_out):
    h_p, h_s = x_prompt, x_sample
    bp = x_prompt.shape[0]
    conv_p, ssm_p, pool_p, conv_s, ssm_s, pool_s = [], [], [], [], [], []
    for l in range(DEPTH):
        p = dict(norm_mix_pre=norm_mix_pre[l], norm_mix_post=norm_mix_post[l],
                 norm_ffn_pre=norm_ffn_pre[l], norm_ffn_post=norm_ffn_post[l],
                 w_in=w_in[l], conv_w=conv_w[l], conv_b=conv_b[l], dt_bias=dt_bias[l],
                 a_log=a_log[l], d_skip=d_skip[l], ssm_norm=ssm_norm[l],
                 w_pool_group=w_pool_group[l], pool_scale=pool_scale[l],
                 w_branch_a=w_branch_a[l], w_branch_b=w_branch_b[l], w_out=w_out[l],
                 w_ffn_in=w_ffn_in[l], w_ffn_out=w_ffn_out[l])
        zc = jnp.zeros((bp, CONV_WIDTH - 1, CONV_DIM), x_prompt.dtype)
        zs = jnp.zeros((bp, N_HEADS, HEAD_DIM, D_STATE), jnp.float32)
        zq = jnp.zeros((bp, POOL_BUF, POOL_DIM), x_prompt.dtype)
        h_p, c1, s1, q1 = hybrid_layer(h_p, zc, zs, zq, 0, p)
        h_s, c2, s2, q2 = hybrid_layer(h_s, state_conv[l], state_ssm[l], state_pool[l], PAST_LEN, p)
        conv_p.append(c1); ssm_p.append(s1); pool_p.append(q1)
        conv_s.append(c2); ssm_s.append(s2); pool_s.append(q2)
    new_conv_prompt = jnp.stack(conv_p)
    new_ssm_prompt = jnp.stack(ssm_p)
    new_pool_prompt = jnp.stack(pool_p)
    new_conv_sample = jnp.stack(conv_s)
    new_ssm_sample = jnp.stack(ssm_s)
    new_pool_sample = jnp.stack(pool_s)
    return (h_p, h_s, new_conv_prompt, new_ssm_prompt, new_pool_prompt,
            new_conv_sample, new_ssm_sample, new_pool_sample)
```

```python
import functools

import jax
import jax.numpy as jnp
from jax import lax
from jax.experimental import pallas as pl
from jax.experimental.pallas import tpu as pltpu

D_MODEL = 1024
D_INNER = 2048
HEAD_DIM = 64
N_HEADS = 32
N_GROUPS = 8
HEADS_PER_GROUP = 4
D_STATE = 128
CONV_WIDTH = 4
CONV_DIM = 4096
CHUNK = 128
POOL_DIM = 1024
POOL_WINDOWS = (2, 4, 8, 16)
POOL_GROUP_DIM = 256
POOL_BUF = 15
D_FF = 2816
EPS = 1e-6
PAST_LEN = 16384

LANES = 128
GROUP_COLS = HEADS_PER_GROUP * HEAD_DIM
DT_PAD = LANES
VMEM_LIMIT = 56 * 1024 * 1024

F32 = jnp.float32
BF16 = jnp.bfloat16


def _rms(x, g):
    y = x * lax.rsqrt(jnp.mean(x * x, axis=-1, keepdims=True) + EPS)
    return y * g


def _silu(x):
    return x * jax.nn.sigmoid(x)


def _softplus(x):
    return jnp.maximum(x, 0.0) + jnp.log1p(jnp.exp(-jnp.abs(x)))


def _split3(x):
    hi = x.astype(BF16)
    r = x - hi.astype(F32)
    mid = r.astype(BF16)
    lo = (r - mid.astype(F32)).astype(BF16)
    return hi, mid, lo


def _dot(a, b):
    return jnp.dot(a, b, preferred_element_type=F32)


def _dot_nt(a, b):
    return lax.dot_general(a, b, (((1,), (1,)), ((), ())), preferred_element_type=F32)


def _const_spec(shape):
    nd = len(shape)
    return pl.BlockSpec(shape, lambda *_: (0,) * nd, pipeline_mode=pl.Buffered(1))


def _in_proj_kernel(x_ref, g_ref, w_ref, z_ref, xbc_ref, u_ref, gate_ref, dt_ref):
    xn = _rms(x_ref[...], g_ref[...]).astype(BF16)
    off = 0
    for o_ref in (z_ref, xbc_ref, u_ref, gate_ref, dt_ref):
        width = o_ref.shape[1]
        for c in range(0, width, 1024):
            cw = min(1024, width - c)
            o_ref[:, c:c + cw] = _dot(xn, w_ref[:, off + c:off + c + cw])
        off += width


def _in_proj(x, g, w, tm):
    m = x.shape[0]
    widths = (D_INNER, CONV_DIM, POOL_DIM, 2 * D_MODEL, DT_PAD)
    return pl.pallas_call(
        _in_proj_kernel,
        grid=(m // tm,),
        in_specs=[pl.BlockSpec((tm, D_MODEL), lambda i: (i, 0)),
                  _const_spec((1, D_MODEL)),
                  _const_spec(w.shape)],
        out_specs=[pl.BlockSpec((tm, wd), lambda i: (i, 0)) for wd in widths],
        out_shape=[jax.ShapeDtypeStruct((m, wd), F32) for wd in widths],
        compiler_params=pltpu.CompilerParams(dimension_semantics=("arbitrary",),
                                             vmem_limit_bytes=VMEM_LIMIT),
        name="in_proj",
    )(x, g, w)


def _conv_silu(taps, cw_ref, cb_ref, cols):
    acc = cb_ref[:, cols]
    for k, t in enumerate(taps):
        acc = acc + t * cw_ref[k:k + 1, cols]
    return _silu(acc)


def _pool_mix(win_sum, u, cnt, wp_ref, ps_ref, gi):
    cols = slice(gi * POOL_GROUP_DIM, (gi + 1) * POOL_GROUP_DIM)
    d = win_sum / cnt - u
    mixed = _dot(d.astype(BF16), wp_ref[gi])
    return (mixed * ps_ref[:, cols]).astype(BF16)


def _mixer_kernel(xbc_ref, dt_ref, u_ref, cw_ref, cb_ref, dtb_ref, alog_ref, dskip_ref, wp_ref, ps_ref,
                  y_ref, yb_ref, nconv_ref, nssm_ref, npool_ref,
                  cbuf, pbuf, ht_ref, xc_ref):
    q = CHUNK
    c = pl.program_id(1)
    last_c = pl.num_programs(1) - 1
    halo_c, halo_p = 8, 16

    @pl.when(c == 0)
    def _():
        cbuf[0:halo_c, :] = jnp.zeros((halo_c, CONV_DIM), F32)
        pbuf[0:halo_p, :] = jnp.zeros((halo_p, POOL_DIM), F32)
        ht_ref[...] = jnp.zeros(ht_ref.shape, F32)

    cbuf[halo_c:halo_c + q, :] = xbc_ref[...]
    for c0 in range(0, CONV_DIM, 512):
        cols = slice(c0, c0 + 512)
        taps = [cbuf[halo_c - (CONV_WIDTH - 1) + k:halo_c - (CONV_WIDTH - 1) + k + q, cols]
                for k in range(CONV_WIDTH)]
        xc_ref[:, cols] = _conv_silu(taps, cw_ref, cb_ref, cols)
    cbuf[0:halo_c, :] = xbc_ref[q - halo_c:q, :]

    dt = _softplus(dt_ref[...] + dtb_ref[...])
    da = dt * (-jnp.exp(alog_ref[...]))
    row = lax.broadcasted_iota(jnp.int32, (q, q), 0)
    col = lax.broadcasted_iota(jnp.int32, (q, q), 1)
    causal = row >= col
    tri = jnp.where(causal, 1.0, 0.0).astype(BF16)
    acum = sum(_dot(tri, part) for part in _split3(da))
    e_acum = jnp.exp(acum)
    w_end = jnp.exp(acum[q - 1:q, :] - acum) * dt
    acum_t = acum.T
    dt_t = dt.T
    lo_half = col < HEAD_DIM

    for g in range(N_GROUPS):
        b_g = xc_ref[:, D_INNER + g * D_STATE:D_INNER + (g + 1) * D_STATE]
        c_g = xc_ref[:, D_INNER + N_GROUPS * D_STATE + g * D_STATE:D_INNER + N_GROUPS * D_STATE + (g + 1) * D_STATE]
        c_b = c_g.astype(BF16)
        cb = _dot_nt(c_b, b_g.astype(BF16))
        ht_g = ht_ref[g]
        y_inter = _dot(c_b, ht_g.astype(BF16))
        xw = []
        e_last = []
        for pair in range(HEADS_PER_GROUP // 2):
            h0 = g * HEADS_PER_GROUP + 2 * pair
            h1 = h0 + 1
            pcols = slice(h0 * HEAD_DIM, (h0 + 2) * HEAD_DIM)
            x_pair = xc_ref[:, pcols]
            ms = []
            for h in (h0, h1):
                seg = acum[:, h:h + 1] - acum_t[h:h + 1, :]
                decay = jnp.exp(jnp.where(causal, seg, -jnp.inf))
                ms.append((cb * decay * dt_t[h:h + 1, :]).astype(BF16))
            x_top = jnp.where(lo_half, x_pair, 0.0).astype(BF16)
            x_bot = jnp.where(lo_half, 0.0, x_pair).astype(BF16)
            y_intra = _dot(jnp.concatenate(ms, axis=1), jnp.concatenate([x_top, x_bot], axis=0))
            e_pair = jnp.where(lo_half, e_acum[:, h0:h0 + 1], e_acum[:, h1:h1 + 1])
            y_ref[:, pcols] = (y_intra + y_inter[:, pair * LANES:(pair + 1) * LANES] * e_pair
                               + dskip_ref[:, pcols] * x_pair)
            w_pair = jnp.where(lo_half, w_end[:, h0:h0 + 1], w_end[:, h1:h1 + 1])
            xw.append((x_pair * w_pair).astype(BF16))
            e_last.append(jnp.where(lo_half, e_acum[q - 1:q, h0:h0 + 1], e_acum[q - 1:q, h1:h1 + 1]))
        upd = _dot(b_g.T.astype(BF16), jnp.concatenate(xw, axis=1))
        ht_ref[g] = ht_g * jnp.concatenate(e_last, axis=1) + upd

    pbuf[halo_p:halo_p + q, :] = u_ref[...]
    pos = c * q + lax.broadcasted_iota(jnp.int32, (q, 1), 0)
    for gi, w in enumerate(POOL_WINDOWS):
        cols = slice(gi * POOL_GROUP_DIM, (gi + 1) * POOL_GROUP_DIM)
        u_g = pbuf[halo_p:halo_p + q, cols]
        s = u_g
        for k in range(1, w):
            s = s + pbuf[halo_p - k:halo_p - k + q, cols]
        cnt = jnp.minimum(pos + 1, w).astype(F32)
        yb_ref[:, cols] = _pool_mix(s, u_g, cnt, wp_ref, ps_ref, gi)
    pbuf[0:halo_p, :] = u_ref[q - halo_p:q, :]

    @pl.when(c == last_c)
    def _():
        nconv_ref[0] = xbc_ref[q - (CONV_WIDTH - 1):q, :]
        npool_ref[0] = u_ref[q - POOL_BUF:q, :]
        for g in range(N_GROUPS):
            nssm_ref[0, g * HEADS_PER_GROUP:(g + 1) * HEADS_PER_GROUP] = (
                ht_ref[g].T.reshape(HEADS_PER_GROUP, HEAD_DIM, D_STATE))


def _mixer_prompt(xbc, dt, u, cw, cb, dtb, alog, dskip_e, wp, ps, bsz, seq):
    nc = seq // CHUNK
    m = bsz * seq
    tok = lambda b, c: (b * nc + c, 0)
    per_b = lambda b, c: (b, 0, 0)
    return pl.pallas_call(
        _mixer_kernel,
        grid=(bsz, nc),
        in_specs=[pl.BlockSpec((CHUNK, CONV_DIM), tok),
                  pl.BlockSpec((CHUNK, DT_PAD), tok),
                  pl.BlockSpec((CHUNK, POOL_DIM), tok),
                  _const_spec(cw.shape), _const_spec(cb.shape), _const_spec(dtb.shape),
                  _const_spec(alog.shape), _const_spec(dskip_e.shape), _const_spec(wp.shape),
                  _const_spec(ps.shape)],
        out_specs=[pl.BlockSpec((CHUNK, D_INNER), tok),
                   pl.BlockSpec((CHUNK, POOL_DIM), tok),
                   pl.BlockSpec((1, CONV_WIDTH - 1, CONV_DIM), per_b),
                   pl.BlockSpec((1, N_HEADS, HEAD_DIM, D_STATE), lambda b, c: (b, 0, 0, 0)),
                   pl.BlockSpec((1, POOL_BUF, POOL_DIM), per_b)],
        out_shape=[jax.ShapeDtypeStruct((m, D_INNER), F32),
                   jax.ShapeDtypeStruct((m, POOL_DIM), BF16),
                   jax.ShapeDtypeStruct((bsz, CONV_WIDTH - 1, CONV_DIM), F32),
                   jax.ShapeDtypeStruct((bsz, N_HEADS, HEAD_DIM, D_STATE), F32),
                   jax.ShapeDtypeStruct((bsz, POOL_BUF, POOL_DIM), F32)],
        scratch_shapes=[pltpu.VMEM((8 + CHUNK, CONV_DIM), F32),
                        pltpu.VMEM((16 + CHUNK, POOL_DIM), F32),
                        pltpu.VMEM((N_GROUPS, D_STATE, GROUP_COLS), F32),
                        pltpu.VMEM((CHUNK, CONV_DIM), F32)],
        compiler_params=pltpu.CompilerParams(dimension_semantics=("arbitrary", "arbitrary"),
                                             vmem_limit_bytes=VMEM_LIMIT),
        name="mixer_prompt",
    )(xbc, dt, u, cw, cb, dtb, alog, dskip_e, wp, ps)


def _sample_tok_kernel(xbc_ref, dt_ref, u_ref, sconv_ref, spool_ref, cw_ref, cb_ref, dtb_ref, alog_ref,
                       wp_ref, ps_ref,
                       xc_ref, dec_ref, dtx_ref, nconv_ref, npool_ref, yb_ref):
    for c0 in range(0, CONV_DIM, 512):
        cols = slice(c0, c0 + 512)
        taps = [sconv_ref[:, k * CONV_DIM + c0:k * CONV_DIM + c0 + 512] for k in range(CONV_WIDTH - 1)]
        taps.append(xbc_ref[:, cols])
        xc_ref[:, cols] = _conv_silu(taps, cw_ref, cb_ref, cols)
    nconv_ref[:, 0:(CONV_WIDTH - 2) * CONV_DIM] = sconv_ref[:, CONV_DIM:]
    nconv_ref[:, (CONV_WIDTH - 2) * CONV_DIM:] = xbc_ref[...]

    dt = _softplus(dt_ref[...] + dtb_ref[...])
    dec_ref[...] = jnp.exp(dt * (-jnp.exp(alog_ref[...])))
    hrow = lax.broadcasted_iota(jnp.int32, (DT_PAD, D_INNER), 0)
    ccol = lax.broadcasted_iota(jnp.int32, (DT_PAD, D_INNER), 1)
    expand = jnp.where((ccol >= hrow * HEAD_DIM) & (ccol < (hrow + 1) * HEAD_DIM), 1.0, 0.0).astype(BF16)
    dt_e = sum(_dot(part, expand) for part in _split3(dt))
    dtx_ref[...] = dt_e * xc_ref[:, 0:D_INNER]

    cnt_base = PAST_LEN + 1
    for gi, w in enumerate(POOL_WINDOWS):
        cols = slice(gi * POOL_GROUP_DIM, (gi + 1) * POOL_GROUP_DIM)
        u_g = u_ref[:, cols]
        s = u_g
        for k in range(1, w):
            o = (POOL_BUF - k) * POOL_DIM + gi * POOL_GROUP_DIM
            s = s + spool_ref[:, o:o + POOL_GROUP_DIM]
        yb_ref[:, cols] = _pool_mix(s, u_g, float(min(cnt_base, w)), wp_ref, ps_ref, gi)
    npool_ref[:, 0:(POOL_BUF - 1) * POOL_DIM] = spool_ref[:, POOL_DIM:]
    npool_ref[:, (POOL_BUF - 1) * POOL_DIM:] = u_ref[...]


def _sample_tok(xbc, dt, u, sconv, spool, cw, cb, dtb, alog, wp, ps):
    nb = xbc.shape[0]
    full = lambda a: pl.BlockSpec(a.shape, lambda i: (0,) * a.ndim)
    args = (xbc, dt, u, sconv, spool, cw, cb, dtb, alog, wp, ps)
    out_shape = [jax.ShapeDtypeStruct((nb, CONV_DIM), F32),
                 jax.ShapeDtypeStruct((nb, DT_PAD), F32),
                 jax.ShapeDtypeStruct((nb, D_INNER), F32),
                 jax.ShapeDtypeStruct(sconv.shape, F32),
                 jax.ShapeDtypeStruct(spool.shape, F32),
                 jax.ShapeDtypeStruct((nb, POOL_DIM), BF16)]
    return pl.pallas_call(
        _sample_tok_kernel,
        grid=(1,),
        in_specs=[full(a) for a in args],
        out_specs=[pl.BlockSpec(s.shape, lambda i: (0, 0)) for s in out_shape],
        out_shape=out_shape,
        compiler_params=pltpu.CompilerParams(dimension_semantics=("arbitrary",),
                                             vmem_limit_bytes=VMEM_LIMIT),
        name="sample_tok",
    )(*args)


def _sample_ssm_kernel(st_ref, b_ref, c_ref, dec_ref, dtx_ref, xs_ref, dskip_ref, y_ref, nst_ref):
    grow = lax.broadcasted_iota(jnp.int32, (N_GROUPS, D_INNER), 0)
    gcol = lax.broadcasted_iota(jnp.int32, (N_GROUPS, D_INNER), 1)
    gmask = (gcol >= grow * GROUP_COLS) & (gcol < (grow + 1) * GROUP_COLS)
    dtx = jnp.where(gmask, dtx_ref[0], 0.0)
    x_hi, x_mid, _ = _split3(dtx)
    b_hi, b_mid, _ = _split3(b_ref[0])
    lhs = jnp.concatenate([x_hi, x_mid, x_hi], axis=0).astype(F32)
    rhs = jnp.concatenate([b_hi, b_hi, b_mid], axis=0).astype(F32)
    upd = lax.dot_general(lhs, rhs, (((0,), (0,)), ((), ())), preferred_element_type=F32)
    dec = dec_ref[0]
    c_b = c_ref[0].astype(BF16)
    for g in range(N_GROUPS):
        new_g = []
        for r in range(HEADS_PER_GROUP):
            h = g * HEADS_PER_GROUP + r
            scale = jnp.broadcast_to(dec[:, h:h + 1], (HEAD_DIM, D_STATE))
            new = st_ref[0, h] * scale + upd[h * HEAD_DIM:(h + 1) * HEAD_DIM, :]
            nst_ref[0, h] = new
            new_g.append(new.astype(BF16))
        y_g = _dot_nt(c_b, jnp.concatenate(new_g, axis=0))
        gcols = slice(g * GROUP_COLS, (g + 1) * GROUP_COLS)
        y_ref[0, :, gcols] = y_g[g:g + 1, :] + dskip_ref[:, gcols] * xs_ref[0, :, gcols]


def _sample_ssm(state, b3, c3, dec3, dtx3, xs3, dskip_e):
    nb = state.shape[0]
    tok3 = lambda i: (i, 0, 0)
    return pl.pallas_call(
        _sample_ssm_kernel,
        grid=(nb,),
        in_specs=[pl.BlockSpec((1, N_HEADS, HEAD_DIM, D_STATE), lambda i: (i, 0, 0, 0)),
                  pl.BlockSpec((1, N_GROUPS, D_STATE), tok3),
                  pl.BlockSpec((1, N_GROUPS, D_STATE), tok3),
                  pl.BlockSpec((1, 1, DT_PAD), tok3),
                  pl.BlockSpec((1, 1, D_INNER), tok3),
                  pl.BlockSpec((1, 1, D_INNER), tok3),
                  _const_spec(dskip_e.shape)],
        out_specs=[pl.BlockSpec((1, 1, D_INNER), tok3),
                   pl.BlockSpec((1, N_HEADS, HEAD_DIM, D_STATE), lambda i: (i, 0, 0, 0))],
        out_shape=[jax.ShapeDtypeStruct((nb, 1, D_INNER), F32),
                   jax.ShapeDtypeStruct(state.shape, F32)],
        compiler_params=pltpu.CompilerParams(dimension_semantics=("arbitrary",),
                                             vmem_limit_bytes=VMEM_LIMIT),
        name="sample_ssm",
    )(state, b3, c3, dec3, dtx3, xs3, dskip_e)


def _post_kernel(x_ref, z_ref, y_ref, yb_ref, gate_ref, sn_ref, wa_ref, wb_ref, wo_ref, npost_ref, h_ref):
    ya_in = _rms(y_ref[...] * _silu(z_ref[...]), sn_ref[...]).astype(BF16)
    ya = _dot(ya_in, wa_ref[...])
    yb = _dot(yb_ref[...], wb_ref[...])
    merged = (jax.nn.sigmoid(gate_ref[:, 0:D_MODEL]) * ya
              + jax.nn.sigmoid(gate_ref[:, D_MODEL:]) * yb)
    mo = _dot(merged.astype(BF16), wo_ref[...])
    h_ref[...] = x_ref[...] + _rms(mo, npost_ref[...])


def _post(x, z, y, yb, gate, sn, wa, wb, wo, npost, tm):
    m = x.shape[0]
    tok = lambda i: (i, 0)
    return pl.pallas_call(
        _post_kernel,
        grid=(m // tm,),
        in_specs=[pl.BlockSpec((tm, D_MODEL), tok), pl.BlockSpec((tm, D_INNER), tok),
                  pl.BlockSpec((tm, D_INNER), tok), pl.BlockSpec((tm, POOL_DIM), tok),
                  pl.BlockSpec((tm, 2 * D_MODEL), tok),
                  _const_spec(sn.shape), _const_spec(wa.shape), _const_spec(wb.shape),
                  _const_spec(wo.shape), _const_spec(npost.shape)],
        out_specs=pl.BlockSpec((tm, D_MODEL), tok),
        out_shape=jax.ShapeDtypeStruct((m, D_MODEL), F32),
        compiler_params=pltpu.CompilerParams(dimension_semantics=("arbitrary",),
                                             vmem_limit_bytes=VMEM_LIMIT),
        name="post",
    )(x, z, y, yb, gate, sn, wa, wb, wo, npost)


def _ffn_kernel(h_ref, npre_ref, w1_ref, w2_ref, npost_ref, o_ref):
    h = h_ref[...]
    hn = _rms(h, npre_ref[...]).astype(BF16)
    gate = _dot(hn, w1_ref[:, 0:D_FF])
    up = _dot(hn, w1_ref[:, D_FF:])
    f = _dot((_silu(gate) * up).astype(BF16), w2_ref[...])
    o_ref[...] = h + _rms(f, npost_ref[...])


def _ffn(h, npre, w1, w2, npost, tm):
    m = h.shape[0]
    tok = lambda i: (i, 0)
    return pl.pallas_call(
        _ffn_kernel,
        grid=(m // tm,),
        in_specs=[pl.BlockSpec((tm, D_MODEL), tok), _const_spec(npre.shape), _const_spec(w1.shape),
                  _const_spec(w2.shape), _const_spec(npost.shape)],
        out_specs=pl.BlockSpec((tm, D_MODEL), tok),
        out_shape=jax.ShapeDtypeStruct((m, D_MODEL), F32),
        compiler_params=pltpu.CompilerParams(dimension_semantics=("arbitrary",),
                                             vmem_limit_bytes=VMEM_LIMIT),
        name="ffn",
    )(h, npre, w1, w2, npost)


def _token_tile(m, cap):
    t = min(m, cap)
    assert m % t == 0
    return t


def kernel(x_prompt, x_sample, state_conv, state_ssm, state_pool, norm_mix_pre, norm_mix_post, norm_ffn_pre,
           norm_ffn_post, w_in, conv_w, conv_b, dt_bias, a_log, d_skip, ssm_norm, w_pool_group, pool_scale,
           w_branch_a, w_branch_b, w_out, w_ffn_in, w_ffn_out):
    bsz, seq, _ = x_prompt.shape
    nb, dec_seq, _ = x_sample.shape
    assert seq % CHUNK == 0 and dec_seq == 1 and norm_mix_pre.shape[0] == 1
    l = 0
    o_xbc = D_INNER
    o_dt = o_xbc + CONV_DIM
    o_u = o_dt + N_HEADS
    o_g = o_u + POOL_DIM
    wi = w_in[l]
    w_cat = jnp.concatenate(
        [wi[:, :o_dt], wi[:, o_u:], jnp.pad(wi[:, o_dt:o_u], ((0, 0), (0, DT_PAD - N_HEADS)))], axis=1).astype(BF16)
    pad_h = lambda v: jnp.pad(v.astype(F32), (0, DT_PAD - N_HEADS)).reshape(1, DT_PAD)
    row = lambda v: v.astype(F32).reshape(1, -1)
    dtb, alog = pad_h(dt_bias[l]), pad_h(a_log[l])
    dskip_e = jnp.repeat(d_skip[l].astype(F32), HEAD_DIM).reshape(1, D_INNER)
    cw, cb = conv_w[l].astype(F32), row(conv_b[l])
    wp, ps = w_pool_group[l].astype(BF16), row(pool_scale[l])
    g_pre, g_post, g_fpre, g_fpost = row(norm_mix_pre[l]), row(norm_mix_post[l]), row(norm_ffn_pre[l]), row(norm_ffn_post[l])
    sn = row(ssm_norm[l])
    wa, wb, wo = w_branch_a[l].astype(BF16), w_branch_b[l].astype(BF16), w_out[l].astype(BF16)
    w1, w2 = w_ffn_in[l].astype(BF16), w_ffn_out[l].astype(BF16)

    def tail(x2, z, y, yb, gate):
        tm = _token_tile(x2.shape[0], 512)
        h = _post(x2, z, y, yb, gate, sn, wa, wb, wo, g_post, tm)
        return _ffn(h, g_fpre, w1, w2, g_fpost, tm)

    xp = x_prompt.reshape(bsz * seq, D_MODEL)
    z, xbc, u, gate, dt = _in_proj(xp, g_pre, w_cat, _token_tile(bsz * seq, 256))
    y, yb, nconv_p, nssm_p, npool_p = _mixer_prompt(xbc, dt, u, cw, cb, dtb, alog, dskip_e, wp, ps, bsz, seq)
    out_p = tail(xp, z, y, yb, gate).reshape(bsz, seq, D_MODEL)

    xs2 = x_sample.reshape(nb, D_MODEL)
    z, xbc, u, gate, dt = _in_proj(xs2, g_pre, w_cat, _token_tile(nb, 256))
    sconv = state_conv[l].reshape(nb, (CONV_WIDTH - 1) * CONV_DIM)
    spool = state_pool[l].reshape(nb, POOL_BUF * POOL_DIM)
    xc, dec, dtx, nconv_s, npool_s, yb = _sample_tok(xbc, dt, u, sconv, spool, cw, cb, dtb, alog, wp, ps)
    b3 = xc[:, D_INNER:D_INNER + N_GROUPS * D_STATE].reshape(nb, N_GROUPS, D_STATE)
    c3 = xc[:, D_INNER + N_GROUPS * D_STATE:].reshape(nb, N_GROUPS, D_STATE)
    xs3 = xc[:, :D_INNER].reshape(nb, 1, D_INNER)
    y3, nssm_s = _sample_ssm(state_ssm[l], b3, c3, dec.reshape(nb, 1, DT_PAD), dtx.reshape(nb, 1, D_INNER),
                             xs3, dskip_e)
    out_s = tail(xs2, z, y3.reshape(nb, D_INNER), yb, gate).reshape(nb, 1, D_MODEL)

    return (out_p, out_s,
            nconv_p[None], nssm_p[None], npool_p[None],
            nconv_s.reshape(1, nb, CONV_WIDTH - 1, CONV_DIM), nssm_s[None],
            npool_s.reshape(1, nb, POOL_BUF, POOL_DIM))
```

```python
import functools

import jax
import jax.numpy as jnp
from jax import lax
from jax.experimental import pallas as pl
from jax.experimental.pallas import tpu as pltpu

D_MODEL = 1024
D_INNER = 2048
HEAD_DIM = 64
N_HEADS = 32
N_GROUPS = 8
HEADS_PER_GROUP = 4
D_STATE = 128
CONV_WIDTH = 4
CONV_DIM = 4096
CHUNK = 128
POOL_DIM = 1024
POOL_WINDOWS = (2, 4, 8, 16)
POOL_GROUP_DIM = 256
POOL_BUF = 15
D_FF = 2816
EPS = 1e-6
PAST_LEN = 16384

LANES = 128
SUBLANES = 8
TILES = CHUNK // SUBLANES
GROUP_COLS = HEADS_PER_GROUP * HEAD_DIM
DT_PAD = LANES
MIX_WIDTH = D_INNER + CONV_DIM + POOL_DIM + DT_PAD
CONV_HALO = (CONV_WIDTH - 1) * SUBLANES
POOL_HALO = CHUNK
VMEM_LIMIT = 56 * 1024 * 1024

F32 = jnp.float32
BF16 = jnp.bfloat16


def _rms(x, g):
    y = x * lax.rsqrt(jnp.mean(x * x, axis=-1, keepdims=True) + EPS)
    return y * g


def _silu(x):
    return x * jax.nn.sigmoid(x)


def _softplus(x):
    return jnp.maximum(x, 0.0) + jnp.log1p(jnp.exp(-jnp.abs(x)))


def _split3(x):
    hi = x.astype(BF16)
    r = x - hi.astype(F32)
    mid = r.astype(BF16)
    lo = (r - mid.astype(F32)).astype(BF16)
    return hi, mid, lo


def _dot(a, b):
    return jnp.dot(a, b, preferred_element_type=F32)


def _dot_nt(a, b):
    return lax.dot_general(a, b, (((1,), (1,)), ((), ())), preferred_element_type=F32)


def _const_spec(shape):
    nd = len(shape)
    return pl.BlockSpec(shape, lambda *_: (0,) * nd, pipeline_mode=pl.Buffered(1))


def _time_of_row(r):
    return (r & (SUBLANES - 1)) * TILES + (r >> 3)


def _conv_silu(taps, cw_ref, cb_ref, cols):
    acc = cb_ref[:, cols]
    for k, t in enumerate(taps):
        acc = acc + t * cw_ref[k:k + 1, cols]
    return _silu(acc)


def _pool_mix(win_sum, u, cnt, wp_ref, ps_ref, gi):
    cols = slice(gi * POOL_GROUP_DIM, (gi + 1) * POOL_GROUP_DIM)
    d = win_sum / cnt - u
    mixed = _dot(d.astype(BF16), wp_ref[gi])
    return (mixed * ps_ref[:, cols]).astype(BF16)


def _project(x_ref, gpre_ref, w_ref, dtb_ref, to_perm, xbc_buf, z_buf, u_buf, dt_buf):
    xn = _rms(x_ref[...], gpre_ref[...]).astype(BF16)
    xnp = _dot(to_perm, xn).astype(BF16)
    yield
    piece = 512
    for c0 in range(0, CONV_DIM, piece):
        xbc_buf[CONV_HALO:, c0:c0 + piece] = _dot(xnp, w_ref[:, D_INNER + c0:D_INNER + c0 + piece])
        yield
    o_u = D_INNER + CONV_DIM
    for c0 in range(0, POOL_DIM, piece):
        u_buf[POOL_HALO:, c0:c0 + piece] = _dot(xnp, w_ref[:, o_u + c0:o_u + c0 + piece])
        yield
    dt_buf[...] = _softplus(_dot(xnp, w_ref[:, MIX_WIDTH - DT_PAD:MIX_WIDTH]) + dtb_ref[...])
    for c0 in range(0, D_INNER, piece):
        z_buf[:, c0:c0 + piece] = _dot(xnp, w_ref[:, c0:c0 + piece])
        yield


def _run(gen):
    for _ in gen:
        pass


def _interleave(main, side, main_per_side):
    side_live = True
    while True:
        for _ in range(main_per_side):
            if next(main, StopIteration) is StopIteration:
                if side_live:
                    _run(side)
                return
        if side_live and next(side, StopIteration) is StopIteration:
            side_live = False


def _sequence(xbc_buf, z_buf, u_buf, dt_buf, xc_buf, y_buf, ctail, ptail, ht_ref,
              cw_ref, cb_ref, alog_ref, dskip_ref, wp_ref, ps_ref, sn_ref, ya_ref, yb_ref, rows, tile_in_seq):
    q = CHUNK
    row = lax.broadcasted_iota(jnp.int32, (q, q), 0)
    col = lax.broadcasted_iota(jnp.int32, (q, q), 1)
    to_nat = jnp.where(_time_of_row(col) == row, 1.0, 0.0).astype(BF16)
    causal = _time_of_row(row) >= _time_of_row(col)
    tri = jnp.where(causal, 1.0, 0.0).astype(BF16)
    lo_half = col < HEAD_DIM
    t_col = _time_of_row(lax.broadcasted_iota(jnp.int32, (q, 1), 0))

    for n, j in enumerate(range(TILES - (CONV_WIDTH - 1), TILES)):
        src = CONV_HALO + j * SUBLANES
        xbc_buf[n * SUBLANES + 1:(n + 1) * SUBLANES, :] = xbc_buf[src:src + SUBLANES - 1, :]
        xbc_buf[n * SUBLANES:n * SUBLANES + 1, :] = ctail[n:n + 1, :]
    for c0 in range(0, CONV_DIM, 512):
        cols = slice(c0, c0 + 512)
        taps = []
        for k in range(CONV_WIDTH):
            start = CONV_HALO - (CONV_WIDTH - 1 - k) * SUBLANES
            taps.append(xbc_buf[start:start + q, cols])
        xc_buf[:, cols] = _conv_silu(taps, cw_ref, cb_ref, cols)
        yield
    for n, j in enumerate(range(TILES - (CONV_WIDTH - 1), TILES)):
        src = CONV_HALO + j * SUBLANES + SUBLANES - 1
        ctail[n:n + 1, :] = xbc_buf[src:src + 1, :]

    dt = dt_buf[...]
    da = dt * (-jnp.exp(alog_ref[...]))
    acum = sum(_dot(tri, part) for part in _split3(da))
    e_acum = jnp.exp(acum)
    w_end = jnp.exp(acum[q - 1:q, :] - acum) * dt
    acum_t = acum.T
    dt_t = dt.T

    ssq = jnp.zeros((q, 1), F32)
    for g in range(N_GROUPS):
        b_g = xc_buf[:, D_INNER + g * D_STATE:D_INNER + (g + 1) * D_STATE]
        c_off = D_INNER + N_GROUPS * D_STATE + g * D_STATE
        c_b = xc_buf[:, c_off:c_off + D_STATE].astype(BF16)
        cb = _dot_nt(c_b, b_g.astype(BF16))
        ht_g = ht_ref[g]
        y_inter = _dot(c_b, ht_g.astype(BF16))
        xw = []
        e_last = []
        for pair in range(HEADS_PER_GROUP // 2):
            h0 = g * HEADS_PER_GROUP + 2 * pair
            h1 = h0 + 1
            pcols = slice(h0 * HEAD_DIM, (h0 + 2) * HEAD_DIM)
            x_pair = xc_buf[:, pcols]
            ms = []
            for h in (h0, h1):
                seg = acum[:, h:h + 1] - acum_t[h:h + 1, :]
                decay = jnp.exp(jnp.where(causal, seg, -jnp.inf))
                ms.append((cb * decay * dt_t[h:h + 1, :]).astype(BF16))
            x_top = jnp.where(lo_half, x_pair, 0.0).astype(BF16)
            x_bot = jnp.where(lo_half, 0.0, x_pair).astype(BF16)
            y_intra = _dot(jnp.concatenate(ms, axis=1), jnp.concatenate([x_top, x_bot], axis=0))
            e_pair = jnp.where(lo_half, e_acum[:, h0:h0 + 1], e_acum[:, h1:h1 + 1])
            y_pair = (y_intra + y_inter[:, pair * LANES:(pair + 1) * LANES] * e_pair
                      + dskip_ref[:, pcols] * x_pair)
            gated = y_pair * _silu(z_buf[:, pcols])
            y_buf[:, pcols] = gated
            ssq = ssq + jnp.sum(gated * gated, axis=-1, keepdims=True)
            w_pair = jnp.where(lo_half, w_end[:, h0:h0 + 1], w_end[:, h1:h1 + 1])
            xw.append((x_pair * w_pair).astype(BF16))
            e_last.append(jnp.where(lo_half, e_acum[q - 1:q, h0:h0 + 1], e_acum[q - 1:q, h1:h1 + 1]))
            yield
        upd = _dot(b_g.T.astype(BF16), jnp.concatenate(xw, axis=1))
        ht_ref[g] = ht_g * jnp.concatenate(e_last, axis=1) + upd

    inv = lax.rsqrt(ssq * (1.0 / D_INNER) + EPS)
    for c0 in range(0, D_INNER, 1024):
        cols = slice(c0, c0 + 1024)
        ya = (y_buf[:, cols] * inv * sn_ref[:, cols]).astype(BF16)
        ya_ref[rows, cols] = _dot(to_nat, ya).astype(BF16)
        yield

    u_buf[1:q, :] = u_buf[POOL_HALO:POOL_HALO + q - 1, :]
    for j in range(1, TILES):
        u_buf[j * SUBLANES:j * SUBLANES + 1, :] = ptail[j - 1:j, :]
    pos = tile_in_seq * q + t_col
    yb_parts = []
    for gi, w in enumerate(POOL_WINDOWS):
        cols = slice(gi * POOL_GROUP_DIM, (gi + 1) * POOL_GROUP_DIM)
        u_g = u_buf[POOL_HALO:POOL_HALO + q, cols]
        s = u_g
        for k in range(1, w):
            s = s + u_buf[POOL_HALO - k * SUBLANES:POOL_HALO - k * SUBLANES + q, cols]
        cnt = jnp.minimum(pos + 1, w).astype(F32)
        yb_parts.append(_pool_mix(s, u_g, cnt, wp_ref, ps_ref, gi))
        yield
    yb_ref[rows, :] = _dot(to_nat, jnp.concatenate(yb_parts, axis=1)).astype(BF16)
    for j in range(1, TILES):
        src = POOL_HALO + j * SUBLANES + SUBLANES - 1
        ptail[j - 1:j, :] = u_buf[src:src + 1, :]


def _mix_kernel(x0_ref, x1_ref, x2_ref, gpre_ref, w_ref, cw_ref, cb_ref, dtb_ref, alog_ref, dskip_ref, wp_ref,
                ps_ref, sn_ref,
                ya_ref, yb_ref, nconv_ref, nssm_ref, npool_ref,
                xbc0, z0, u0, dt0, xbc1, z1, u1, dt1, xc0, y0, xc1, y1, ctail, ptail, ht_ref, *, tiles_per_seq):
    q = CHUNK
    k = pl.program_id(0)
    tile_in_seq = lax.rem(2 * k, tiles_per_seq)
    row = lax.broadcasted_iota(jnp.int32, (q, q), 0)
    col = lax.broadcasted_iota(jnp.int32, (q, q), 1)
    to_perm = jnp.where(col == _time_of_row(row), 1.0, 0.0).astype(BF16)
    set0 = (xbc0, z0, u0, dt0)
    set1 = (xbc1, z1, u1, dt1)
    state = (ctail, ptail, ht_ref)
    consts = (cw_ref, cb_ref, alog_ref, dskip_ref, wp_ref, ps_ref, sn_ref)

    @pl.when(k == 0)
    def _():
        _run(_project(x0_ref, gpre_ref, w_ref, dtb_ref, to_perm, *set0))

    @pl.when(tile_in_seq == 0)
    def _():
        ctail[...] = jnp.zeros(ctail.shape, F32)
        ptail[...] = jnp.zeros(ptail.shape, F32)
        ht_ref[...] = jnp.zeros(ht_ref.shape, F32)

    _interleave(_sequence(*set0, xc0, y0, *state, *consts, ya_ref, yb_ref, slice(0, q), tile_in_seq),
                _project(x1_ref, gpre_ref, w_ref, dtb_ref, to_perm, *set1), 2)
    _interleave(_sequence(*set1, xc1, y1, *state, *consts, ya_ref, yb_ref, slice(q, 2 * q), tile_in_seq + 1),
                _project(x2_ref, gpre_ref, w_ref, dtb_ref, to_perm, *set0), 2)

    @pl.when(tile_in_seq == tiles_per_seq - 2)
    def _():
        nconv_ref[0] = ctail[0:CONV_WIDTH - 1, :]
        npool_ref[0] = ptail[0:POOL_BUF, :]
        for g in range(N_GROUPS):
            nssm_ref[0, g * HEADS_PER_GROUP:(g + 1) * HEADS_PER_GROUP] = (
                ht_ref[g].T.reshape(HEADS_PER_GROUP, HEAD_DIM, D_STATE))


def _mix_prompt(x, gpre, w, cw, cb, dtb, alog, dskip_e, wp, ps, sn, bsz, seq):
    q = CHUNK
    nt = seq // q
    n_tiles = bsz * nt
    assert seq % (2 * q) == 0
    steps_per_seq = nt // 2
    m = bsz * seq
    consts = (gpre, w, cw, cb, dtb, alog, dskip_e, wp, ps, sn)
    pair = lambda k: (k, 0)
    per_b = lambda k: (k // steps_per_seq, 0, 0)
    proj_set = [pltpu.VMEM((CONV_HALO + q, CONV_DIM), F32), pltpu.VMEM((q, D_INNER), F32),
                pltpu.VMEM((POOL_HALO + q, POOL_DIM), F32), pltpu.VMEM((q, DT_PAD), F32)]
    seq_set = [pltpu.VMEM((q, CONV_DIM), F32), pltpu.VMEM((q, D_INNER), F32)]
    return pl.pallas_call(
        functools.partial(_mix_kernel, tiles_per_seq=nt),
        grid=(n_tiles // 2,),
        in_specs=[pl.BlockSpec((q, D_MODEL), lambda k: (0, 0), pipeline_mode=pl.Buffered(1)),
                  pl.BlockSpec((q, D_MODEL), lambda k: (2 * k + 1, 0)),
                  pl.BlockSpec((q, D_MODEL), lambda k: (jnp.minimum(2 * k + 2, n_tiles - 1), 0))]
                 + [_const_spec(a.shape) for a in consts],
        out_specs=[pl.BlockSpec((2 * q, D_INNER), pair),
                   pl.BlockSpec((2 * q, POOL_DIM), pair),
                   pl.BlockSpec((1, CONV_WIDTH - 1, CONV_DIM), per_b),
                   pl.BlockSpec((1, N_HEADS, HEAD_DIM, D_STATE), lambda k: (k // steps_per_seq, 0, 0, 0)),
                   pl.BlockSpec((1, POOL_BUF, POOL_DIM), per_b)],
        out_shape=[jax.ShapeDtypeStruct((m, D_INNER), BF16),
                   jax.ShapeDtypeStruct((m, POOL_DIM), BF16),
                   jax.ShapeDtypeStruct((bsz, CONV_WIDTH - 1, CONV_DIM), F32),
                   jax.ShapeDtypeStruct((bsz, N_HEADS, HEAD_DIM, D_STATE), F32),
                   jax.ShapeDtypeStruct((bsz, POOL_BUF, POOL_DIM), F32)],
        scratch_shapes=proj_set + proj_set + seq_set + seq_set + [
            pltpu.VMEM((SUBLANES, CONV_DIM), F32),
            pltpu.VMEM((2 * SUBLANES, POOL_DIM), F32),
            pltpu.VMEM((N_GROUPS, D_STATE, GROUP_COLS), F32)],
        compiler_params=pltpu.CompilerParams(dimension_semantics=("arbitrary",),
                                             vmem_limit_bytes=VMEM_LIMIT),
        name="mix_prompt",
    )(x, x, x, *consts)


def _in_proj_kernel(x_ref, g_ref, w_ref, z_ref, xbc_ref, u_ref, dt_ref):
    xn = _rms(x_ref[...], g_ref[...]).astype(BF16)
    off = 0
    for o_ref in (z_ref, xbc_ref, u_ref, dt_ref):
        width = o_ref.shape[1]
        for c in range(0, width, 1024):
            cw = min(1024, width - c)
            o_ref[:, c:c + cw] = _dot(xn, w_ref[:, off + c:off + c + cw])
        off += width


def _in_proj(x, g, w):
    m = x.shape[0]
    widths = (D_INNER, CONV_DIM, POOL_DIM, DT_PAD)
    return pl.pallas_call(
        _in_proj_kernel,
        grid=(1,),
        in_specs=[pl.BlockSpec((m, D_MODEL), lambda i: (0, 0)), _const_spec(g.shape), _const_spec(w.shape)],
        out_specs=[pl.BlockSpec((m, wd), lambda i: (0, 0)) for wd in widths],
        out_shape=[jax.ShapeDtypeStruct((m, wd), F32) for wd in widths],
        compiler_params=pltpu.CompilerParams(dimension_semantics=("arbitrary",),
                                             vmem_limit_bytes=VMEM_LIMIT),
        name="in_proj_sample",
    )(x, g, w)


def _sample_tok_kernel(xbc_ref, dt_ref, u_ref, sconv_ref, spool_ref, cw_ref, cb_ref, dtb_ref, alog_ref,
                       wp_ref, ps_ref,
                       xc_ref, dec_ref, dtx_ref, nconv_ref, npool_ref, yb_ref):
    for c0 in range(0, CONV_DIM, 512):
        cols = slice(c0, c0 + 512)
        taps = [sconv_ref[:, k * CONV_DIM + c0:k * CONV_DIM + c0 + 512] for k in range(CONV_WIDTH - 1)]
        taps.append(xbc_ref[:, cols])
        xc_ref[:, cols] = _conv_silu(taps, cw_ref, cb_ref, cols)
    nconv_ref[:, 0:(CONV_WIDTH - 2) * CONV_DIM] = sconv_ref[:, CONV_DIM:]
    nconv_ref[:, (CONV_WIDTH - 2) * CONV_DIM:] = xbc_ref[...]

    dt = _softplus(dt_ref[...] + dtb_ref[...])
    dec_ref[...] = jnp.exp(dt * (-jnp.exp(alog_ref[...])))
    hrow = lax.broadcasted_iota(jnp.int32, (DT_PAD, D_INNER), 0)
    ccol = lax.broadcasted_iota(jnp.int32, (DT_PAD, D_INNER), 1)
    expand = jnp.where((ccol >= hrow * HEAD_DIM) & (ccol < (hrow + 1) * HEAD_DIM), 1.0, 0.0).astype(BF16)
    dt_e = sum(_dot(part, expand) for part in _split3(dt))
    dtx_ref[...] = dt_e * xc_ref[:, 0:D_INNER]

    cnt_base = PAST_LEN + 1
    for gi, w in enumerate(POOL_WINDOWS):
        cols = slice(gi * POOL_GROUP_DIM, (gi + 1) * POOL_GROUP_DIM)
        u_g = u_ref[:, cols]
        s = u_g
        for k in range(1, w):
            o = (POOL_BUF - k) * POOL_DIM + gi * POOL_GROUP_DIM
            s = s + spool_ref[:, o:o + POOL_GROUP_DIM]
        yb_ref[:, cols] = _pool_mix(s, u_g, float(min(cnt_base, w)), wp_ref, ps_ref, gi)
    npool_ref[:, 0:(POOL_BUF - 1) * POOL_DIM] = spool_ref[:, POOL_DIM:]
    npool_ref[:, (POOL_BUF - 1) * POOL_DIM:] = u_ref[...]


def _sample_tok(xbc, dt, u, sconv, spool, cw, cb, dtb, alog, wp, ps):
    nb = xbc.shape[0]
    full = lambda a: pl.BlockSpec(a.shape, lambda i: (0,) * a.ndim)
    args = (xbc, dt, u, sconv, spool, cw, cb, dtb, alog, wp, ps)
    out_shape = [jax.ShapeDtypeStruct((nb, CONV_DIM), F32),
                 jax.ShapeDtypeStruct((nb, DT_PAD), F32),
                 jax.ShapeDtypeStruct((nb, D_INNER), F32),
                 jax.ShapeDtypeStruct(sconv.shape, F32),
                 jax.ShapeDtypeStruct(spool.shape, F32),
                 jax.ShapeDtypeStruct((nb, POOL_DIM), BF16)]
    return pl.pallas_call(
        _sample_tok_kernel,
        grid=(1,),
        in_specs=[full(a) for a in args],
        out_specs=[pl.BlockSpec(s.shape, lambda i: (0, 0)) for s in out_shape],
        out_shape=out_shape,
        compiler_params=pltpu.CompilerParams(dimension_semantics=("arbitrary",),
                                             vmem_limit_bytes=VMEM_LIMIT),
        name="sample_tok",
    )(*args)


def _sample_ssm_kernel(st_ref, b_ref, c_ref, dec_ref, dtx_ref, xs_ref, z_ref, dskip_ref, sn_ref, ya_ref, nst_ref):
    grow = lax.broadcasted_iota(jnp.int32, (N_GROUPS, D_INNER), 0)
    gcol = lax.broadcasted_iota(jnp.int32, (N_GROUPS, D_INNER), 1)
    gmask = (gcol >= grow * GROUP_COLS) & (gcol < (grow + 1) * GROUP_COLS)
    dtx = jnp.where(gmask, dtx_ref[0], 0.0)
    x_hi, x_mid, _ = _split3(dtx)
    b_hi, b_mid, _ = _split3(b_ref[0])
    lhs = jnp.concatenate([x_hi.astype(F32), x_mid.astype(F32), x_hi.astype(F32)], axis=0)
    rhs = jnp.concatenate([b_hi.astype(F32), b_hi.astype(F32), b_mid.astype(F32)], axis=0)
    upd = lax.dot_general(lhs, rhs, (((0,), (0,)), ((), ())), preferred_element_type=F32)
    dec = dec_ref[0]
    c_b = c_ref[0].astype(BF16)
    y_parts = []
    for g in range(N_GROUPS):
        new_g = []
        for r in range(HEADS_PER_GROUP):
            h = g * HEADS_PER_GROUP + r
            scale = jnp.broadcast_to(dec[:, h:h + 1], (HEAD_DIM, D_STATE))
            new = st_ref[0, h] * scale + upd[h * HEAD_DIM:(h + 1) * HEAD_DIM, :]
            nst_ref[0, h] = new
            new_g.append(new.astype(BF16))
        y_g = _dot_nt(c_b, jnp.concatenate(new_g, axis=0))
        gcols = slice(g * GROUP_COLS, (g + 1) * GROUP_COLS)
        y_parts.append(y_g[g:g + 1, :] + dskip_ref[:, gcols] * xs_ref[0, :, gcols])
    y = jnp.concatenate(y_parts, axis=1)
    ya_ref[0] = _rms(y * _silu(z_ref[0]), sn_ref[...])


def _sample_ssm(state, b3, c3, dec3, dtx3, xs3, z3, dskip_e, sn):
    nb = state.shape[0]
    tok3 = lambda i: (i, 0, 0)
    row3 = pl.BlockSpec((1, 1, D_INNER), tok3)
    return pl.pallas_call(
        _sample_ssm_kernel,
        grid=(nb,),
        in_specs=[pl.BlockSpec((1, N_HEADS, HEAD_DIM, D_STATE), lambda i: (i, 0, 0, 0)),
                  pl.BlockSpec((1, N_GROUPS, D_STATE), tok3),
                  pl.BlockSpec((1, N_GROUPS, D_STATE), tok3),
                  pl.BlockSpec((1, 1, DT_PAD), tok3),
                  row3, row3, row3,
                  _const_spec(dskip_e.shape), _const_spec(sn.shape)],
        out_specs=[row3,
                   pl.BlockSpec((1, N_HEADS, HEAD_DIM, D_STATE), lambda i: (i, 0, 0, 0))],
        out_shape=[jax.ShapeDtypeStruct((nb, 1, D_INNER), F32),
                   jax.ShapeDtypeStruct(state.shape, F32)],
        compiler_params=pltpu.CompilerParams(dimension_semantics=("arbitrary",),
                                             vmem_limit_bytes=VMEM_LIMIT),
        name="sample_ssm",
    )(state, b3, c3, dec3, dtx3, xs3, z3, dskip_e, sn)


def _out_kernel(x_ref, ya_ref, yb_ref, gpre_ref, wg_ref, wa_ref, wb_ref, wo_ref, gpost_ref,
                fpre_ref, w1_ref, w2_ref, fpost_ref, o_ref):
    x = x_ref[...]
    xn = _rms(x, gpre_ref[...]).astype(BF16)
    ya = _dot(ya_ref[...].astype(BF16), wa_ref[...])
    yb = _dot(yb_ref[...], wb_ref[...])
    merged = (jax.nn.sigmoid(_dot(xn, wg_ref[:, 0:D_MODEL])) * ya
              + jax.nn.sigmoid(_dot(xn, wg_ref[:, D_MODEL:])) * yb)
    mo = _dot(merged.astype(BF16), wo_ref[...])
    h = x + _rms(mo, gpost_ref[...])
    hn = _rms(h, fpre_ref[...]).astype(BF16)
    gate = _dot(hn, w1_ref[:, 0:D_FF])
    up = _dot(hn, w1_ref[:, D_FF:])
    f = _dot((_silu(gate) * up).astype(BF16), w2_ref[...])
    o_ref[...] = h + _rms(f, fpost_ref[...])


def _out(x, ya, yb, consts, tm):
    m = x.shape[0]
    tok = lambda i: (i, 0)
    return pl.pallas_call(
        _out_kernel,
        grid=(m // tm,),
        in_specs=[pl.BlockSpec((tm, D_MODEL), tok), pl.BlockSpec((tm, D_INNER), tok),
                  pl.BlockSpec((tm, POOL_DIM), tok)] + [_const_spec(a.shape) for a in consts],
        out_specs=pl.BlockSpec((tm, D_MODEL), tok),
        out_shape=jax.ShapeDtypeStruct((m, D_MODEL), F32),
        compiler_params=pltpu.CompilerParams(dimension_semantics=("arbitrary",),
                                             vmem_limit_bytes=VMEM_LIMIT),
        name="out",
    )(x, ya, yb, *consts)


def _token_tile(m, cap):
    t = min(m, cap)
    assert m % t == 0
    return t


def kernel(x_prompt, x_sample, state_conv, state_ssm, state_pool, norm_mix_pre, norm_mix_post, norm_ffn_pre,
           norm_ffn_post, w_in, conv_w, conv_b, dt_bias, a_log, d_skip, ssm_norm, w_pool_group, pool_scale,
           w_branch_a, w_branch_b, w_out, w_ffn_in, w_ffn_out):
    bsz, seq, _ = x_prompt.shape
    nb, dec_seq, _ = x_sample.shape
    assert dec_seq == 1 and norm_mix_pre.shape[0] == 1
    l = 0
    o_dt = D_INNER + CONV_DIM
    o_u = o_dt + N_HEADS
    o_g = o_u + POOL_DIM
    wi = w_in[l]
    w_mix = jnp.concatenate(
        [wi[:, :o_dt], wi[:, o_u:o_g], jnp.pad(wi[:, o_dt:o_u], ((0, 0), (0, DT_PAD - N_HEADS)))], axis=1).astype(BF16)
    w_gate = wi[:, o_g:].astype(BF16)
    pad_h = lambda v: jnp.pad(v.astype(F32), (0, DT_PAD - N_HEADS)).reshape(1, DT_PAD)
    row = lambda v: v.astype(F32).reshape(1, -1)
    dtb, alog = pad_h(dt_bias[l]), pad_h(a_log[l])
    dskip_e = jnp.repeat(d_skip[l].astype(F32), HEAD_DIM).reshape(1, D_INNER)
    cw, cb = conv_w[l].astype(F32), row(conv_b[l])
    wp, ps = w_pool_group[l].astype(BF16), row(pool_scale[l])
    g_pre, sn = row(norm_mix_pre[l]), row(ssm_norm[l])
    out_consts = (g_pre, w_gate, w_branch_a[l].astype(BF16), w_branch_b[l].astype(BF16), w_out[l].astype(BF16),
                  row(norm_mix_post[l]), row(norm_ffn_pre[l]), w_ffn_in[l].astype(BF16), w_ffn_out[l].astype(BF16),
                  row(norm_ffn_post[l]))

    xp = x_prompt.reshape(bsz * seq, D_MODEL)
    ya, yb, nconv_p, nssm_p, npool_p = _mix_prompt(xp, g_pre, w_mix, cw, cb, dtb, alog, dskip_e, wp, ps, sn,
                                                   bsz, seq)
    out_p = _out(xp, ya, yb, out_consts, _token_tile(bsz * seq, 256)).reshape(bsz, seq, D_MODEL)

    xs2 = x_sample.reshape(nb, D_MODEL)
    z, xbc, u, dt = _in_proj(xs2, g_pre, w_mix)
    sconv = state_conv[l].reshape(nb, (CONV_WIDTH - 1) * CONV_DIM)
    spool = state_pool[l].reshape(nb, POOL_BUF * POOL_DIM)
    xc, dec, dtx, nconv_s, npool_s, yb_s = _sample_tok(xbc, dt, u, sconv, spool, cw, cb, dtb, alog, wp, ps)
    b3 = xc[:, D_INNER:D_INNER + N_GROUPS * D_STATE].reshape(nb, N_GROUPS, D_STATE)
    c3 = xc[:, D_INNER + N_GROUPS * D_STATE:].reshape(nb, N_GROUPS, D_STATE)
    xs3 = xc[:, :D_INNER].reshape(nb, 1, D_INNER)
    ya3, nssm_s = _sample_ssm(state_ssm[l], b3, c3, dec.reshape(nb, 1, DT_PAD), dtx.reshape(nb, 1, D_INNER),
                              xs3, z.reshape(nb, 1, D_INNER), dskip_e, sn)
    out_s = _out(xs2, ya3.reshape(nb, D_INNER), yb_s, out_consts, _token_tile(nb, 256)).reshape(nb, 1, D_MODEL)

    return (out_p, out_s,
            nconv_p[None], nssm_p[None], npool_p[None],
            nconv_s.reshape(1, nb, CONV_WIDTH - 1, CONV_DIM), nssm_s[None],
            npool_s.reshape(1, nb, POOL_BUF, POOL_DIM))
```

```python
import functools

import jax
import jax.numpy as jnp
from jax import lax
from jax.experimental import pallas as pl
from jax.experimental.pallas import tpu as pltpu

D_MODEL = 1024
D_INNER = 2048
HEAD_DIM = 64
N_HEADS = 32
N_GROUPS = 8
HEADS_PER_GROUP = 4
D_STATE = 128
CONV_WIDTH = 4
CONV_DIM = 4096
CHUNK = 128
POOL_DIM = 1024
POOL_WINDOWS = (2, 4, 8, 16)
POOL_GROUP_DIM = 256
POOL_BUF = 15
D_FF = 2816
EPS = 1e-6
PAST_LEN = 16384

LANES = 128
SUBLANES = 8
TILES = CHUNK // SUBLANES
GROUP_COLS = HEADS_PER_GROUP * HEAD_DIM
DT_PAD = LANES
MIX_WIDTH = D_INNER + CONV_DIM + POOL_DIM + DT_PAD
CONV_HALO = (CONV_WIDTH - 1) * SUBLANES
POOL_HALO = CHUNK
VMEM_LIMIT = 56 * 1024 * 1024

F32 = jnp.float32
BF16 = jnp.bfloat16


def _rms(x, g):
    y = x * lax.rsqrt(jnp.mean(x * x, axis=-1, keepdims=True) + EPS)
    return y * g


def _silu(x):
    return x * jax.nn.sigmoid(x)


def _softplus(x):
    return jnp.maximum(x, 0.0) + jnp.log1p(jnp.exp(-jnp.abs(x)))


def _split3(x):
    hi = x.astype(BF16)
    r = x - hi.astype(F32)
    mid = r.astype(BF16)
    lo = (r - mid.astype(F32)).astype(BF16)
    return hi, mid, lo


def _dot(a, b):
    return jnp.dot(a, b, preferred_element_type=F32)


def _dot_nt(a, b):
    return lax.dot_general(a, b, (((1,), (1,)), ((), ())), preferred_element_type=F32)


def _const_spec(shape):
    nd = len(shape)
    return pl.BlockSpec(shape, lambda *_: (0,) * nd)


_HBM_SPEC = pl.BlockSpec(memory_space=pl.ANY)


def _load_resident(pairs, sem):
    copies = [pltpu.make_async_copy(src, dst, sem.at[i]) for i, (src, dst) in enumerate(pairs)]
    for c in copies:
        c.start()
    for c in copies:
        c.wait()


def _time_of_row(r):
    return (r & (SUBLANES - 1)) * TILES + (r >> 3)


def _conv_silu(taps, cw_ref, cb_ref, cols):
    acc = cb_ref[:, cols]
    for k, t in enumerate(taps):
        acc = acc + t * cw_ref[k:k + 1, cols]
    return _silu(acc)


def _pool_mix(win_sum, u, cnt, wp_ref, ps_ref, gi):
    cols = slice(gi * POOL_GROUP_DIM, (gi + 1) * POOL_GROUP_DIM)
    d = win_sum / cnt - u
    mixed = _dot(d.astype(BF16), wp_ref[gi])
    return (mixed * ps_ref[:, cols]).astype(BF16)


def _project(x_ref, gpre_ref, w_ref, dtb_ref, to_perm, xbc_buf, z_buf, u_buf, dt_buf):
    xn = _rms(x_ref[...], gpre_ref[...]).astype(BF16)
    xnp = _dot(to_perm, xn).astype(BF16)
    yield
    piece = 512
    for c0 in range(0, CONV_DIM, piece):
        xbc_buf[CONV_HALO:, c0:c0 + piece] = _dot(xnp, w_ref[:, D_INNER + c0:D_INNER + c0 + piece])
        yield
    o_u = D_INNER + CONV_DIM
    for c0 in range(0, POOL_DIM, piece):
        u_buf[POOL_HALO:, c0:c0 + piece] = _dot(xnp, w_ref[:, o_u + c0:o_u + c0 + piece])
        yield
    dt_buf[...] = _softplus(_dot(xnp, w_ref[:, MIX_WIDTH - DT_PAD:MIX_WIDTH]) + dtb_ref[...])
    for c0 in range(0, D_INNER, piece):
        z_buf[:, c0:c0 + piece] = _dot(xnp, w_ref[:, c0:c0 + piece])
        yield


def _run(gen):
    for _ in gen:
        pass


def _interleave(main, side, main_per_side):
    side_live = True
    while True:
        for _ in range(main_per_side):
            if next(main, StopIteration) is StopIteration:
                if side_live:
                    _run(side)
                return
        if side_live and next(side, StopIteration) is StopIteration:
            side_live = False


def _sequence(xbc_buf, z_buf, u_buf, dt_buf, xc_buf, y_buf, ctail, ptail, ht_ref,
              cw_ref, cb_ref, alog_ref, dskip_ref, wp_ref, ps_ref, sn_ref, ya_ref, yb_ref, rows, tile_in_seq):
    q = CHUNK
    row = lax.broadcasted_iota(jnp.int32, (q, q), 0)
    col = lax.broadcasted_iota(jnp.int32, (q, q), 1)
    to_nat = jnp.where(_time_of_row(col) == row, 1.0, 0.0).astype(BF16)
    causal = _time_of_row(row) >= _time_of_row(col)
    tri = jnp.where(causal, 1.0, 0.0).astype(BF16)
    lo_half = col < HEAD_DIM
    t_col = _time_of_row(lax.broadcasted_iota(jnp.int32, (q, 1), 0))

    for n, j in enumerate(range(TILES - (CONV_WIDTH - 1), TILES)):
        src = CONV_HALO + j * SUBLANES
        xbc_buf[n * SUBLANES + 1:(n + 1) * SUBLANES, :] = xbc_buf[src:src + SUBLANES - 1, :]
        xbc_buf[n * SUBLANES:n * SUBLANES + 1, :] = ctail[n:n + 1, :]
    for c0 in range(0, CONV_DIM, 512):
        cols = slice(c0, c0 + 512)
        taps = []
        for k in range(CONV_WIDTH):
            start = CONV_HALO - (CONV_WIDTH - 1 - k) * SUBLANES
            taps.append(xbc_buf[start:start + q, cols])
        xc_buf[:, cols] = _conv_silu(taps, cw_ref, cb_ref, cols)
        yield
    for n, j in enumerate(range(TILES - (CONV_WIDTH - 1), TILES)):
        src = CONV_HALO + j * SUBLANES + SUBLANES - 1
        ctail[n:n + 1, :] = xbc_buf[src:src + 1, :]

    dt = dt_buf[...]
    da = dt * (-jnp.exp(alog_ref[...]))
    acum = sum(_dot(tri, part) for part in _split3(da))
    e_acum = jnp.exp(acum)
    w_end = jnp.exp(acum[q - 1:q, :] - acum) * dt
    acum_t = acum.T
    dt_t = dt.T

    ssq = jnp.zeros((q, 1), F32)
    for g in range(N_GROUPS):
        b_g = xc_buf[:, D_INNER + g * D_STATE:D_INNER + (g + 1) * D_STATE]
        c_off = D_INNER + N_GROUPS * D_STATE + g * D_STATE
        c_b = xc_buf[:, c_off:c_off + D_STATE].astype(BF16)
        cb = _dot_nt(c_b, b_g.astype(BF16))
        ht_g = ht_ref[g]
        y_inter = _dot(c_b, ht_g.astype(BF16))
        xw = []
        e_last = []
        for pair in range(HEADS_PER_GROUP // 2):
            h0 = g * HEADS_PER_GROUP + 2 * pair
            h1 = h0 + 1
            pcols = slice(h0 * HEAD_DIM, (h0 + 2) * HEAD_DIM)
            x_pair = xc_buf[:, pcols]
            ms = []
            for h in (h0, h1):
                seg = acum[:, h:h + 1] - acum_t[h:h + 1, :]
                decay = jnp.exp(jnp.where(causal, seg, -jnp.inf))
                ms.append((cb * decay * dt_t[h:h + 1, :]).astype(BF16))
            x_top = jnp.where(lo_half, x_pair, 0.0).astype(BF16)
            x_bot = jnp.where(lo_half, 0.0, x_pair).astype(BF16)
            y_intra = _dot(jnp.concatenate(ms, axis=1), jnp.concatenate([x_top, x_bot], axis=0))
            e_pair = jnp.where(lo_half, e_acum[:, h0:h0 + 1], e_acum[:, h1:h1 + 1])
            y_pair = (y_intra + y_inter[:, pair * LANES:(pair + 1) * LANES] * e_pair
                      + dskip_ref[:, pcols] * x_pair)
            gated = y_pair * _silu(z_buf[:, pcols])
            y_buf[:, pcols] = gated
            ssq = ssq + jnp.sum(gated * gated, axis=-1, keepdims=True)
            w_pair = jnp.where(lo_half, w_end[:, h0:h0 + 1], w_end[:, h1:h1 + 1])
            xw.append((x_pair * w_pair).astype(BF16))
            e_last.append(jnp.where(lo_half, e_acum[q - 1:q, h0:h0 + 1], e_acum[q - 1:q, h1:h1 + 1]))
            yield
        upd = _dot(b_g.T.astype(BF16), jnp.concatenate(xw, axis=1))
        ht_ref[g] = ht_g * jnp.concatenate(e_last, axis=1) + upd

    inv = lax.rsqrt(ssq * (1.0 / D_INNER) + EPS)
    for c0 in range(0, D_INNER, 1024):
        cols = slice(c0, c0 + 1024)
        ya = (y_buf[:, cols] * inv * sn_ref[:, cols]).astype(BF16)
        ya_ref[rows, cols] = _dot(to_nat, ya).astype(BF16)
        yield

    u_buf[1:q, :] = u_buf[POOL_HALO:POOL_HALO + q - 1, :]
    for j in range(1, TILES):
        u_buf[j * SUBLANES:j * SUBLANES + 1, :] = ptail[j - 1:j, :]
    pos = tile_in_seq * q + t_col
    yb_parts = []
    for gi, w in enumerate(POOL_WINDOWS):
        cols = slice(gi * POOL_GROUP_DIM, (gi + 1) * POOL_GROUP_DIM)
        u_g = u_buf[POOL_HALO:POOL_HALO + q, cols]
        s = u_g
        for k in range(1, w):
            s = s + u_buf[POOL_HALO - k * SUBLANES:POOL_HALO - k * SUBLANES + q, cols]
        cnt = jnp.minimum(pos + 1, w).astype(F32)
        yb_parts.append(_pool_mix(s, u_g, cnt, wp_ref, ps_ref, gi))
        yield
    yb_ref[rows, :] = _dot(to_nat, jnp.concatenate(yb_parts, axis=1)).astype(BF16)
    for j in range(1, TILES):
        src = POOL_HALO + j * SUBLANES + SUBLANES - 1
        ptail[j - 1:j, :] = u_buf[src:src + 1, :]


def _mix_kernel(x0_ref, x1_ref, x2_ref, gpre_ref, w_hbm, cw_ref, cb_ref, dtb_ref, alog_ref, dskip_ref, wp_hbm,
                ps_ref, sn_ref,
                ya_ref, yb_ref, nconv_ref, nssm_ref, npool_ref,
                xbc0, z0, u0, dt0, xbc1, z1, u1, dt1, xc0, y0, xc1, y1, ctail, ptail, ht_ref,
                w_ref, wp_ref, wsem, *, tiles_per_seq):
    q = CHUNK
    k = pl.program_id(0)
    tile_in_seq = lax.rem(2 * k, tiles_per_seq)
    row = lax.broadcasted_iota(jnp.int32, (q, q), 0)
    col = lax.broadcasted_iota(jnp.int32, (q, q), 1)
    to_perm = jnp.where(col == _time_of_row(row), 1.0, 0.0).astype(BF16)
    set0 = (xbc0, z0, u0, dt0)
    set1 = (xbc1, z1, u1, dt1)
    state = (ctail, ptail, ht_ref)
    consts = (cw_ref, cb_ref, alog_ref, dskip_ref, wp_ref, ps_ref, sn_ref)

    @pl.when(k == 0)
    def _():
        _load_resident([(w_hbm, w_ref), (wp_hbm, wp_ref)], wsem)
        _run(_project(x0_ref, gpre_ref, w_ref, dtb_ref, to_perm, *set0))

    @pl.when(tile_in_seq == 0)
    def _():
        ctail[...] = jnp.zeros(ctail.shape, F32)
        ptail[...] = jnp.zeros(ptail.shape, F32)
        ht_ref[...] = jnp.zeros(ht_ref.shape, F32)

    _interleave(_sequence(*set0, xc0, y0, *state, *consts, ya_ref, yb_ref, slice(0, q), tile_in_seq),
                _project(x1_ref, gpre_ref, w_ref, dtb_ref, to_perm, *set1), 2)
    _interleave(_sequence(*set1, xc1, y1, *state, *consts, ya_ref, yb_ref, slice(q, 2 * q), tile_in_seq + 1),
                _project(x2_ref, gpre_ref, w_ref, dtb_ref, to_perm, *set0), 2)

    @pl.when(tile_in_seq == tiles_per_seq - 2)
    def _():
        nconv_ref[0] = ctail[0:CONV_WIDTH - 1, :]
        npool_ref[0] = ptail[0:POOL_BUF, :]
        for g in range(N_GROUPS):
            nssm_ref[0, g * HEADS_PER_GROUP:(g + 1) * HEADS_PER_GROUP] = (
                ht_ref[g].T.reshape(HEADS_PER_GROUP, HEAD_DIM, D_STATE))


def _mix_prompt(x, gpre, w, cw, cb, dtb, alog, dskip_e, wp, ps, sn, bsz, seq):
    q = CHUNK
    nt = seq // q
    n_tiles = bsz * nt
    assert seq % (2 * q) == 0
    steps_per_seq = nt // 2
    m = bsz * seq
    consts = (gpre, w, cw, cb, dtb, alog, dskip_e, wp, ps, sn)
    pair = lambda k: (k, 0)
    per_b = lambda k: (k // steps_per_seq, 0, 0)
    proj_set = [pltpu.VMEM((CONV_HALO + q, CONV_DIM), F32), pltpu.VMEM((q, D_INNER), F32),
                pltpu.VMEM((POOL_HALO + q, POOL_DIM), F32), pltpu.VMEM((q, DT_PAD), F32)]
    seq_set = [pltpu.VMEM((q, CONV_DIM), F32), pltpu.VMEM((q, D_INNER), F32)]
    return pl.pallas_call(
        functools.partial(_mix_kernel, tiles_per_seq=nt),
        grid=(n_tiles // 2,),
        in_specs=[pl.BlockSpec((q, D_MODEL), lambda k: (0, 0)),
                  pl.BlockSpec((q, D_MODEL), lambda k: (2 * k + 1, 0)),
                  pl.BlockSpec((q, D_MODEL), lambda k: (jnp.minimum(2 * k + 2, n_tiles - 1), 0))]
                 + [_HBM_SPEC if a is w or a is wp else _const_spec(a.shape) for a in consts],
        out_specs=[pl.BlockSpec((2 * q, D_INNER), pair),
                   pl.BlockSpec((2 * q, POOL_DIM), pair),
                   pl.BlockSpec((1, CONV_WIDTH - 1, CONV_DIM), per_b),
                   pl.BlockSpec((1, N_HEADS, HEAD_DIM, D_STATE), lambda k: (k // steps_per_seq, 0, 0, 0)),
                   pl.BlockSpec((1, POOL_BUF, POOL_DIM), per_b)],
        out_shape=[jax.ShapeDtypeStruct((m, D_INNER), BF16),
                   jax.ShapeDtypeStruct((m, POOL_DIM), BF16),
                   jax.ShapeDtypeStruct((bsz, CONV_WIDTH - 1, CONV_DIM), F32),
                   jax.ShapeDtypeStruct((bsz, N_HEADS, HEAD_DIM, D_STATE), F32),
                   jax.ShapeDtypeStruct((bsz, POOL_BUF, POOL_DIM), F32)],
        scratch_shapes=proj_set + proj_set + seq_set + seq_set + [
            pltpu.VMEM((SUBLANES, CONV_DIM), F32),
            pltpu.VMEM((2 * SUBLANES, POOL_DIM), F32),
            pltpu.VMEM((N_GROUPS, D_STATE, GROUP_COLS), F32),
            pltpu.VMEM(w.shape, BF16), pltpu.VMEM(wp.shape, BF16), pltpu.SemaphoreType.DMA((2,))],
        compiler_params=pltpu.CompilerParams(dimension_semantics=("arbitrary",),
                                             vmem_limit_bytes=VMEM_LIMIT),
        name="mix_prompt",
    )(x, x, x, *consts)


def _in_proj_kernel(x_ref, g_ref, w_ref, z_ref, xbc_ref, u_ref, dt_ref):
    xn = _rms(x_ref[...], g_ref[...]).astype(BF16)
    off = 0
    for o_ref in (z_ref, xbc_ref, u_ref, dt_ref):
        width = o_ref.shape[1]
        for c in range(0, width, 1024):
            cw = min(1024, width - c)
            o_ref[:, c:c + cw] = _dot(xn, w_ref[:, off + c:off + c + cw])
        off += width


def _in_proj(x, g, w):
    m = x.shape[0]
    widths = (D_INNER, CONV_DIM, POOL_DIM, DT_PAD)
    return pl.pallas_call(
        _in_proj_kernel,
        grid=(1,),
        in_specs=[pl.BlockSpec((m, D_MODEL), lambda i: (0, 0)), _const_spec(g.shape), _const_spec(w.shape)],
        out_specs=[pl.BlockSpec((m, wd), lambda i: (0, 0)) for wd in widths],
        out_shape=[jax.ShapeDtypeStruct((m, wd), F32) for wd in widths],
        compiler_params=pltpu.CompilerParams(dimension_semantics=("arbitrary",),
                                             vmem_limit_bytes=VMEM_LIMIT),
        name="in_proj_sample",
    )(x, g, w)


def _sample_tok_kernel(xbc_ref, dt_ref, u_ref, sconv_ref, spool_ref, cw_ref, cb_ref, dtb_ref, alog_ref,
                       wp_ref, ps_ref,
                       xc_ref, dec_ref, dtx_ref, nconv_ref, npool_ref, yb_ref):
    for c0 in range(0, CONV_DIM, 512):
        cols = slice(c0, c0 + 512)
        taps = [sconv_ref[:, k * CONV_DIM + c0:k * CONV_DIM + c0 + 512] for k in range(CONV_WIDTH - 1)]
        taps.append(xbc_ref[:, cols])
        xc_ref[:, cols] = _conv_silu(taps, cw_ref, cb_ref, cols)
    nconv_ref[:, 0:(CONV_WIDTH - 2) * CONV_DIM] = sconv_ref[:, CONV_DIM:]
    nconv_ref[:, (CONV_WIDTH - 2) * CONV_DIM:] = xbc_ref[...]

    dt = _softplus(dt_ref[...] + dtb_ref[...])
    dec_ref[...] = jnp.exp(dt * (-jnp.exp(alog_ref[...])))
    hrow = lax.broadcasted_iota(jnp.int32, (DT_PAD, D_INNER), 0)
    ccol = lax.broadcasted_iota(jnp.int32, (DT_PAD, D_INNER), 1)
    expand = jnp.where((ccol >= hrow * HEAD_DIM) & (ccol < (hrow + 1) * HEAD_DIM), 1.0, 0.0).astype(BF16)
    dt_e = sum(_dot(part, expand) for part in _split3(dt))
    dtx_ref[...] = dt_e * xc_ref[:, 0:D_INNER]

    cnt_base = PAST_LEN + 1
    for gi, w in enumerate(POOL_WINDOWS):
        cols = slice(gi * POOL_GROUP_DIM, (gi + 1) * POOL_GROUP_DIM)
        u_g = u_ref[:, cols]
        s = u_g
        for k in range(1, w):
            o = (POOL_BUF - k) * POOL_DIM + gi * POOL_GROUP_DIM
            s = s + spool_ref[:, o:o + POOL_GROUP_DIM]
        yb_ref[:, cols] = _pool_mix(s, u_g, float(min(cnt_base, w)), wp_ref, ps_ref, gi)
    npool_ref[:, 0:(POOL_BUF - 1) * POOL_DIM] = spool_ref[:, POOL_DIM:]
    npool_ref[:, (POOL_BUF - 1) * POOL_DIM:] = u_ref[...]


def _sample_tok(xbc, dt, u, sconv, spool, cw, cb, dtb, alog, wp, ps):
    nb = xbc.shape[0]
    full = lambda a: pl.BlockSpec(a.shape, lambda i: (0,) * a.ndim)
    args = (xbc, dt, u, sconv, spool, cw, cb, dtb, alog, wp, ps)
    out_shape = [jax.ShapeDtypeStruct((nb, CONV_DIM), F32),
                 jax.ShapeDtypeStruct((nb, DT_PAD), F32),
                 jax.ShapeDtypeStruct((nb, D_INNER), F32),
                 jax.ShapeDtypeStruct(sconv.shape, F32),
                 jax.ShapeDtypeStruct(spool.shape, F32),
                 jax.ShapeDtypeStruct((nb, POOL_DIM), BF16)]
    return pl.pallas_call(
        _sample_tok_kernel,
        grid=(1,),
        in_specs=[full(a) for a in args],
        out_specs=[pl.BlockSpec(s.shape, lambda i: (0, 0)) for s in out_shape],
        out_shape=out_shape,
        compiler_params=pltpu.CompilerParams(dimension_semantics=("arbitrary",),
                                             vmem_limit_bytes=VMEM_LIMIT),
        name="sample_tok",
    )(*args)


def _sample_ssm_kernel(st_ref, b_ref, c_ref, dec_ref, dtx_ref, xs_ref, z_ref, dskip_ref, sn_ref, ya_ref, nst_ref):
    grow = lax.broadcasted_iota(jnp.int32, (N_GROUPS, D_INNER), 0)
    gcol = lax.broadcasted_iota(jnp.int32, (N_GROUPS, D_INNER), 1)
    gmask = (gcol >= grow * GROUP_COLS) & (gcol < (grow + 1) * GROUP_COLS)
    dtx = jnp.where(gmask, dtx_ref[0], 0.0)
    x_hi, x_mid, _ = _split3(dtx)
    b_hi, b_mid, _ = _split3(b_ref[0])
    lhs = jnp.concatenate([x_hi.astype(F32), x_mid.astype(F32), x_hi.astype(F32)], axis=0)
    rhs = jnp.concatenate([b_hi.astype(F32), b_hi.astype(F32), b_mid.astype(F32)], axis=0)
    upd = lax.dot_general(lhs, rhs, (((0,), (0,)), ((), ())), preferred_element_type=F32)
    dec = dec_ref[0]
    c_b = c_ref[0].astype(BF16)
    y_parts = []
    for g in range(N_GROUPS):
        new_g = []
        for r in range(HEADS_PER_GROUP):
            h = g * HEADS_PER_GROUP + r
            scale = jnp.broadcast_to(dec[:, h:h + 1], (HEAD_DIM, D_STATE))
            new = st_ref[0, h] * scale + upd[h * HEAD_DIM:(h + 1) * HEAD_DIM, :]
            nst_ref[0, h] = new
            new_g.append(new.astype(BF16))
        y_g = _dot_nt(c_b, jnp.concatenate(new_g, axis=0))
        gcols = slice(g * GROUP_COLS, (g + 1) * GROUP_COLS)
        y_parts.append(y_g[g:g + 1, :] + dskip_ref[:, gcols] * xs_ref[0, :, gcols])
    y = jnp.concatenate(y_parts, axis=1)
    ya_ref[0] = _rms(y * _silu(z_ref[0]), sn_ref[...])


def _sample_ssm(state, b3, c3, dec3, dtx3, xs3, z3, dskip_e, sn):
    nb = state.shape[0]
    tok3 = lambda i: (i, 0, 0)
    row3 = pl.BlockSpec((1, 1, D_INNER), tok3)
    return pl.pallas_call(
        _sample_ssm_kernel,
        grid=(nb,),
        in_specs=[pl.BlockSpec((1, N_HEADS, HEAD_DIM, D_STATE), lambda i: (i, 0, 0, 0)),
                  pl.BlockSpec((1, N_GROUPS, D_STATE), tok3),
                  pl.BlockSpec((1, N_GROUPS, D_STATE), tok3),
                  pl.BlockSpec((1, 1, DT_PAD), tok3),
                  row3, row3, row3,
                  _const_spec(dskip_e.shape), _const_spec(sn.shape)],
        out_specs=[row3,
                   pl.BlockSpec((1, N_HEADS, HEAD_DIM, D_STATE), lambda i: (i, 0, 0, 0))],
        out_shape=[jax.ShapeDtypeStruct((nb, 1, D_INNER), F32),
                   jax.ShapeDtypeStruct(state.shape, F32)],
        compiler_params=pltpu.CompilerParams(dimension_semantics=("arbitrary",),
                                             vmem_limit_bytes=VMEM_LIMIT),
        name="sample_ssm",
    )(state, b3, c3, dec3, dtx3, xs3, z3, dskip_e, sn)


def _out_kernel(x_ref, ya_ref, yb_ref, gpre_ref, wg_hbm, wa_hbm, wb_hbm, wo_hbm, gpost_ref,
                fpre_ref, w1_hbm, w2_hbm, fpost_ref, o_ref,
                wg_ref, wa_ref, wb_ref, wo_ref, w1_ref, w2_ref, wsem):
    @pl.when(pl.program_id(0) == 0)
    def _():
        _load_resident([(wg_hbm, wg_ref), (wa_hbm, wa_ref), (wb_hbm, wb_ref), (wo_hbm, wo_ref),
                        (w1_hbm, w1_ref), (w2_hbm, w2_ref)], wsem)

    x = x_ref[...]
    xn = _rms(x, gpre_ref[...]).astype(BF16)
    ya = _dot(ya_ref[...].astype(BF16), wa_ref[...])
    yb = _dot(yb_ref[...], wb_ref[...])
    merged = (jax.nn.sigmoid(_dot(xn, wg_ref[:, 0:D_MODEL])) * ya
              + jax.nn.sigmoid(_dot(xn, wg_ref[:, D_MODEL:])) * yb)
    mo = _dot(merged.astype(BF16), wo_ref[...])
    h = x + _rms(mo, gpost_ref[...])
    hn = _rms(h, fpre_ref[...]).astype(BF16)
    gate = _dot(hn, w1_ref[:, 0:D_FF])
    up = _dot(hn, w1_ref[:, D_FF:])
    f = _dot((_silu(gate) * up).astype(BF16), w2_ref[...])
    o_ref[...] = h + _rms(f, fpost_ref[...])


def _out(x, ya, yb, consts, tm):
    m = x.shape[0]
    tok = lambda i: (i, 0)
    weights = [a for a in consts if a.dtype == BF16]
    return pl.pallas_call(
        _out_kernel,
        grid=(m // tm,),
        in_specs=[pl.BlockSpec((tm, D_MODEL), tok), pl.BlockSpec((tm, D_INNER), tok),
                  pl.BlockSpec((tm, POOL_DIM), tok)]
                 + [_HBM_SPEC if a.dtype == BF16 else _const_spec(a.shape) for a in consts],
        out_specs=pl.BlockSpec((tm, D_MODEL), tok),
        out_shape=jax.ShapeDtypeStruct((m, D_MODEL), F32),
        scratch_shapes=[pltpu.VMEM(a.shape, BF16) for a in weights] + [pltpu.SemaphoreType.DMA((len(weights),))],
        compiler_params=pltpu.CompilerParams(dimension_semantics=("arbitrary",),
                                             vmem_limit_bytes=VMEM_LIMIT),
        name="out",
    )(x, ya, yb, *consts)


def _token_tile(m, cap):
    t = min(m, cap)
    assert m % t == 0
    return t


def kernel(x_prompt, x_sample, state_conv, state_ssm, state_pool, norm_mix_pre, norm_mix_post, norm_ffn_pre,
           norm_ffn_post, w_in, conv_w, conv_b, dt_bias, a_log, d_skip, ssm_norm, w_pool_group, pool_scale,
           w_branch_a, w_branch_b, w_out, w_ffn_in, w_ffn_out):
    bsz, seq, _ = x_prompt.shape
    nb, dec_seq, _ = x_sample.shape
    assert dec_seq == 1 and norm_mix_pre.shape[0] == 1
    l = 0
    o_dt = D_INNER + CONV_DIM
    o_u = o_dt + N_HEADS
    o_g = o_u + POOL_DIM
    wi = w_in[l]
    w_mix = jnp.concatenate(
        [wi[:, :o_dt], wi[:, o_u:o_g], jnp.pad(wi[:, o_dt:o_u], ((0, 0), (0, DT_PAD - N_HEADS)))], axis=1).astype(BF16)
    w_gate = wi[:, o_g:].astype(BF16)
    pad_h = lambda v: jnp.pad(v.astype(F32), (0, DT_PAD - N_HEADS)).reshape(1, DT_PAD)
    row = lambda v: v.astype(F32).reshape(1, -1)
    dtb, alog = pad_h(dt_bias[l]), pad_h(a_log[l])
    dskip_e = jnp.repeat(d_skip[l].astype(F32), HEAD_DIM).reshape(1, D_INNER)
    cw, cb = conv_w[l].astype(F32), row(conv_b[l])
    wp, ps = w_pool_group[l].astype(BF16), row(pool_scale[l])
    g_pre, sn = row(norm_mix_pre[l]), row(ssm_norm[l])
    out_consts = (g_pre, w_gate, w_branch_a[l].astype(BF16), w_branch_b[l].astype(BF16), w_out[l].astype(BF16),
                  row(norm_mix_post[l]), row(norm_ffn_pre[l]), w_ffn_in[l].astype(BF16), w_ffn_out[l].astype(BF16),
                  row(norm_ffn_post[l]))

    xp = x_prompt.reshape(bsz * seq, D_MODEL)
    ya, yb, nconv_p, nssm_p, npool_p = _mix_prompt(xp, g_pre, w_mix, cw, cb, dtb, alog, dskip_e, wp, ps, sn,
                                                   bsz, seq)
    out_p = _out(xp, ya, yb, out_consts, _token_tile(bsz * seq, 256)).reshape(bsz, seq, D_MODEL)

    xs2 = x_sample.reshape(nb, D_MODEL)
    z, xbc, u, dt = _in_proj(xs2, g_pre, w_mix)
    sconv = state_conv[l].reshape(nb, (CONV_WIDTH - 1) * CONV_DIM)
    spool = state_pool[l].reshape(nb, POOL_BUF * POOL_DIM)
    xc, dec, dtx, nconv_s, npool_s, yb_s = _sample_tok(xbc, dt, u, sconv, spool, cw, cb, dtb, alog, wp, ps)
    b3 = xc[:, D_INNER:D_INNER + N_GROUPS * D_STATE].reshape(nb, N_GROUPS, D_STATE)
    c3 = xc[:, D_INNER + N_GROUPS * D_STATE:].reshape(nb, N_GROUPS, D_STATE)
    xs3 = xc[:, :D_INNER].reshape(nb, 1, D_INNER)
    ya3, nssm_s = _sample_ssm(state_ssm[l], b3, c3, dec.reshape(nb, 1, DT_PAD), dtx.reshape(nb, 1, D_INNER),
                              xs3, z.reshape(nb, 1, D_INNER), dskip_e, sn)
    out_s = _out(xs2, ya3.reshape(nb, D_INNER), yb_s, out_consts, _token_tile(nb, 256)).reshape(nb, 1, D_MODEL)

    return (out_p, out_s,
            nconv_p[None], nssm_p[None], npool_p[None],
            nconv_s.reshape(1, nb, CONV_WIDTH - 1, CONV_DIM), nssm_s[None],
            npool_s.reshape(1, nb, POOL_BUF, POOL_DIM))
```

```python
import functools

import jax
import jax.numpy as jnp
from jax import lax
from jax.experimental import pallas as pl
from jax.experimental.pallas import tpu as pltpu

D_MODEL = 1024
D_INNER = 2048
HEAD_DIM = 64
N_HEADS = 32
N_GROUPS = 8
HEADS_PER_GROUP = 4
D_STATE = 128
CONV_WIDTH = 4
CONV_DIM = 4096
CHUNK = 128
POOL_DIM = 1024
POOL_WINDOWS = (2, 4, 8, 16)
POOL_GROUP_DIM = 256
POOL_BUF = 15
D_FF = 2816
EPS = 1e-6
PAST_LEN = 16384
LOG2E = 1.4426950408889634

LANES = 128
SUBLANES = 8
TILES = CHUNK // SUBLANES
GROUP_COLS = HEADS_PER_GROUP * HEAD_DIM
DT_PAD = LANES
MIX_WIDTH = D_INNER + CONV_DIM + POOL_DIM + DT_PAD
CONV_HALO = (CONV_WIDTH - 1) * SUBLANES
POOL_HALO = CHUNK
VMEM_LIMIT = 56 * 1024 * 1024

F32 = jnp.float32
BF16 = jnp.bfloat16


def _rms(x, g):
    y = x * lax.rsqrt(jnp.mean(x * x, axis=-1, keepdims=True) + EPS)
    return y * g


def _silu(x):
    return x * jax.nn.sigmoid(x)


def _softplus(x):
    return jnp.maximum(x, 0.0) + jnp.log(1.0 + jnp.exp(-jnp.abs(x)))


def _split3(x):
    hi = x.astype(BF16)
    r = x - hi.astype(F32)
    mid = r.astype(BF16)
    lo = (r - mid.astype(F32)).astype(BF16)
    return hi, mid, lo


def _dot(a, b):
    return jnp.dot(a, b, preferred_element_type=F32)


def _dot_nt(a, b):
    return lax.dot_general(a, b, (((1,), (1,)), ((), ())), preferred_element_type=F32)


def _const_spec(shape):
    nd = len(shape)
    return pl.BlockSpec(shape, lambda *_: (0,) * nd)


_HBM_SPEC = pl.BlockSpec(memory_space=pl.ANY)


def _load_resident(pairs, sem):
    copies = [pltpu.make_async_copy(src, dst, sem.at[i]) for i, (src, dst) in enumerate(pairs)]
    for c in copies:
        c.start()
    for c in copies:
        c.wait()


def _time_of_row(r):
    return (r & (SUBLANES - 1)) * TILES + (r >> 3)


def _conv_silu(taps, cw_ref, cb_ref, cols):
    acc = cb_ref[:, cols]
    for k, t in enumerate(taps):
        acc = acc + t * cw_ref[k:k + 1, cols]
    return _silu(acc)


def _pool_mix(win_sum, u, cnt, wp_ref, ps_ref, gi):
    cols = slice(gi * POOL_GROUP_DIM, (gi + 1) * POOL_GROUP_DIM)
    d = win_sum / cnt - u
    mixed = _dot(d.astype(BF16), wp_ref[gi])
    return (mixed * ps_ref[:, cols]).astype(BF16)


def _project(x_ref, gpre_ref, w_ref, dtb_ref, to_perm, xnp_buf, xbc_buf, z_buf, u_buf, dt_buf):
    xn = _rms(x_ref[...], gpre_ref[...]).astype(BF16)
    xnp_buf[...] = _dot(to_perm, xn).astype(BF16)
    yield

    def proj(c0, width):
        tile = 256
        return jnp.concatenate([_dot(xnp_buf[...], w_ref[:, c:c + min(tile, c0 + width - c)])
                                for c in range(c0, c0 + width, tile)], axis=1)

    piece = 512
    for c0 in range(0, CONV_DIM, piece):
        xbc_buf[CONV_HALO:, c0:c0 + piece] = proj(D_INNER + c0, piece)
        yield
    o_u = D_INNER + CONV_DIM
    for c0 in range(0, POOL_DIM, piece):
        u_buf[POOL_HALO:, c0:c0 + piece] = proj(o_u + c0, piece)
        yield
    dt_buf[...] = _softplus(proj(MIX_WIDTH - DT_PAD, DT_PAD) + dtb_ref[...])
    for c0 in range(0, D_INNER, piece):
        z_buf[:, c0:c0 + piece] = proj(c0, piece)
        yield


def _run(gen):
    for _ in gen:
        pass


N_CONV_PIECES = CONV_DIM // 512


def _interleave(main, side, side_after):
    next(side)
    i = 0
    while next(main, StopIteration) is not StopIteration:
        for _ in range(side_after(i)):
            next(side, None)
        i += 1
    _run(side)


def _project_pieces_after(i):
    if i < N_CONV_PIECES:
        return 1
    return 1 if (i - N_CONV_PIECES) % 2 == 0 else 0


def _sequence(xbc_buf, z_buf, u_buf, dt_buf, xc_buf, y_buf, ctail, ptail, ht_ref,
              cw_ref, cb_ref, alog_ref, dskip_ref, wp_ref, ps_ref, sn_ref, masks, ya_ref, yb_ref, rows, tile_in_seq):
    q = CHUNK
    to_nat, causal, tri, lo_half, t_col = masks

    for n, j in enumerate(range(TILES - (CONV_WIDTH - 1), TILES)):
        src = CONV_HALO + j * SUBLANES
        xbc_buf[n * SUBLANES + 1:(n + 1) * SUBLANES, :] = xbc_buf[src:src + SUBLANES - 1, :]
        xbc_buf[n * SUBLANES:n * SUBLANES + 1, :] = ctail[n:n + 1, :]
    for c0 in range(0, CONV_DIM, 512):
        cols = slice(c0, c0 + 512)
        taps = []
        for k in range(CONV_WIDTH):
            start = CONV_HALO - (CONV_WIDTH - 1 - k) * SUBLANES
            taps.append(xbc_buf[start:start + q, cols])
        xc = _conv_silu(taps, cw_ref, cb_ref, cols)
        xc_buf[:, cols] = xc
        yield
    for n, j in enumerate(range(TILES - (CONV_WIDTH - 1), TILES)):
        src = CONV_HALO + j * SUBLANES + SUBLANES - 1
        ctail[n:n + 1, :] = xbc_buf[src:src + 1, :]

    dt = dt_buf[...]
    da = dt * (-jnp.exp(alog_ref[...]))
    acum = sum(_dot(tri, part) for part in _split3(da)) * LOG2E
    acum_last = acum[q - 1:q, :]
    e_last = jnp.exp2(acum_last)
    w_end_t = (jnp.exp2(acum_last - acum) * dt).T
    acum_t = acum.T
    dt_t = dt.T
    lo_half_row = lo_half[0:1, :]

    ssq = jnp.zeros((q, 1), F32)
    for g in range(N_GROUPS):
        b_g = xc_buf[:, D_INNER + g * D_STATE:D_INNER + (g + 1) * D_STATE]
        c_off = D_INNER + N_GROUPS * D_STATE + g * D_STATE
        c_b = xc_buf[:, c_off:c_off + D_STATE].astype(BF16)
        cb = _dot_nt(c_b, b_g.astype(BF16))
        b_t = b_g.T
        y_inter = _dot(c_b, ht_ref[g].astype(BF16))
        for pair in range(HEADS_PER_GROUP // 2):
            h0 = g * HEADS_PER_GROUP + 2 * pair
            h1 = h0 + 1
            pcols = slice(h0 * HEAD_DIM, (h0 + 2) * HEAD_DIM)
            scols = slice(pair * LANES, (pair + 1) * LANES)
            x_pair = xc_buf[:, pcols]
            ms, bws, e_cols = [], [], []
            for h in (h0, h1):
                a_col = jnp.broadcast_to(acum[:, h:h + 1], (q, q))
                seg = a_col - acum_t[h:h + 1, :]
                decay = jnp.exp2(jnp.where(causal, seg, -jnp.inf))
                ms.append((cb * decay * dt_t[h:h + 1, :]).astype(BF16))
                bws.append((b_t * w_end_t[h:h + 1, :]).astype(BF16))
                e_cols.append(jnp.exp2(a_col))
            x_top = jnp.where(lo_half, x_pair, 0.0).astype(BF16)
            x_bot = jnp.where(lo_half, 0.0, x_pair).astype(BF16)
            x_diag = jnp.concatenate([x_top, x_bot], axis=0)
            y_intra = _dot(jnp.concatenate(ms, axis=1), x_diag)
            e_pair = jnp.where(lo_half, e_cols[0], e_cols[1])
            y_pair = y_intra + y_inter[:, scols] * e_pair + dskip_ref[:, pcols] * x_pair
            gated = y_pair * _silu(z_buf[:, pcols])
            y_buf[:, pcols] = gated
            ssq = ssq + jnp.sum(gated * gated, axis=-1, keepdims=True)
            upd = _dot(jnp.concatenate(bws, axis=1), x_diag)
            e_last_pair = jnp.where(lo_half_row, e_last[:, h0:h0 + 1], e_last[:, h1:h1 + 1])
            ht_ref[g, :, scols] = ht_ref[g, :, scols] * e_last_pair + upd
            yield

    inv = lax.rsqrt(ssq * (1.0 / D_INNER) + EPS)
    for c0 in range(0, D_INNER, 1024):
        cols = slice(c0, c0 + 1024)
        ya = (y_buf[:, cols] * inv * sn_ref[:, cols]).astype(BF16)
        ya_ref[rows, cols] = _dot(to_nat, ya).astype(BF16)
        yield

    u_buf[1:q, :] = u_buf[POOL_HALO:POOL_HALO + q - 1, :]
    for j in range(1, TILES):
        u_buf[j * SUBLANES:j * SUBLANES + 1, :] = ptail[j - 1:j, :]
    pos = tile_in_seq * q + t_col
    yb_parts = []
    for gi, w in enumerate(POOL_WINDOWS):
        cols = slice(gi * POOL_GROUP_DIM, (gi + 1) * POOL_GROUP_DIM)
        u_g = u_buf[POOL_HALO:POOL_HALO + q, cols]
        s = u_g
        for k in range(1, w):
            s = s + u_buf[POOL_HALO - k * SUBLANES:POOL_HALO - k * SUBLANES + q, cols]
        cnt = jnp.minimum(pos + 1, w).astype(F32)
        yb_parts.append(_pool_mix(s, u_g, cnt, wp_ref, ps_ref, gi))
        yield
    yb_ref[rows, :] = _dot(to_nat, jnp.concatenate(yb_parts, axis=1)).astype(BF16)
    for j in range(1, TILES):
        src = POOL_HALO + j * SUBLANES + SUBLANES - 1
        ptail[j - 1:j, :] = u_buf[src:src + 1, :]


def _mix_kernel(x0_ref, x1_ref, x2_ref, gpre_ref, w_hbm, cw_ref, cb_ref, dtb_ref, alog_ref, dskip_ref,
                wp_hbm, ps_ref, sn_ref,
                ya_ref, yb_ref, nconv_ref, nssm_ref, npool_ref,
                xbc0, z0, u0, dt0, xbc1, z1, u1, dt1, xc0, y0, xc1, y1, ctail, ptail, ht_ref,
                w_ref, wp_ref, wsem, xnp_buf, *, tiles_per_seq):
    q = CHUNK
    k = pl.program_id(0)
    tile_in_seq = lax.rem(2 * k, tiles_per_seq)
    row = lax.broadcasted_iota(jnp.int32, (q, q), 0)
    col = lax.broadcasted_iota(jnp.int32, (q, q), 1)
    to_perm = jnp.where(col == _time_of_row(row), 1.0, 0.0).astype(BF16)
    to_nat = jnp.where(_time_of_row(col) == row, 1.0, 0.0).astype(BF16)
    causal = _time_of_row(row) >= _time_of_row(col)
    tri = jnp.where(causal, 1.0, 0.0).astype(BF16)
    t_col = _time_of_row(lax.broadcasted_iota(jnp.int32, (q, 1), 0))
    masks = (to_nat, causal, tri, col < HEAD_DIM, t_col)
    set0 = (xbc0, z0, u0, dt0)
    set1 = (xbc1, z1, u1, dt1)
    state = (ctail, ptail, ht_ref)
    consts = (cw_ref, cb_ref, alog_ref, dskip_ref, wp_ref, ps_ref, sn_ref, masks)

    @pl.when(k == 0)
    def _():
        _load_resident([(w_hbm, w_ref), (wp_hbm, wp_ref)], wsem)
        _run(_project(x0_ref, gpre_ref, w_ref, dtb_ref, to_perm, xnp_buf, *set0))

    @pl.when(tile_in_seq == 0)
    def _():
        ctail[...] = jnp.zeros(ctail.shape, F32)
        ptail[...] = jnp.zeros(ptail.shape, F32)
        ht_ref[...] = jnp.zeros(ht_ref.shape, F32)

    _interleave(_sequence(*set0, xc0, y0, *state, *consts, ya_ref, yb_ref, slice(0, q), tile_in_seq),
                _project(x1_ref, gpre_ref, w_ref, dtb_ref, to_perm, xnp_buf, *set1), _project_pieces_after)
    _interleave(_sequence(*set1, xc1, y1, *state, *consts, ya_ref, yb_ref, slice(q, 2 * q), tile_in_seq + 1),
                _project(x2_ref, gpre_ref, w_ref, dtb_ref, to_perm, xnp_buf, *set0), _project_pieces_after)

    @pl.when(tile_in_seq == tiles_per_seq - 2)
    def _():
        nconv_ref[0] = ctail[0:CONV_WIDTH - 1, :]
        npool_ref[0] = ptail[0:POOL_BUF, :]
        for g in range(N_GROUPS):
            nssm_ref[0, g * HEADS_PER_GROUP:(g + 1) * HEADS_PER_GROUP] = (
                ht_ref[g].T.reshape(HEADS_PER_GROUP, HEAD_DIM, D_STATE))


def _mix_prompt(x, gpre, w, cw, cb, dtb, alog, dskip_e, wp, ps, sn, bsz, seq):
    q = CHUNK
    nt = seq // q
    n_tiles = bsz * nt
    assert seq % (2 * q) == 0
    steps_per_seq = nt // 2
    m = bsz * seq
    consts = (gpre, w, cw, cb, dtb, alog, dskip_e, wp, ps, sn)
    pair = lambda k: (k, 0)
    per_b = lambda k: (k // steps_per_seq, 0, 0)
    proj_set = [pltpu.VMEM((CONV_HALO + q, CONV_DIM), F32), pltpu.VMEM((q, D_INNER), F32),
                pltpu.VMEM((POOL_HALO + q, POOL_DIM), F32), pltpu.VMEM((q, DT_PAD), F32)]
    seq_set = [pltpu.VMEM((q, CONV_DIM), F32), pltpu.VMEM((q, D_INNER), F32)]
    return pl.pallas_call(
        functools.partial(_mix_kernel, tiles_per_seq=nt),
        grid=(n_tiles // 2,),
        in_specs=[pl.BlockSpec((q, D_MODEL), lambda k: (0, 0)),
                  pl.BlockSpec((q, D_MODEL), lambda k: (2 * k + 1, 0)),
                  pl.BlockSpec((q, D_MODEL), lambda k: (jnp.minimum(2 * k + 2, n_tiles - 1), 0))]
                 + [_HBM_SPEC if a is w or a is wp else _const_spec(a.shape) for a in consts],
        out_specs=[pl.BlockSpec((2 * q, D_INNER), pair),
                   pl.BlockSpec((2 * q, POOL_DIM), pair),
                   pl.BlockSpec((1, CONV_WIDTH - 1, CONV_DIM), per_b),
                   pl.BlockSpec((1, N_HEADS, HEAD_DIM, D_STATE), lambda k: (k // steps_per_seq, 0, 0, 0)),
                   pl.BlockSpec((1, POOL_BUF, POOL_DIM), per_b)],
        out_shape=[jax.ShapeDtypeStruct((m, D_INNER), BF16),
                   jax.ShapeDtypeStruct((m, POOL_DIM), BF16),
                   jax.ShapeDtypeStruct((bsz, CONV_WIDTH - 1, CONV_DIM), F32),
                   jax.ShapeDtypeStruct((bsz, N_HEADS, HEAD_DIM, D_STATE), F32),
                   jax.ShapeDtypeStruct((bsz, POOL_BUF, POOL_DIM), F32)],
        scratch_shapes=proj_set + proj_set + seq_set + seq_set + [
            pltpu.VMEM((SUBLANES, CONV_DIM), F32),
            pltpu.VMEM((2 * SUBLANES, POOL_DIM), F32),
            pltpu.VMEM((N_GROUPS, D_STATE, GROUP_COLS), F32),
            pltpu.VMEM(w.shape, BF16), pltpu.VMEM(wp.shape, BF16), pltpu.SemaphoreType.DMA((2,)),
            pltpu.VMEM((q, D_MODEL), BF16)],
        compiler_params=pltpu.CompilerParams(dimension_semantics=("arbitrary",),
                                             vmem_limit_bytes=VMEM_LIMIT),
        name="mix_prompt",
    )(x, x, x, *consts)


def _in_proj_kernel(x_ref, g_ref, w_ref, z_ref, xbc_ref, u_ref, dt_ref):
    xn = _rms(x_ref[...], g_ref[...]).astype(BF16)
    off = 0
    for o_ref in (z_ref, xbc_ref, u_ref, dt_ref):
        width = o_ref.shape[1]
        for c in range(0, width, 1024):
            cw = min(1024, width - c)
            o_ref[:, c:c + cw] = _dot(xn, w_ref[:, off + c:off + c + cw])
        off += width


def _in_proj(x, g, w):
    m = x.shape[0]
    widths = (D_INNER, CONV_DIM, POOL_DIM, DT_PAD)
    return pl.pallas_call(
        _in_proj_kernel,
        grid=(1,),
        in_specs=[pl.BlockSpec((m, D_MODEL), lambda i: (0, 0)), _const_spec(g.shape), _const_spec(w.shape)],
        out_specs=[pl.BlockSpec((m, wd), lambda i: (0, 0)) for wd in widths],
        out_shape=[jax.ShapeDtypeStruct((m, wd), F32) for wd in widths],
        compiler_params=pltpu.CompilerParams(dimension_semantics=("arbitrary",),
                                             vmem_limit_bytes=VMEM_LIMIT),
        name="in_proj_sample",
    )(x, g, w)


def _sample_tok_kernel(xbc_ref, dt_ref, u_ref, sconv_ref, spool_ref, cw_ref, cb_ref, dtb_ref, alog_ref,
                       wp_ref, ps_ref,
                       xc_ref, dec_ref, dtx_ref, nconv_ref, npool_ref, yb_ref):
    for c0 in range(0, CONV_DIM, 512):
        cols = slice(c0, c0 + 512)
        taps = [sconv_ref[k, :, cols] for k in range(CONV_WIDTH - 1)]
        taps.append(xbc_ref[:, cols])
        xc_ref[:, cols] = _conv_silu(taps, cw_ref, cb_ref, cols)
    for k in range(CONV_WIDTH - 2):
        nconv_ref[k] = sconv_ref[k + 1]
    nconv_ref[CONV_WIDTH - 2] = xbc_ref[...]

    dt = _softplus(dt_ref[...] + dtb_ref[...])
    dec_ref[...] = jnp.exp(dt * (-jnp.exp(alog_ref[...])))
    hrow = lax.broadcasted_iota(jnp.int32, (DT_PAD, D_INNER), 0)
    ccol = lax.broadcasted_iota(jnp.int32, (DT_PAD, D_INNER), 1)
    expand = jnp.where((ccol >= hrow * HEAD_DIM) & (ccol < (hrow + 1) * HEAD_DIM), 1.0, 0.0).astype(BF16)
    dt_e = sum(_dot(part, expand) for part in _split3(dt))
    dtx_ref[...] = dt_e * xc_ref[:, 0:D_INNER]

    cnt_base = PAST_LEN + 1
    for gi, w in enumerate(POOL_WINDOWS):
        cols = slice(gi * POOL_GROUP_DIM, (gi + 1) * POOL_GROUP_DIM)
        u_g = u_ref[:, cols]
        s = u_g
        for k in range(1, w):
            s = s + spool_ref[POOL_BUF - k, :, cols]
        yb_ref[:, cols] = _pool_mix(s, u_g, float(min(cnt_base, w)), wp_ref, ps_ref, gi)
    for k in range(POOL_BUF - 1):
        npool_ref[k] = spool_ref[k + 1]
    npool_ref[POOL_BUF - 1] = u_ref[...]


def _sample_tok(xbc, dt, u, sconv, spool, cw, cb, dtb, alog, wp, ps):
    nb = xbc.shape[0]
    full = lambda a: pl.BlockSpec(a.shape, lambda i: (0,) * a.ndim)
    args = (xbc, dt, u, sconv, spool, cw, cb, dtb, alog, wp, ps)
    out_shape = [jax.ShapeDtypeStruct((nb, CONV_DIM), F32),
                 jax.ShapeDtypeStruct((nb, DT_PAD), F32),
                 jax.ShapeDtypeStruct((nb, D_INNER), F32),
                 jax.ShapeDtypeStruct(sconv.shape, F32),
                 jax.ShapeDtypeStruct(spool.shape, F32),
                 jax.ShapeDtypeStruct((nb, POOL_DIM), BF16)]
    return pl.pallas_call(
        _sample_tok_kernel,
        grid=(1,),
        in_specs=[full(a) for a in args],
        out_specs=[pl.BlockSpec(s.shape, lambda i, nd=len(s.shape): (0,) * nd) for s in out_shape],
        out_shape=out_shape,
        compiler_params=pltpu.CompilerParams(dimension_semantics=("arbitrary",),
                                             vmem_limit_bytes=VMEM_LIMIT),
        name="sample_tok",
    )(*args)


def _sample_ssm_kernel(st_ref, b_ref, c_ref, dec_ref, dtx_ref, xs_ref, z_ref, dskip_ref, sn_ref, ya_ref, nst_ref):
    grow = lax.broadcasted_iota(jnp.int32, (N_GROUPS, D_INNER), 0)
    gcol = lax.broadcasted_iota(jnp.int32, (N_GROUPS, D_INNER), 1)
    gmask = (gcol >= grow * GROUP_COLS) & (gcol < (grow + 1) * GROUP_COLS)
    for t in range(st_ref.shape[0]):
        dtx = jnp.where(gmask, dtx_ref[t], 0.0)
        x_hi, x_mid, _ = _split3(dtx)
        b_hi, b_mid, _ = _split3(b_ref[t])
        lhs = jnp.concatenate([x_hi.astype(F32), x_mid.astype(F32), x_hi.astype(F32)], axis=0)
        rhs = jnp.concatenate([b_hi.astype(F32), b_hi.astype(F32), b_mid.astype(F32)], axis=0)
        upd = lax.dot_general(lhs, rhs, (((0,), (0,)), ((), ())), preferred_element_type=F32)
        dec = dec_ref[t]
        c_b = c_ref[t].astype(BF16)
        y_parts = []
        for g in range(N_GROUPS):
            new_g = []
            for r in range(HEADS_PER_GROUP):
                h = g * HEADS_PER_GROUP + r
                scale = jnp.broadcast_to(dec[:, h:h + 1], (HEAD_DIM, D_STATE))
                new = st_ref[t, h] * scale + upd[h * HEAD_DIM:(h + 1) * HEAD_DIM, :]
                nst_ref[t, h] = new
                new_g.append(new.astype(BF16))
            y_g = _dot_nt(c_b, jnp.concatenate(new_g, axis=0))
            gcols = slice(g * GROUP_COLS, (g + 1) * GROUP_COLS)
            y_parts.append(y_g[g:g + 1, :] + dskip_ref[:, gcols] * xs_ref[t, :, gcols])
        y = jnp.concatenate(y_parts, axis=1)
        ya_ref[t] = _rms(y * _silu(z_ref[t]), sn_ref[...])


def _sample_ssm(state, b3, c3, dec3, dtx3, xs3, z3, dskip_e, sn):
    nb = state.shape[0]
    tb = _token_tile(nb, 4)
    tok3 = lambda i: (i, 0, 0)
    row3 = pl.BlockSpec((tb, 1, D_INNER), tok3)
    state_spec = pl.BlockSpec((tb, N_HEADS, HEAD_DIM, D_STATE), lambda i: (i, 0, 0, 0))
    return pl.pallas_call(
        _sample_ssm_kernel,
        grid=(nb // tb,),
        in_specs=[state_spec,
                  pl.BlockSpec((tb, N_GROUPS, D_STATE), tok3),
                  pl.BlockSpec((tb, N_GROUPS, D_STATE), tok3),
                  pl.BlockSpec((tb, 1, DT_PAD), tok3),
                  row3, row3, row3,
                  _const_spec(dskip_e.shape), _const_spec(sn.shape)],
        out_specs=[row3, state_spec],
        out_shape=[jax.ShapeDtypeStruct((nb, 1, D_INNER), F32),
                   jax.ShapeDtypeStruct(state.shape, F32)],
        compiler_params=pltpu.CompilerParams(dimension_semantics=("arbitrary",),
                                             vmem_limit_bytes=VMEM_LIMIT),
        name="sample_ssm",
    )(state, b3, c3, dec3, dtx3, xs3, z3, dskip_e, sn)


def _out_kernel(x_ref, ya_ref, yb_ref, gpre_ref, wg_hbm, wa_hbm, wb_hbm, wo_hbm, gpost_ref,
                fpre_ref, w1_hbm, w2_hbm, fpost_ref, o_ref,
                wg_ref, wa_ref, wb_ref, wo_ref, w1_ref, w2_ref, wsem):
    @pl.when(pl.program_id(0) == 0)
    def _():
        _load_resident([(wg_hbm, wg_ref), (wa_hbm, wa_ref), (wb_hbm, wb_ref), (wo_hbm, wo_ref),
                        (w1_hbm, w1_ref), (w2_hbm, w2_ref)], wsem)

    x = x_ref[...]
    xn = _rms(x, gpre_ref[...]).astype(BF16)
    ya = _dot(ya_ref[...].astype(BF16), wa_ref[...])
    yb = _dot(yb_ref[...], wb_ref[...])
    merged = (jax.nn.sigmoid(_dot(xn, wg_ref[:, 0:D_MODEL])) * ya
              + jax.nn.sigmoid(_dot(xn, wg_ref[:, D_MODEL:])) * yb)
    mo = _dot(merged.astype(BF16), wo_ref[...])
    h = x + _rms(mo, gpost_ref[...])
    hn = _rms(h, fpre_ref[...]).astype(BF16)
    gate = _dot(hn, w1_ref[:, 0:D_FF])
    up = _dot(hn, w1_ref[:, D_FF:])
    f = _dot((_silu(gate) * up).astype(BF16), w2_ref[...])
    o_ref[...] = h + _rms(f, fpost_ref[...])


def _out(x, ya, yb, consts, tm):
    m = x.shape[0]
    tok = lambda i: (i, 0)
    weights = [a for a in consts if a.dtype == BF16]
    return pl.pallas_call(
        _out_kernel,
        grid=(m // tm,),
        in_specs=[pl.BlockSpec((tm, D_MODEL), tok), pl.BlockSpec((tm, D_INNER), tok),
                  pl.BlockSpec((tm, POOL_DIM), tok)]
                 + [_HBM_SPEC if a.dtype == BF16 else _const_spec(a.shape) for a in consts],
        out_specs=pl.BlockSpec((tm, D_MODEL), tok),
        out_shape=jax.ShapeDtypeStruct((m, D_MODEL), F32),
        scratch_shapes=[pltpu.VMEM(a.shape, BF16) for a in weights] + [pltpu.SemaphoreType.DMA((len(weights),))],
        compiler_params=pltpu.CompilerParams(dimension_semantics=("arbitrary",),
                                             vmem_limit_bytes=VMEM_LIMIT),
        name="out",
    )(x, ya, yb, *consts)


def _token_tile(m, cap):
    t = min(m, cap)
    assert m % t == 0
    return t


def kernel(x_prompt, x_sample, state_conv, state_ssm, state_pool, norm_mix_pre, norm_mix_post, norm_ffn_pre,
           norm_ffn_post, w_in, conv_w, conv_b, dt_bias, a_log, d_skip, ssm_norm, w_pool_group, pool_scale,
           w_branch_a, w_branch_b, w_out, w_ffn_in, w_ffn_out):
    bsz, seq, _ = x_prompt.shape
    nb, dec_seq, _ = x_sample.shape
    assert dec_seq == 1 and norm_mix_pre.shape[0] == 1
    l = 0
    o_dt = D_INNER + CONV_DIM
    o_u = o_dt + N_HEADS
    o_g = o_u + POOL_DIM
    wi = w_in[l]
    w_mix = jnp.concatenate(
        [wi[:, :o_dt], wi[:, o_u:o_g], jnp.pad(wi[:, o_dt:o_u], ((0, 0), (0, DT_PAD - N_HEADS)))], axis=1).astype(BF16)
    w_gate = wi[:, o_g:].astype(BF16)
    pad_h = lambda v: jnp.pad(v.astype(F32), (0, DT_PAD - N_HEADS)).reshape(1, DT_PAD)
    row = lambda v: v.astype(F32).reshape(1, -1)
    dtb, alog = pad_h(dt_bias[l]), pad_h(a_log[l])
    dskip_e = jnp.repeat(d_skip[l].astype(F32), HEAD_DIM).reshape(1, D_INNER)
    cw, cb = conv_w[l].astype(F32), row(conv_b[l])
    wp, ps = w_pool_group[l].astype(BF16), row(pool_scale[l])
    g_pre, sn = row(norm_mix_pre[l]), row(ssm_norm[l])
    out_consts = (g_pre, w_gate, w_branch_a[l].astype(BF16), w_branch_b[l].astype(BF16), w_out[l].astype(BF16),
                  row(norm_mix_post[l]), row(norm_ffn_pre[l]), w_ffn_in[l].astype(BF16), w_ffn_out[l].astype(BF16),
                  row(norm_ffn_post[l]))

    xp = x_prompt.reshape(bsz * seq, D_MODEL)
    ya, yb, nconv_p, nssm_p, npool_p = _mix_prompt(xp, g_pre, w_mix, cw, cb, dtb, alog, dskip_e, wp, ps, sn,
                                                   bsz, seq)
    out_p = _out(xp, ya, yb, out_consts, _token_tile(bsz * seq, 256)).reshape(bsz, seq, D_MODEL)

    xs2 = x_sample.reshape(nb, D_MODEL)
    z, xbc, u, dt = _in_proj(xs2, g_pre, w_mix)
    sconv = jnp.swapaxes(state_conv[l], 0, 1)
    spool = jnp.swapaxes(state_pool[l], 0, 1)
    xc, dec, dtx, nconv_s, npool_s, yb_s = _sample_tok(xbc, dt, u, sconv, spool, cw, cb, dtb, alog, wp, ps)
    b3 = xc[:, D_INNER:D_INNER + N_GROUPS * D_STATE].reshape(nb, N_GROUPS, D_STATE)
    c3 = xc[:, D_INNER + N_GROUPS * D_STATE:].reshape(nb, N_GROUPS, D_STATE)
    xs3 = xc[:, :D_INNER].reshape(nb, 1, D_INNER)
    ya3, nssm_s = _sample_ssm(state_ssm[l], b3, c3, dec.reshape(nb, 1, DT_PAD), dtx.reshape(nb, 1, D_INNER),
                              xs3, z.reshape(nb, 1, D_INNER), dskip_e, sn)
    out_s = _out(xs2, ya3.reshape(nb, D_INNER), yb_s, out_consts, _token_tile(nb, 256)).reshape(nb, 1, D_MODEL)

    return (out_p, out_s,
            nconv_p[None], nssm_p[None], npool_p[None],
            jnp.swapaxes(nconv_s, 0, 1)[None], nssm_s[None],
            jnp.swapaxes(npool_s, 0, 1)[None])
```

```python
import functools

import jax
import jax.numpy as jnp
from jax import lax
from jax.experimental import pallas as pl
from jax.experimental.pallas import tpu as pltpu

D_MODEL = 1024
D_INNER = 2048
HEAD_DIM = 64
N_HEADS = 32
N_GROUPS = 8
HEADS_PER_GROUP = 4
D_STATE = 128
CONV_WIDTH = 4
CONV_DIM = 4096
CHUNK = 128
POOL_DIM = 1024
POOL_WINDOWS = (2, 4, 8, 16)
POOL_GROUP_DIM = 256
POOL_BUF = 15
D_FF = 2816
EPS = 1e-6
PAST_LEN = 16384
LOG2E = 1.4426950408889634

LANES = 128
SUBLANES = 8
TILES = CHUNK // SUBLANES
GROUP_COLS = HEADS_PER_GROUP * HEAD_DIM
DT_PAD = LANES
MIX_WIDTH = D_INNER + CONV_DIM + POOL_DIM + DT_PAD
CONV_HALO = (CONV_WIDTH - 1) * SUBLANES
POOL_HALO = CHUNK
VMEM_LIMIT = 56 * 1024 * 1024
VMEM_LIMIT_OUT = 62 * 1024 * 1024

F32 = jnp.float32
BF16 = jnp.bfloat16


def _rms(x, g):
    y = x * lax.rsqrt(jnp.mean(x * x, axis=-1, keepdims=True) + EPS)
    return y * g


def _silu(x):
    return x * jax.nn.sigmoid(x)


def _softplus(x):
    return jnp.maximum(x, 0.0) + jnp.log(1.0 + jnp.exp(-jnp.abs(x)))


def _split3(x):
    hi = x.astype(BF16)
    r = x - hi.astype(F32)
    mid = r.astype(BF16)
    lo = (r - mid.astype(F32)).astype(BF16)
    return hi, mid, lo


def _dot(a, b):
    return jnp.dot(a, b, preferred_element_type=F32)


def _dot_nt(a, b):
    return lax.dot_general(a, b, (((1,), (1,)), ((), ())), preferred_element_type=F32)


def _const_spec(shape):
    nd = len(shape)
    return pl.BlockSpec(shape, lambda *_: (0,) * nd)


_HBM_SPEC = pl.BlockSpec(memory_space=pl.ANY)


def _load_resident(pairs, sem):
    copies = [pltpu.make_async_copy(src, dst, sem.at[i]) for i, (src, dst) in enumerate(pairs)]
    for c in copies:
        c.start()
    for c in copies:
        c.wait()


STAGE_ROWS = 1024
STAGE_COLS = 640
PREP_COLS = 512
PREP_ROWS = 256


def _prep_weights(chunks, stage, sem):
    def copy(i):
        src, r0, nr, c0, nc, _ = chunks[i]
        return pltpu.make_async_copy(src.at[pl.ds(r0, nr), pl.ds(c0, nc)],
                                     stage.at[i % 2, pl.ds(0, nr), pl.ds(0, nc)], sem.at[i % 2])

    copy(0).start()
    for i, (_, _, nr, _, nc, emit) in enumerate(chunks):
        if i + 1 < len(chunks):
            copy(i + 1).start()
        copy(i).wait()
        for r in range(0, nr, PREP_ROWS):
            emit(slice(r, r + PREP_ROWS), stage[i % 2, r:r + PREP_ROWS, 0:nc])


def _plain_chunks(src, dst):
    rows, cols = src.shape

    def emit_at(r0, c0, nc):
        def emit(rs, v):
            dst[r0 + rs.start:r0 + rs.stop, c0:c0 + nc] = v.astype(BF16)
        return emit

    return [(src, r0, min(STAGE_ROWS, rows - r0), c0, min(PREP_COLS, cols - c0),
             emit_at(r0, c0, min(PREP_COLS, cols - c0)))
            for r0 in range(0, rows, STAGE_ROWS) for c0 in range(0, cols, PREP_COLS)]


IN_O_DT = D_INNER + CONV_DIM
IN_O_POOL = IN_O_DT + N_HEADS
IN_O_GATE = IN_O_POOL + POOL_DIM
IN_DIM = IN_O_GATE + 2 * D_MODEL
IN_SKEW = IN_O_POOL % LANES
IN_TAIL0 = (IN_DIM // LANES) * LANES


def _skewed_chunks(src, src_col0, n_cols, dst, dst_col0, tail=None):
    aligned0 = src_col0 - IN_SKEW
    chunks = []
    for j0 in range(0, n_cols, PREP_COLS):
        win0 = aligned0 + j0
        win = min(PREP_COLS + LANES, IN_TAIL0 - win0)

        def emit(rs, v, j0=j0, win=win):
            got = v[:, IN_SKEW:min(win, IN_SKEW + PREP_COLS)]
            if got.shape[1] < PREP_COLS:
                got = jnp.concatenate([got, tail[rs, 0:PREP_COLS - got.shape[1]]], axis=1)
            dst[rs, dst_col0 + j0:dst_col0 + j0 + PREP_COLS] = got.astype(BF16)

        chunks.append((src, 0, D_MODEL, win0, win, emit))
    return chunks


def _mix_weight_chunks(w_in, w_ref):
    def emit_dt(rs, v):
        lane = lax.broadcasted_iota(jnp.int32, v.shape, 1)
        w_ref[rs, MIX_WIDTH - DT_PAD:MIX_WIDTH] = jnp.where(lane < N_HEADS, v, 0.0).astype(BF16)

    def emit_front(c0):
        def emit(rs, v):
            w_ref[rs, c0:c0 + PREP_COLS] = v.astype(BF16)
        return emit

    chunks = [(w_in, 0, D_MODEL, c0, PREP_COLS, emit_front(c0)) for c0 in range(0, IN_O_DT, PREP_COLS)]
    chunks += _skewed_chunks(w_in, IN_O_POOL, POOL_DIM, w_ref, IN_O_DT)
    chunks.append((w_in, 0, D_MODEL, IN_O_DT, DT_PAD, emit_dt))
    return chunks


def _time_of_row(r):
    return (r & (SUBLANES - 1)) * TILES + (r >> 3)


def _conv_silu(taps, cw_ref, cb_ref, cols):
    acc = cb_ref[:, cols]
    for k, t in enumerate(taps):
        acc = acc + t * cw_ref[k:k + 1, cols]
    return _silu(acc)


def _pool_mix(win_sum, u, cnt, wp_ref, ps_ref, gi):
    cols = slice(gi * POOL_GROUP_DIM, (gi + 1) * POOL_GROUP_DIM)
    d = win_sum / cnt - u
    mixed = _dot(d.astype(BF16), wp_ref[gi])
    return (mixed * ps_ref[:, cols]).astype(BF16)


def _project(x_ref, gpre_ref, w_ref, dtb_ref, to_perm, xnp_buf, xbc_buf, z_buf, u_buf, dt_buf):
    xn = _rms(x_ref[...], gpre_ref[...]).astype(BF16)
    xnp_buf[...] = _dot(to_perm, xn).astype(BF16)
    yield

    def proj(c0, width):
        tile = 256
        return jnp.concatenate([_dot(xnp_buf[...], w_ref[:, c:c + min(tile, c0 + width - c)])
                                for c in range(c0, c0 + width, tile)], axis=1)

    piece = 512
    for c0 in range(0, CONV_DIM, piece):
        xbc_buf[CONV_HALO:, c0:c0 + piece] = proj(D_INNER + c0, piece)
        yield
    o_u = D_INNER + CONV_DIM
    for c0 in range(0, POOL_DIM, piece):
        u_buf[POOL_HALO:, c0:c0 + piece] = proj(o_u + c0, piece)
        yield
    dt_buf[...] = _softplus(proj(MIX_WIDTH - DT_PAD, DT_PAD) + dtb_ref[...])
    for c0 in range(0, D_INNER, piece):
        z_buf[:, c0:c0 + piece] = proj(c0, piece)
        yield


def _run(gen):
    for _ in gen:
        pass


N_CONV_PIECES = CONV_DIM // 512


def _interleave(main, side, side_after):
    next(side)
    i = 0
    while next(main, StopIteration) is not StopIteration:
        for _ in range(side_after(i)):
            next(side, None)
        i += 1
    _run(side)


def _project_pieces_after(i):
    if i < N_CONV_PIECES:
        return 1
    return 1 if (i - N_CONV_PIECES) % 2 == 0 else 0


def _sequence(xbc_buf, z_buf, u_buf, dt_buf, xc_buf, y_buf, ctail, ptail, ht_ref,
              cw_ref, cb_ref, alog_ref, dskip_ref, wp_ref, ps_ref, sn_ref, masks, ya_ref, yb_ref, rows, tile_in_seq):
    q = CHUNK
    to_nat, causal, tri, lo_half, t_col = masks

    for n, j in enumerate(range(TILES - (CONV_WIDTH - 1), TILES)):
        src = CONV_HALO + j * SUBLANES
        xbc_buf[n * SUBLANES + 1:(n + 1) * SUBLANES, :] = xbc_buf[src:src + SUBLANES - 1, :]
        xbc_buf[n * SUBLANES:n * SUBLANES + 1, :] = ctail[n:n + 1, :]
    for c0 in range(0, CONV_DIM, 512):
        cols = slice(c0, c0 + 512)
        taps = []
        for k in range(CONV_WIDTH):
            start = CONV_HALO - (CONV_WIDTH - 1 - k) * SUBLANES
            taps.append(xbc_buf[start:start + q, cols])
        xc = _conv_silu(taps, cw_ref, cb_ref, cols)
        xc_buf[:, cols] = xc
        yield
    for n, j in enumerate(range(TILES - (CONV_WIDTH - 1), TILES)):
        src = CONV_HALO + j * SUBLANES + SUBLANES - 1
        ctail[n:n + 1, :] = xbc_buf[src:src + 1, :]

    dt = dt_buf[...]
    da = dt * (-jnp.exp(alog_ref[...]))
    acum = sum(_dot(tri, part) for part in _split3(da)) * LOG2E
    acum_last = acum[q - 1:q, :]
    e_last = jnp.exp2(acum_last)
    w_end_t = (jnp.exp2(acum_last - acum) * dt).T
    acum_t = acum.T
    dt_t = dt.T
    lo_half_row = lo_half[0:1, :]

    ssq = jnp.zeros((q, 1), F32)
    for g in range(N_GROUPS):
        b_g = xc_buf[:, D_INNER + g * D_STATE:D_INNER + (g + 1) * D_STATE]
        c_off = D_INNER + N_GROUPS * D_STATE + g * D_STATE
        c_b = xc_buf[:, c_off:c_off + D_STATE].astype(BF16)
        cb = _dot_nt(c_b, b_g.astype(BF16))
        b_t = b_g.T
        y_inter = _dot(c_b, ht_ref[g].astype(BF16))
        for pair in range(HEADS_PER_GROUP // 2):
            h0 = g * HEADS_PER_GROUP + 2 * pair
            h1 = h0 + 1
            pcols = slice(h0 * HEAD_DIM, (h0 + 2) * HEAD_DIM)
            scols = slice(pair * LANES, (pair + 1) * LANES)
            x_pair = xc_buf[:, pcols]
            ms, bws, e_cols = [], [], []
            for h in (h0, h1):
                a_col = jnp.broadcast_to(acum[:, h:h + 1], (q, q))
                seg = a_col - acum_t[h:h + 1, :]
                decay = jnp.exp2(jnp.where(causal, seg, -jnp.inf))
                ms.append((cb * decay * dt_t[h:h + 1, :]).astype(BF16))
                bws.append((b_t * w_end_t[h:h + 1, :]).astype(BF16))
                e_cols.append(jnp.exp2(a_col))
            x_top = jnp.where(lo_half, x_pair, 0.0).astype(BF16)
            x_bot = jnp.where(lo_half, 0.0, x_pair).astype(BF16)
            x_diag = jnp.concatenate([x_top, x_bot], axis=0)
            y_intra = _dot(jnp.concatenate(ms, axis=1), x_diag)
            e_pair = jnp.where(lo_half, e_cols[0], e_cols[1])
            y_pair = y_intra + y_inter[:, scols] * e_pair + dskip_ref[:, pcols] * x_pair
            gated = y_pair * _silu(z_buf[:, pcols])
            y_buf[:, pcols] = gated
            ssq = ssq + jnp.sum(gated * gated, axis=-1, keepdims=True)
            upd = _dot(jnp.concatenate(bws, axis=1), x_diag)
            e_last_pair = jnp.where(lo_half_row, e_last[:, h0:h0 + 1], e_last[:, h1:h1 + 1])
            ht_ref[g, :, scols] = ht_ref[g, :, scols] * e_last_pair + upd
            yield

    inv = lax.rsqrt(ssq * (1.0 / D_INNER) + EPS)
    for c0 in range(0, D_INNER, 1024):
        cols = slice(c0, c0 + 1024)
        ya = (y_buf[:, cols] * inv * sn_ref[:, cols]).astype(BF16)
        ya_ref[rows, cols] = _dot(to_nat, ya).astype(BF16)
        yield

    u_buf[1:q, :] = u_buf[POOL_HALO:POOL_HALO + q - 1, :]
    for j in range(1, TILES):
        u_buf[j * SUBLANES:j * SUBLANES + 1, :] = ptail[j - 1:j, :]
    pos = tile_in_seq * q + t_col
    yb_parts = []
    for gi, w in enumerate(POOL_WINDOWS):
        cols = slice(gi * POOL_GROUP_DIM, (gi + 1) * POOL_GROUP_DIM)
        u_g = u_buf[POOL_HALO:POOL_HALO + q, cols]
        s = u_g
        for k in range(1, w):
            s = s + u_buf[POOL_HALO - k * SUBLANES:POOL_HALO - k * SUBLANES + q, cols]
        cnt = jnp.minimum(pos + 1, w).astype(F32)
        yb_parts.append(_pool_mix(s, u_g, cnt, wp_ref, ps_ref, gi))
        yield
    yb_ref[rows, :] = _dot(to_nat, jnp.concatenate(yb_parts, axis=1)).astype(BF16)
    for j in range(1, TILES):
        src = POOL_HALO + j * SUBLANES + SUBLANES - 1
        ptail[j - 1:j, :] = u_buf[src:src + 1, :]


def _mix_kernel(x0_ref, x1_ref, x2_ref, gpre_ref, w_hbm, cw_ref, cb_ref, dtb_ref, alog_ref, dskip_ref,
                wp_hbm, ps_ref, sn_ref,
                ya_ref, yb_ref, nconv_ref, nssm_ref, npool_ref,
                xbc0, z0, u0, dt0, xbc1, z1, u1, dt1, xc0, y0, xc1, y1, ctail, ptail, ht_ref,
                w_ref, wp_ref, wsem, xnp_buf, stage, ssem, *, tiles_per_seq):
    q = CHUNK
    k = pl.program_id(0)
    tile_in_seq = lax.rem(2 * k, tiles_per_seq)
    row = lax.broadcasted_iota(jnp.int32, (q, q), 0)
    col = lax.broadcasted_iota(jnp.int32, (q, q), 1)
    to_perm = jnp.where(col == _time_of_row(row), 1.0, 0.0).astype(BF16)
    to_nat = jnp.where(_time_of_row(col) == row, 1.0, 0.0).astype(BF16)
    causal = _time_of_row(row) >= _time_of_row(col)
    tri = jnp.where(causal, 1.0, 0.0).astype(BF16)
    t_col = _time_of_row(lax.broadcasted_iota(jnp.int32, (q, 1), 0))
    masks = (to_nat, causal, tri, col < HEAD_DIM, t_col)
    set0 = (xbc0, z0, u0, dt0)
    set1 = (xbc1, z1, u1, dt1)
    state = (ctail, ptail, ht_ref)
    consts = (cw_ref, cb_ref, alog_ref, dskip_ref, wp_ref, ps_ref, sn_ref, masks)

    @pl.when(k == 0)
    def _():
        _load_resident([(wp_hbm, wp_ref)], wsem)
        _prep_weights(_mix_weight_chunks(w_hbm, w_ref), stage, ssem)
        _run(_project(x0_ref, gpre_ref, w_ref, dtb_ref, to_perm, xnp_buf, *set0))

    @pl.when(tile_in_seq == 0)
    def _():
        ctail[...] = jnp.zeros(ctail.shape, F32)
        ptail[...] = jnp.zeros(ptail.shape, F32)
        ht_ref[...] = jnp.zeros(ht_ref.shape, F32)

    _interleave(_sequence(*set0, xc0, y0, *state, *consts, ya_ref, yb_ref, slice(0, q), tile_in_seq),
                _project(x1_ref, gpre_ref, w_ref, dtb_ref, to_perm, xnp_buf, *set1), _project_pieces_after)
    _interleave(_sequence(*set1, xc1, y1, *state, *consts, ya_ref, yb_ref, slice(q, 2 * q), tile_in_seq + 1),
                _project(x2_ref, gpre_ref, w_ref, dtb_ref, to_perm, xnp_buf, *set0), _project_pieces_after)

    @pl.when(tile_in_seq == tiles_per_seq - 2)
    def _():
        nconv_ref[0] = ctail[0:CONV_WIDTH - 1, :]
        npool_ref[0] = ptail[0:POOL_BUF, :]
        for g in range(N_GROUPS):
            nssm_ref[0, g * HEADS_PER_GROUP:(g + 1) * HEADS_PER_GROUP] = (
                ht_ref[g].T.reshape(HEADS_PER_GROUP, HEAD_DIM, D_STATE))


def _mix_prompt(x, gpre, w, cw, cb, dtb, alog, dskip_e, wp, ps, sn, bsz, seq):
    q = CHUNK
    nt = seq // q
    n_tiles = bsz * nt
    assert seq % (2 * q) == 0
    steps_per_seq = nt // 2
    m = bsz * seq
    consts = (gpre, w, cw, cb, dtb, alog, dskip_e, wp, ps, sn)
    pair = lambda k: (k, 0)
    per_b = lambda k: (k // steps_per_seq, 0, 0)
    proj_set = [pltpu.VMEM((CONV_HALO + q, CONV_DIM), F32), pltpu.VMEM((q, D_INNER), F32),
                pltpu.VMEM((POOL_HALO + q, POOL_DIM), F32), pltpu.VMEM((q, DT_PAD), F32)]
    seq_set = [pltpu.VMEM((q, CONV_DIM), F32), pltpu.VMEM((q, D_INNER), F32)]
    return pl.pallas_call(
        functools.partial(_mix_kernel, tiles_per_seq=nt),
        grid=(n_tiles // 2,),
        in_specs=[pl.BlockSpec((q, D_MODEL), lambda k: (0, 0)),
                  pl.BlockSpec((q, D_MODEL), lambda k: (2 * k + 1, 0)),
                  pl.BlockSpec((q, D_MODEL), lambda k: (jnp.minimum(2 * k + 2, n_tiles - 1), 0))]
                 + [_HBM_SPEC if a is w or a is wp else _const_spec(a.shape) for a in consts],
        out_specs=[pl.BlockSpec((2 * q, D_INNER), pair),
                   pl.BlockSpec((2 * q, POOL_DIM), pair),
                   pl.BlockSpec((1, CONV_WIDTH - 1, CONV_DIM), per_b),
                   pl.BlockSpec((1, N_HEADS, HEAD_DIM, D_STATE), lambda k: (k // steps_per_seq, 0, 0, 0)),
                   pl.BlockSpec((1, POOL_BUF, POOL_DIM), per_b)],
        out_shape=[jax.ShapeDtypeStruct((m, D_INNER), BF16),
                   jax.ShapeDtypeStruct((m, POOL_DIM), BF16),
                   jax.ShapeDtypeStruct((bsz, CONV_WIDTH - 1, CONV_DIM), F32),
                   jax.ShapeDtypeStruct((bsz, N_HEADS, HEAD_DIM, D_STATE), F32),
                   jax.ShapeDtypeStruct((bsz, POOL_BUF, POOL_DIM), F32)],
        scratch_shapes=proj_set + proj_set + seq_set + seq_set + [
            pltpu.VMEM((SUBLANES, CONV_DIM), F32),
            pltpu.VMEM((2 * SUBLANES, POOL_DIM), F32),
            pltpu.VMEM((N_GROUPS, D_STATE, GROUP_COLS), F32),
            pltpu.VMEM((D_MODEL, MIX_WIDTH), BF16), pltpu.VMEM(wp.shape, BF16), pltpu.SemaphoreType.DMA((1,)),
            pltpu.VMEM((q, D_MODEL), BF16),
            pltpu.VMEM((2, STAGE_ROWS, STAGE_COLS), F32), pltpu.SemaphoreType.DMA((2,))],
        compiler_params=pltpu.CompilerParams(dimension_semantics=("arbitrary",),
                                             vmem_limit_bytes=VMEM_LIMIT),
        name="mix_prompt",
    )(x, x, x, *consts)


def _in_proj_kernel(x_ref, g_ref, w_hbm, z_ref, xbc_ref, u_ref, dt_ref, w_ref, stage, ssem):
    _prep_weights(_mix_weight_chunks(w_hbm, w_ref), stage, ssem)
    xn = _rms(x_ref[...], g_ref[...]).astype(BF16)
    off = 0
    for o_ref in (z_ref, xbc_ref, u_ref, dt_ref):
        width = o_ref.shape[1]
        for c in range(0, width, 1024):
            cw = min(1024, width - c)
            o_ref[:, c:c + cw] = _dot(xn, w_ref[:, off + c:off + c + cw])
        off += width


def _in_proj(x, g, w):
    m = x.shape[0]
    widths = (D_INNER, CONV_DIM, POOL_DIM, DT_PAD)
    return pl.pallas_call(
        _in_proj_kernel,
        grid=(1,),
        in_specs=[pl.BlockSpec((m, D_MODEL), lambda i: (0, 0)), _const_spec(g.shape), _HBM_SPEC],
        out_specs=[pl.BlockSpec((m, wd), lambda i: (0, 0)) for wd in widths],
        out_shape=[jax.ShapeDtypeStruct((m, wd), F32) for wd in widths],
        scratch_shapes=[pltpu.VMEM((D_MODEL, MIX_WIDTH), BF16), pltpu.VMEM((2, STAGE_ROWS, STAGE_COLS), F32),
                        pltpu.SemaphoreType.DMA((2,))],
        compiler_params=pltpu.CompilerParams(dimension_semantics=("arbitrary",),
                                             vmem_limit_bytes=VMEM_LIMIT),
        name="in_proj_sample",
    )(x, g, w)


def _sample_tok_kernel(xbc_ref, dt_ref, u_ref, sconv_ref, spool_ref, cw_ref, cb_ref, dtb_ref, alog_ref,
                       wp_ref, ps_ref,
                       xc_ref, dec_ref, dtx_ref, nconv_ref, npool_ref, yb_ref):
    for c0 in range(0, CONV_DIM, 512):
        cols = slice(c0, c0 + 512)
        taps = [sconv_ref[k, :, cols] for k in range(CONV_WIDTH - 1)]
        taps.append(xbc_ref[:, cols])
        xc_ref[:, cols] = _conv_silu(taps, cw_ref, cb_ref, cols)
    for k in range(CONV_WIDTH - 2):
        nconv_ref[k] = sconv_ref[k + 1]
    nconv_ref[CONV_WIDTH - 2] = xbc_ref[...]

    dt = _softplus(dt_ref[...] + dtb_ref[...])
    dec_ref[...] = jnp.exp(dt * (-jnp.exp(alog_ref[...])))
    hrow = lax.broadcasted_iota(jnp.int32, (DT_PAD, D_INNER), 0)
    ccol = lax.broadcasted_iota(jnp.int32, (DT_PAD, D_INNER), 1)
    expand = jnp.where((ccol >= hrow * HEAD_DIM) & (ccol < (hrow + 1) * HEAD_DIM), 1.0, 0.0).astype(BF16)
    dt_e = sum(_dot(part, expand) for part in _split3(dt))
    dtx_ref[...] = dt_e * xc_ref[:, 0:D_INNER]

    cnt_base = PAST_LEN + 1
    for gi, w in enumerate(POOL_WINDOWS):
        cols = slice(gi * POOL_GROUP_DIM, (gi + 1) * POOL_GROUP_DIM)
        u_g = u_ref[:, cols]
        s = u_g
        for k in range(1, w):
            s = s + spool_ref[POOL_BUF - k, :, cols]
        yb_ref[:, cols] = _pool_mix(s, u_g, float(min(cnt_base, w)), wp_ref, ps_ref, gi)
    for k in range(POOL_BUF - 1):
        npool_ref[k] = spool_ref[k + 1]
    npool_ref[POOL_BUF - 1] = u_ref[...]


def _sample_tok(xbc, dt, u, sconv, spool, cw, cb, dtb, alog, wp, ps):
    nb = xbc.shape[0]
    full = lambda a: pl.BlockSpec(a.shape, lambda i: (0,) * a.ndim)
    args = (xbc, dt, u, sconv, spool, cw, cb, dtb, alog, wp, ps)
    out_shape = [jax.ShapeDtypeStruct((nb, CONV_DIM), F32),
                 jax.ShapeDtypeStruct((nb, DT_PAD), F32),
                 jax.ShapeDtypeStruct((nb, D_INNER), F32),
                 jax.ShapeDtypeStruct(sconv.shape, F32),
                 jax.ShapeDtypeStruct(spool.shape, F32),
                 jax.ShapeDtypeStruct((nb, POOL_DIM), BF16)]
    return pl.pallas_call(
        _sample_tok_kernel,
        grid=(1,),
        in_specs=[full(a) for a in args],
        out_specs=[pl.BlockSpec(s.shape, lambda i, nd=len(s.shape): (0,) * nd) for s in out_shape],
        out_shape=out_shape,
        compiler_params=pltpu.CompilerParams(dimension_semantics=("arbitrary",),
                                             vmem_limit_bytes=VMEM_LIMIT),
        name="sample_tok",
    )(*args)


def _sample_ssm_kernel(st_ref, b_ref, c_ref, dec_ref, dtx_ref, xs_ref, z_ref, dskip_ref, sn_ref, ya_ref, nst_ref):
    grow = lax.broadcasted_iota(jnp.int32, (N_GROUPS, D_INNER), 0)
    gcol = lax.broadcasted_iota(jnp.int32, (N_GROUPS, D_INNER), 1)
    gmask = (gcol >= grow * GROUP_COLS) & (gcol < (grow + 1) * GROUP_COLS)
    for t in range(st_ref.shape[0]):
        dtx = jnp.where(gmask, dtx_ref[t], 0.0)
        x_hi, x_mid, _ = _split3(dtx)
        b_hi, b_mid, _ = _split3(b_ref[t])
        lhs = jnp.concatenate([x_hi.astype(F32), x_mid.astype(F32), x_hi.astype(F32)], axis=0)
        rhs = jnp.concatenate([b_hi.astype(F32), b_hi.astype(F32), b_mid.astype(F32)], axis=0)
        upd = lax.dot_general(lhs, rhs, (((0,), (0,)), ((), ())), preferred_element_type=F32)
        dec = dec_ref[t]
        c_b = c_ref[t].astype(BF16)
        y_parts = []
        for g in range(N_GROUPS):
            new_g = []
            for r in range(HEADS_PER_GROUP):
                h = g * HEADS_PER_GROUP + r
                scale = jnp.broadcast_to(dec[:, h:h + 1], (HEAD_DIM, D_STATE))
                new = st_ref[t, h] * scale + upd[h * HEAD_DIM:(h + 1) * HEAD_DIM, :]
                nst_ref[t, h] = new
                new_g.append(new.astype(BF16))
            y_g = _dot_nt(c_b, jnp.concatenate(new_g, axis=0))
            gcols = slice(g * GROUP_COLS, (g + 1) * GROUP_COLS)
            y_parts.append(y_g[g:g + 1, :] + dskip_ref[:, gcols] * xs_ref[t, :, gcols])
        y = jnp.concatenate(y_parts, axis=1)
        ya_ref[t] = _rms(y * _silu(z_ref[t]), sn_ref[...])


def _sample_ssm(state, b3, c3, dec3, dtx3, xs3, z3, dskip_e, sn):
    nb = state.shape[0]
    tb = _token_tile(nb, 4)
    tok3 = lambda i: (i, 0, 0)
    row3 = pl.BlockSpec((tb, 1, D_INNER), tok3)
    state_spec = pl.BlockSpec((tb, N_HEADS, HEAD_DIM, D_STATE), lambda i: (i, 0, 0, 0))
    return pl.pallas_call(
        _sample_ssm_kernel,
        grid=(nb // tb,),
        in_specs=[state_spec,
                  pl.BlockSpec((tb, N_GROUPS, D_STATE), tok3),
                  pl.BlockSpec((tb, N_GROUPS, D_STATE), tok3),
                  pl.BlockSpec((tb, 1, DT_PAD), tok3),
                  row3, row3, row3,
                  _const_spec(dskip_e.shape), _const_spec(sn.shape)],
        out_specs=[row3, state_spec],
        out_shape=[jax.ShapeDtypeStruct((nb, 1, D_INNER), F32),
                   jax.ShapeDtypeStruct(state.shape, F32)],
        compiler_params=pltpu.CompilerParams(dimension_semantics=("arbitrary",),
                                             vmem_limit_bytes=VMEM_LIMIT),
        name="sample_ssm",
    )(state, b3, c3, dec3, dtx3, xs3, z3, dskip_e, sn)


def _out_kernel(x_ref, ya_ref, yb_ref, gpre_ref, win_hbm, wtail_ref, wa_hbm, wb_hbm, wo_hbm, gpost_ref,
                fpre_ref, w1_hbm, w2_hbm, fpost_ref, o_ref,
                wg_ref, wa_ref, wb_ref, wo_ref, w1_ref, w2_ref, stage, ssem):
    @pl.when(pl.program_id(0) == 0)
    def _():
        chunks = _skewed_chunks(win_hbm, IN_O_GATE, 2 * D_MODEL, wg_ref, 0, tail=wtail_ref)
        for src, dst in ((wa_hbm, wa_ref), (wb_hbm, wb_ref), (wo_hbm, wo_ref), (w1_hbm, w1_ref), (w2_hbm, w2_ref)):
            chunks += _plain_chunks(src, dst)
        _prep_weights(chunks, stage, ssem)

    x = x_ref[...]
    xn = _rms(x, gpre_ref[...]).astype(BF16)
    ya = _dot(ya_ref[...].astype(BF16), wa_ref[...])
    yb = _dot(yb_ref[...], wb_ref[...])
    merged = (jax.nn.sigmoid(_dot(xn, wg_ref[:, 0:D_MODEL])) * ya
              + jax.nn.sigmoid(_dot(xn, wg_ref[:, D_MODEL:])) * yb)
    mo = _dot(merged.astype(BF16), wo_ref[...])
    h = x + _rms(mo, gpost_ref[...])
    hn = _rms(h, fpre_ref[...]).astype(BF16)
    gate = _dot(hn, w1_ref[:, 0:D_FF])
    up = _dot(hn, w1_ref[:, D_FF:])
    f = _dot((_silu(gate) * up).astype(BF16), w2_ref[...])
    o_ref[...] = h + _rms(f, fpost_ref[...])


def _out(x, ya, yb, consts, tm):
    m = x.shape[0]
    tok = lambda i: (i, 0)
    big = [a.shape[0] >= D_MODEL and a.shape[1] >= D_MODEL for a in consts]
    resident = [(D_MODEL, 2 * D_MODEL)] + [a.shape for a, b in zip(consts, big) if b][1:]
    return pl.pallas_call(
        _out_kernel,
        grid=(m // tm,),
        in_specs=[pl.BlockSpec((tm, D_MODEL), tok), pl.BlockSpec((tm, D_INNER), tok),
                  pl.BlockSpec((tm, POOL_DIM), tok)]
                 + [_HBM_SPEC if b else _const_spec(a.shape) for a, b in zip(consts, big)],
        out_specs=pl.BlockSpec((tm, D_MODEL), tok),
        out_shape=jax.ShapeDtypeStruct((m, D_MODEL), F32),
        scratch_shapes=[pltpu.VMEM(shape, BF16) for shape in resident]
                       + [pltpu.VMEM((2, STAGE_ROWS, STAGE_COLS), F32), pltpu.SemaphoreType.DMA((2,))],
        compiler_params=pltpu.CompilerParams(dimension_semantics=("arbitrary",),
                                             vmem_limit_bytes=VMEM_LIMIT_OUT),
        name="out",
    )(x, ya, yb, *consts)


def _token_tile(m, cap):
    t = min(m, cap)
    assert m % t == 0
    return t


def kernel(x_prompt, x_sample, state_conv, state_ssm, state_pool, norm_mix_pre, norm_mix_post, norm_ffn_pre,
           norm_ffn_post, w_in, conv_w, conv_b, dt_bias, a_log, d_skip, ssm_norm, w_pool_group, pool_scale,
           w_branch_a, w_branch_b, w_out, w_ffn_in, w_ffn_out):
    bsz, seq, _ = x_prompt.shape
    nb, dec_seq, _ = x_sample.shape
    assert dec_seq == 1 and norm_mix_pre.shape[0] == 1
    l = 0
    wi = w_in[l]
    w_tail = jnp.pad(wi[:, IN_TAIL0:], ((0, 0), (0, LANES - (IN_DIM - IN_TAIL0))))
    pad_h = lambda v: jnp.pad(v.astype(F32), (0, DT_PAD - N_HEADS)).reshape(1, DT_PAD)
    row = lambda v: v.astype(F32).reshape(1, -1)
    dtb, alog = pad_h(dt_bias[l]), pad_h(a_log[l])
    dskip_e = jnp.repeat(d_skip[l].astype(F32), HEAD_DIM).reshape(1, D_INNER)
    cw, cb = conv_w[l].astype(F32), row(conv_b[l])
    wp, ps = w_pool_group[l].astype(BF16), row(pool_scale[l])
    g_pre, sn = row(norm_mix_pre[l]), row(ssm_norm[l])
    out_consts = (g_pre, wi, w_tail, w_branch_a[l], w_branch_b[l], w_out[l],
                  row(norm_mix_post[l]), row(norm_ffn_pre[l]), w_ffn_in[l], w_ffn_out[l],
                  row(norm_ffn_post[l]))

    xp = x_prompt.reshape(bsz * seq, D_MODEL)
    ya, yb, nconv_p, nssm_p, npool_p = _mix_prompt(xp, g_pre, wi, cw, cb, dtb, alog, dskip_e, wp, ps, sn,
                                                   bsz, seq)
    out_p = _out(xp, ya, yb, out_consts, _token_tile(bsz * seq, 256)).reshape(bsz, seq, D_MODEL)

    xs2 = x_sample.reshape(nb, D_MODEL)
    z, xbc, u, dt = _in_proj(xs2, g_pre, wi)
    sconv = jnp.swapaxes(state_conv[l], 0, 1)
    spool = jnp.swapaxes(state_pool[l], 0, 1)
    xc, dec, dtx, nconv_s, npool_s, yb_s = _sample_tok(xbc, dt, u, sconv, spool, cw, cb, dtb, alog, wp, ps)
    b3 = xc[:, D_INNER:D_INNER + N_GROUPS * D_STATE].reshape(nb, N_GROUPS, D_STATE)
    c3 = xc[:, D_INNER + N_GROUPS * D_STATE:].reshape(nb, N_GROUPS, D_STATE)
    xs3 = xc[:, :D_INNER].reshape(nb, 1, D_INNER)
    ya3, nssm_s = _sample_ssm(state_ssm[l], b3, c3, dec.reshape(nb, 1, DT_PAD), dtx.reshape(nb, 1, D_INNER),
                              xs3, z.reshape(nb, 1, D_INNER), dskip_e, sn)
    out_s = _out(xs2, ya3.reshape(nb, D_INNER), yb_s, out_consts, _token_tile(nb, 256)).reshape(nb, 1, D_MODEL)

    return (out_p, out_s,
            nconv_p[None], nssm_p[None], npool_p[None],
            jnp.swapaxes(nconv_s, 0, 1)[None], nssm_s[None],
            jnp.swapaxes(npool_s, 0, 1)[None])
```

```python
import functools

import jax
import jax.numpy as jnp
from jax import lax
from jax.experimental import pallas as pl
from jax.experimental.pallas import tpu as pltpu

D_MODEL = 1024
D_INNER = 2048
HEAD_DIM = 64
N_HEADS = 32
N_GROUPS = 8
HEADS_PER_GROUP = 4
D_STATE = 128
CONV_WIDTH = 4
CONV_DIM = 4096
CHUNK = 128
POOL_DIM = 1024
POOL_WINDOWS = (2, 4, 8, 16)
POOL_GROUP_DIM = 256
POOL_BUF = 15
D_FF = 2816
EPS = 1e-6
PAST_LEN = 16384
LOG2E = 1.4426950408889634

LANES = 128
SUBLANES = 8
TILES = CHUNK // SUBLANES
GROUP_COLS = HEADS_PER_GROUP * HEAD_DIM
DT_PAD = LANES
MIX_WIDTH = D_INNER + CONV_DIM + POOL_DIM + DT_PAD
CONV_HALO = (CONV_WIDTH - 1) * SUBLANES
POOL_HALO = CHUNK
VMEM_LIMIT = 56 * 1024 * 1024
VMEM_LIMIT_OUT = 62 * 1024 * 1024

F32 = jnp.float32
BF16 = jnp.bfloat16


def _rms(x, g):
    y = x * lax.rsqrt(jnp.mean(x * x, axis=-1, keepdims=True) + EPS)
    return y * g


def _silu(x):
    return x * jax.nn.sigmoid(x)


def _softplus(x):
    return jnp.maximum(x, 0.0) + jnp.log(1.0 + jnp.exp(-jnp.abs(x)))


def _split3(x):
    hi = x.astype(BF16)
    r = x - hi.astype(F32)
    mid = r.astype(BF16)
    lo = (r - mid.astype(F32)).astype(BF16)
    return hi, mid, lo


def _dot(a, b):
    return jnp.dot(a, b, preferred_element_type=F32)


def _dot_nt(a, b):
    return lax.dot_general(a, b, (((1,), (1,)), ((), ())), preferred_element_type=F32)


def _const_spec(shape):
    nd = len(shape)
    return pl.BlockSpec(shape, lambda *_: (0,) * nd)


_HBM_SPEC = pl.BlockSpec(memory_space=pl.ANY)


def _load_resident(pairs, sem):
    copies = [pltpu.make_async_copy(src, dst, sem.at[i]) for i, (src, dst) in enumerate(pairs)]
    for c in copies:
        c.start()
    for c in copies:
        c.wait()


STAGE_ROWS = 512
STAGE_COLS = 1024
PREP_ROWS = 256


def _prep_weights(chunks, stage, sem):
    def copy(i):
        src, r0, nr, c0, nc, _ = chunks[i]
        return pltpu.make_async_copy(src.at[pl.ds(r0, nr), pl.ds(c0, nc)],
                                     stage.at[i % 2, pl.ds(0, nr), pl.ds(0, nc)], sem.at[i % 2])

    copy(0).start()
    for i, (_, _, nr, _, nc, emit) in enumerate(chunks):
        if i + 1 < len(chunks):
            copy(i + 1).start()
        copy(i).wait()
        step = min(PREP_ROWS, nr)
        for r in range(0, nr, step):
            emit(slice(r, r + step), stage[i % 2, r:r + step, 0:nc])


def _plain_chunks(src, dst):
    rows, cols = src.shape

    def emit_at(r0, c0, nc):
        def emit(rs, v):
            dst[r0 + rs.start:r0 + rs.stop, c0:c0 + nc] = v.astype(BF16)
        return emit

    return [(src, r0, min(STAGE_ROWS, rows - r0), c0, min(STAGE_COLS, cols - c0),
             emit_at(r0, c0, min(STAGE_COLS, cols - c0)))
            for r0 in range(0, rows, STAGE_ROWS) for c0 in range(0, cols, STAGE_COLS)]


IN_O_DT = D_INNER + CONV_DIM
IN_O_POOL = IN_O_DT + N_HEADS
IN_O_GATE = IN_O_POOL + POOL_DIM


def _transposed_chunks(src_t, row0, n_rows, dst, dst_col0):
    def emit_at(j0):
        def emit(rs, v):
            dst[:, dst_col0 + j0 + rs.start:dst_col0 + j0 + rs.stop] = v.T.astype(BF16)
        return emit

    return [(src_t, row0 + j0, min(STAGE_ROWS, n_rows - j0), 0, D_MODEL, emit_at(j0))
            for j0 in range(0, n_rows, STAGE_ROWS)]


def _mix_weight_chunks(w_in_t, w_ref):
    def emit_dt(rs, v):
        vt = v.T
        lane = lax.broadcasted_iota(jnp.int32, vt.shape, 1)
        w_ref[:, MIX_WIDTH - DT_PAD:MIX_WIDTH] = jnp.where(lane < N_HEADS, vt, 0.0).astype(BF16)

    chunks = _transposed_chunks(w_in_t, 0, IN_O_DT, w_ref, 0)
    chunks += _transposed_chunks(w_in_t, IN_O_POOL, POOL_DIM, w_ref, IN_O_DT)
    chunks.append((w_in_t, IN_O_DT, DT_PAD, 0, D_MODEL, emit_dt))
    return chunks


def _time_of_row(r):
    return (r & (SUBLANES - 1)) * TILES + (r >> 3)


def _conv_silu(taps, cw_ref, cb_ref, cols):
    acc = cb_ref[:, cols]
    for k, t in enumerate(taps):
        acc = acc + t * cw_ref[k:k + 1, cols]
    return _silu(acc)


def _pool_mix(win_sum, u, cnt, wp_ref, ps_ref, gi):
    cols = slice(gi * POOL_GROUP_DIM, (gi + 1) * POOL_GROUP_DIM)
    d = win_sum / cnt - u
    mixed = _dot(d.astype(BF16), wp_ref[gi])
    return (mixed * ps_ref[:, cols]).astype(BF16)


def _project(x_ref, gpre_ref, w_ref, dtb_ref, to_perm, xnp_buf, xbc_buf, z_buf, u_buf, dt_buf):
    xn = _rms(x_ref[...], gpre_ref[...]).astype(BF16)
    xnp_buf[...] = _dot(to_perm, xn).astype(BF16)
    yield

    def proj(c0, width):
        tile = 256
        return jnp.concatenate([_dot(xnp_buf[...], w_ref[:, c:c + min(tile, c0 + width - c)])
                                for c in range(c0, c0 + width, tile)], axis=1)

    piece = 512
    for c0 in range(0, CONV_DIM, piece):
        xbc_buf[CONV_HALO:, c0:c0 + piece] = proj(D_INNER + c0, piece)
        yield
    o_u = D_INNER + CONV_DIM
    for c0 in range(0, POOL_DIM, piece):
        u_buf[POOL_HALO:, c0:c0 + piece] = proj(o_u + c0, piece)
        yield
    dt_buf[...] = _softplus(proj(MIX_WIDTH - DT_PAD, DT_PAD) + dtb_ref[...])
    for c0 in range(0, D_INNER, piece):
        z_buf[:, c0:c0 + piece] = proj(c0, piece)
        yield


def _run(gen):
    for _ in gen:
        pass


N_CONV_PIECES = CONV_DIM // 512


def _interleave(main, side, side_after):
    next(side)
    i = 0
    while next(main, StopIteration) is not StopIteration:
        for _ in range(side_after(i)):
            next(side, None)
        i += 1
    _run(side)


def _project_pieces_after(i):
    if i < N_CONV_PIECES:
        return 1
    return 1 if (i - N_CONV_PIECES) % 2 == 0 else 0


def _sequence(xbc_buf, z_buf, u_buf, dt_buf, xc_buf, y_buf, ctail, ptail, ht_ref,
              cw_ref, cb_ref, alog_ref, dskip_ref, wp_ref, ps_ref, sn_ref, masks, ya_ref, yb_ref, rows, tile_in_seq):
    q = CHUNK
    to_nat, causal, tri, lo_half, t_col = masks

    for n, j in enumerate(range(TILES - (CONV_WIDTH - 1), TILES)):
        src = CONV_HALO + j * SUBLANES
        xbc_buf[n * SUBLANES + 1:(n + 1) * SUBLANES, :] = xbc_buf[src:src + SUBLANES - 1, :]
        xbc_buf[n * SUBLANES:n * SUBLANES + 1, :] = ctail[n:n + 1, :]
    for c0 in range(0, CONV_DIM, 512):
        cols = slice(c0, c0 + 512)
        taps = []
        for k in range(CONV_WIDTH):
            start = CONV_HALO - (CONV_WIDTH - 1 - k) * SUBLANES
            taps.append(xbc_buf[start:start + q, cols])
        xc = _conv_silu(taps, cw_ref, cb_ref, cols)
        xc_buf[:, cols] = xc
        yield
    for n, j in enumerate(range(TILES - (CONV_WIDTH - 1), TILES)):
        src = CONV_HALO + j * SUBLANES + SUBLANES - 1
        ctail[n:n + 1, :] = xbc_buf[src:src + 1, :]

    dt = dt_buf[...]
    da = dt * (-jnp.exp(alog_ref[...]))
    acum = sum(_dot(tri, part) for part in _split3(da)) * LOG2E
    acum_last = acum[q - 1:q, :]
    e_last = jnp.exp2(acum_last)
    w_end_t = (jnp.exp2(acum_last - acum) * dt).T
    acum_t = acum.T
    dt_t = dt.T
    lo_half_row = lo_half[0:1, :]

    ssq = jnp.zeros((q, 1), F32)
    for g in range(N_GROUPS):
        b_g = xc_buf[:, D_INNER + g * D_STATE:D_INNER + (g + 1) * D_STATE]
        c_off = D_INNER + N_GROUPS * D_STATE + g * D_STATE
        c_b = xc_buf[:, c_off:c_off + D_STATE].astype(BF16)
        cb = _dot_nt(c_b, b_g.astype(BF16))
        b_t = b_g.T
        y_inter = _dot(c_b, ht_ref[g].astype(BF16))
        for pair in range(HEADS_PER_GROUP // 2):
            h0 = g * HEADS_PER_GROUP + 2 * pair
            h1 = h0 + 1
            pcols = slice(h0 * HEAD_DIM, (h0 + 2) * HEAD_DIM)
            scols = slice(pair * LANES, (pair + 1) * LANES)
            x_pair = xc_buf[:, pcols]
            ms, bws, e_cols = [], [], []
            for h in (h0, h1):
                a_col = jnp.broadcast_to(acum[:, h:h + 1], (q, q))
                seg = a_col - acum_t[h:h + 1, :]
                decay = jnp.exp2(jnp.where(causal, seg, -jnp.inf))
                ms.append((cb * decay * dt_t[h:h + 1, :]).astype(BF16))
                bws.append((b_t * w_end_t[h:h + 1, :]).astype(BF16))
                e_cols.append(jnp.exp2(a_col))
            x_top = jnp.where(lo_half, x_pair, 0.0).astype(BF16)
            x_bot = jnp.where(lo_half, 0.0, x_pair).astype(BF16)
            x_diag = jnp.concatenate([x_top, x_bot], axis=0)
            y_intra = _dot(jnp.concatenate(ms, axis=1), x_diag)
            e_pair = jnp.where(lo_half, e_cols[0], e_cols[1])
            y_pair = y_intra + y_inter[:, scols] * e_pair + dskip_ref[:, pcols] * x_pair
            gated = y_pair * _silu(z_buf[:, pcols])
            y_buf[:, pcols] = gated
            ssq = ssq + jnp.sum(gated * gated, axis=-1, keepdims=True)
            upd = _dot(jnp.concatenate(bws, axis=1), x_diag)
            e_last_pair = jnp.where(lo_half_row, e_last[:, h0:h0 + 1], e_last[:, h1:h1 + 1])
            ht_ref[g, :, scols] = ht_ref[g, :, scols] * e_last_pair + upd
            yield

    inv = lax.rsqrt(ssq * (1.0 / D_INNER) + EPS)
    for c0 in range(0, D_INNER, 1024):
        cols = slice(c0, c0 + 1024)
        ya = (y_buf[:, cols] * inv * sn_ref[:, cols]).astype(BF16)
        ya_ref[rows, cols] = _dot(to_nat, ya).astype(BF16)
        yield

    u_buf[1:q, :] = u_buf[POOL_HALO:POOL_HALO + q - 1, :]
    for j in range(1, TILES):
        u_buf[j * SUBLANES:j * SUBLANES + 1, :] = ptail[j - 1:j, :]
    pos = tile_in_seq * q + t_col
    yb_parts = []
    for gi, w in enumerate(POOL_WINDOWS):
        cols = slice(gi * POOL_GROUP_DIM, (gi + 1) * POOL_GROUP_DIM)
        u_g = u_buf[POOL_HALO:POOL_HALO + q, cols]
        s = u_g
        for k in range(1, w):
            s = s + u_buf[POOL_HALO - k * SUBLANES:POOL_HALO - k * SUBLANES + q, cols]
        cnt = jnp.minimum(pos + 1, w).astype(F32)
        yb_parts.append(_pool_mix(s, u_g, cnt, wp_ref, ps_ref, gi))
        yield
    yb_ref[rows, :] = _dot(to_nat, jnp.concatenate(yb_parts, axis=1)).astype(BF16)
    for j in range(1, TILES):
        src = POOL_HALO + j * SUBLANES + SUBLANES - 1
        ptail[j - 1:j, :] = u_buf[src:src + 1, :]


def _project_plain(x_ref, g_ref, w_ref, out_refs):
    xn = _rms(x_ref[...], g_ref[...]).astype(BF16)
    off = 0
    for o_ref in out_refs:
        width = o_ref.shape[1]
        for c in range(0, width, 1024):
            cw = min(1024, width - c)
            o_ref[:, c:c + cw] = _dot(xn, w_ref[:, off + c:off + c + cw])
        off += width


def _mix_kernel(x0_ref, x1_ref, x2_ref, xs_ref, gpre_ref, w_hbm, cw_ref, cb_ref, dtb_ref, alog_ref, dskip_ref,
                wp_hbm, ps_ref, sn_ref,
                ya_ref, yb_ref, nconv_ref, nssm_ref, npool_ref, zs_ref, xbcs_ref, us_ref, dts_ref,
                xbc0, z0, u0, dt0, xbc1, z1, u1, dt1, xc0, y0, xc1, y1, ctail, ptail, ht_ref,
                w_ref, wp_ref, wsem, xnp_buf, stage, ssem, *, tiles_per_seq):
    q = CHUNK
    k = pl.program_id(0)
    tile_in_seq = lax.rem(2 * k, tiles_per_seq)
    row = lax.broadcasted_iota(jnp.int32, (q, q), 0)
    col = lax.broadcasted_iota(jnp.int32, (q, q), 1)
    to_perm = jnp.where(col == _time_of_row(row), 1.0, 0.0).astype(BF16)
    to_nat = jnp.where(_time_of_row(col) == row, 1.0, 0.0).astype(BF16)
    causal = _time_of_row(row) >= _time_of_row(col)
    tri = jnp.where(causal, 1.0, 0.0).astype(BF16)
    t_col = _time_of_row(lax.broadcasted_iota(jnp.int32, (q, 1), 0))
    masks = (to_nat, causal, tri, col < HEAD_DIM, t_col)
    set0 = (xbc0, z0, u0, dt0)
    set1 = (xbc1, z1, u1, dt1)
    state = (ctail, ptail, ht_ref)
    consts = (cw_ref, cb_ref, alog_ref, dskip_ref, wp_ref, ps_ref, sn_ref, masks)

    @pl.when(k == 0)
    def _():
        _load_resident([(wp_hbm, wp_ref)], wsem)
        _prep_weights(_mix_weight_chunks(w_hbm, w_ref), stage, ssem)
        _run(_project(x0_ref, gpre_ref, w_ref, dtb_ref, to_perm, xnp_buf, *set0))

    @pl.when(tile_in_seq == 0)
    def _():
        ctail[...] = jnp.zeros(ctail.shape, F32)
        ptail[...] = jnp.zeros(ptail.shape, F32)
        ht_ref[...] = jnp.zeros(ht_ref.shape, F32)

    _interleave(_sequence(*set0, xc0, y0, *state, *consts, ya_ref, yb_ref, slice(0, q), tile_in_seq),
                _project(x1_ref, gpre_ref, w_ref, dtb_ref, to_perm, xnp_buf, *set1), _project_pieces_after)
    _interleave(_sequence(*set1, xc1, y1, *state, *consts, ya_ref, yb_ref, slice(q, 2 * q), tile_in_seq + 1),
                _project(x2_ref, gpre_ref, w_ref, dtb_ref, to_perm, xnp_buf, *set0), _project_pieces_after)

    @pl.when(tile_in_seq == tiles_per_seq - 2)
    def _():
        nconv_ref[0] = ctail[0:CONV_WIDTH - 1, :]
        npool_ref[0] = ptail[0:POOL_BUF, :]
        for g in range(N_GROUPS):
            nssm_ref[0, g * HEADS_PER_GROUP:(g + 1) * HEADS_PER_GROUP] = (
                ht_ref[g].T.reshape(HEADS_PER_GROUP, HEAD_DIM, D_STATE))

    @pl.when(k == pl.num_programs(0) - 1)
    def _():
        _project_plain(xs_ref, gpre_ref, w_ref, (zs_ref, xbcs_ref, us_ref, dts_ref))


def _mix_prompt(x, xs, gpre, w, cw, cb, dtb, alog, dskip_e, wp, ps, sn, bsz, seq):
    q = CHUNK
    nt = seq // q
    n_tiles = bsz * nt
    assert seq % (2 * q) == 0
    steps_per_seq = nt // 2
    m = bsz * seq
    consts = (gpre, w, cw, cb, dtb, alog, dskip_e, wp, ps, sn)
    nb = xs.shape[0]
    sample_widths = (D_INNER, CONV_DIM, POOL_DIM, DT_PAD)
    pair = lambda k: (k, 0)
    per_b = lambda k: (k // steps_per_seq, 0, 0)
    proj_set = [pltpu.VMEM((CONV_HALO + q, CONV_DIM), F32), pltpu.VMEM((q, D_INNER), F32),
                pltpu.VMEM((POOL_HALO + q, POOL_DIM), F32), pltpu.VMEM((q, DT_PAD), F32)]
    seq_set = [pltpu.VMEM((q, CONV_DIM), F32), pltpu.VMEM((q, D_INNER), F32)]
    return pl.pallas_call(
        functools.partial(_mix_kernel, tiles_per_seq=nt),
        grid=(n_tiles // 2,),
        in_specs=[pl.BlockSpec((q, D_MODEL), lambda k: (0, 0)),
                  pl.BlockSpec((q, D_MODEL), lambda k: (2 * k + 1, 0)),
                  pl.BlockSpec((q, D_MODEL), lambda k: (jnp.minimum(2 * k + 2, n_tiles - 1), 0)),
                  _const_spec(xs.shape)]
                 + [_HBM_SPEC if a is w or a is wp else _const_spec(a.shape) for a in consts],
        out_specs=[pl.BlockSpec((2 * q, D_INNER), pair),
                   pl.BlockSpec((2 * q, POOL_DIM), pair),
                   pl.BlockSpec((1, CONV_WIDTH - 1, CONV_DIM), per_b),
                   pl.BlockSpec((1, N_HEADS, HEAD_DIM, D_STATE), lambda k: (k // steps_per_seq, 0, 0, 0)),
                   pl.BlockSpec((1, POOL_BUF, POOL_DIM), per_b)]
                  + [_const_spec((nb, wd)) for wd in sample_widths],
        out_shape=[jax.ShapeDtypeStruct((m, D_INNER), BF16),
                   jax.ShapeDtypeStruct((m, POOL_DIM), BF16),
                   jax.ShapeDtypeStruct((bsz, CONV_WIDTH - 1, CONV_DIM), F32),
                   jax.ShapeDtypeStruct((bsz, N_HEADS, HEAD_DIM, D_STATE), F32),
                   jax.ShapeDtypeStruct((bsz, POOL_BUF, POOL_DIM), F32)]
                  + [jax.ShapeDtypeStruct((nb, wd), F32) for wd in sample_widths],
        scratch_shapes=proj_set + proj_set + seq_set + seq_set + [
            pltpu.VMEM((SUBLANES, CONV_DIM), F32),
            pltpu.VMEM((2 * SUBLANES, POOL_DIM), F32),
            pltpu.VMEM((N_GROUPS, D_STATE, GROUP_COLS), F32),
            pltpu.VMEM((D_MODEL, MIX_WIDTH), BF16), pltpu.VMEM(wp.shape, BF16), pltpu.SemaphoreType.DMA((1,)),
            pltpu.VMEM((q, D_MODEL), BF16),
            pltpu.VMEM((2, STAGE_ROWS, STAGE_COLS), F32), pltpu.SemaphoreType.DMA((2,))],
        compiler_params=pltpu.CompilerParams(dimension_semantics=("arbitrary",),
                                             vmem_limit_bytes=VMEM_LIMIT),
        name="mix_prompt",
    )(x, x, x, xs, *consts)


def _sample_tok_kernel(xbc_ref, dt_ref, u_ref, sconv_ref, spool_ref, cw_ref, cb_ref, dtb_ref, alog_ref,
                       wp_ref, ps_ref,
                       xc_ref, dec_ref, dtx_ref, nconv_ref, npool_ref, yb_ref):
    for c0 in range(0, CONV_DIM, 512):
        cols = slice(c0, c0 + 512)
        taps = [sconv_ref[k, :, cols] for k in range(CONV_WIDTH - 1)]
        taps.append(xbc_ref[:, cols])
        xc_ref[:, cols] = _conv_silu(taps, cw_ref, cb_ref, cols)
    for k in range(CONV_WIDTH - 2):
        nconv_ref[k] = sconv_ref[k + 1]
    nconv_ref[CONV_WIDTH - 2] = xbc_ref[...]

    dt = _softplus(dt_ref[...] + dtb_ref[...])
    dec_ref[...] = jnp.exp(dt * (-jnp.exp(alog_ref[...])))
    hrow = lax.broadcasted_iota(jnp.int32, (DT_PAD, D_INNER), 0)
    ccol = lax.broadcasted_iota(jnp.int32, (DT_PAD, D_INNER), 1)
    expand = jnp.where((ccol >= hrow * HEAD_DIM) & (ccol < (hrow + 1) * HEAD_DIM), 1.0, 0.0).astype(BF16)
    dt_e = sum(_dot(part, expand) for part in _split3(dt))
    dtx_ref[...] = dt_e * xc_ref[:, 0:D_INNER]

    cnt_base = PAST_LEN + 1
    for gi, w in enumerate(POOL_WINDOWS):
        cols = slice(gi * POOL_GROUP_DIM, (gi + 1) * POOL_GROUP_DIM)
        u_g = u_ref[:, cols]
        s = u_g
        for k in range(1, w):
            s = s + spool_ref[POOL_BUF - k, :, cols]
        yb_ref[:, cols] = _pool_mix(s, u_g, float(min(cnt_base, w)), wp_ref, ps_ref, gi)
    for k in range(POOL_BUF - 1):
        npool_ref[k] = spool_ref[k + 1]
    npool_ref[POOL_BUF - 1] = u_ref[...]


def _sample_tok(xbc, dt, u, sconv, spool, cw, cb, dtb, alog, wp, ps):
    nb = xbc.shape[0]
    full = lambda a: pl.BlockSpec(a.shape, lambda i: (0,) * a.ndim)
    args = (xbc, dt, u, sconv, spool, cw, cb, dtb, alog, wp, ps)
    out_shape = [jax.ShapeDtypeStruct((nb, CONV_DIM), F32),
                 jax.ShapeDtypeStruct((nb, DT_PAD), F32),
                 jax.ShapeDtypeStruct((nb, D_INNER), F32),
                 jax.ShapeDtypeStruct(sconv.shape, F32),
                 jax.ShapeDtypeStruct(spool.shape, F32),
                 jax.ShapeDtypeStruct((nb, POOL_DIM), BF16)]
    return pl.pallas_call(
        _sample_tok_kernel,
        grid=(1,),
        in_specs=[full(a) for a in args],
        out_specs=[pl.BlockSpec(s.shape, lambda i, nd=len(s.shape): (0,) * nd) for s in out_shape],
        out_shape=out_shape,
        compiler_params=pltpu.CompilerParams(dimension_semantics=("arbitrary",),
                                             vmem_limit_bytes=VMEM_LIMIT),
        name="sample_tok",
    )(*args)


def _sample_ssm_kernel(st_ref, b_ref, c_ref, dec_ref, dtx_ref, xs_ref, z_ref, dskip_ref, sn_ref, ya_ref, nst_ref):
    grow = lax.broadcasted_iota(jnp.int32, (N_GROUPS, D_INNER), 0)
    gcol = lax.broadcasted_iota(jnp.int32, (N_GROUPS, D_INNER), 1)
    gmask = (gcol >= grow * GROUP_COLS) & (gcol < (grow + 1) * GROUP_COLS)
    for t in range(st_ref.shape[0]):
        dtx = jnp.where(gmask, dtx_ref[t], 0.0)
        x_hi, x_mid, _ = _split3(dtx)
        b_hi, b_mid, _ = _split3(b_ref[t])
        lhs = jnp.concatenate([x_hi.astype(F32), x_mid.astype(F32), x_hi.astype(F32)], axis=0)
        rhs = jnp.concatenate([b_hi.astype(F32), b_hi.astype(F32), b_mid.astype(F32)], axis=0)
        upd = lax.dot_general(lhs, rhs, (((0,), (0,)), ((), ())), preferred_element_type=F32)
        dec = dec_ref[t]
        c_b = c_ref[t].astype(BF16)
        y_parts = []
        for g in range(N_GROUPS):
            new_g = []
            for r in range(HEADS_PER_GROUP):
                h = g * HEADS_PER_GROUP + r
                scale = jnp.broadcast_to(dec[:, h:h + 1], (HEAD_DIM, D_STATE))
                new = st_ref[t, h] * scale + upd[h * HEAD_DIM:(h + 1) * HEAD_DIM, :]
                nst_ref[t, h] = new
                new_g.append(new.astype(BF16))
            y_g = _dot_nt(c_b, jnp.concatenate(new_g, axis=0))
            gcols = slice(g * GROUP_COLS, (g + 1) * GROUP_COLS)
            y_parts.append(y_g[g:g + 1, :] + dskip_ref[:, gcols] * xs_ref[t, :, gcols])
        y = jnp.concatenate(y_parts, axis=1)
        ya_ref[t] = _rms(y * _silu(z_ref[t]), sn_ref[...])


def _sample_ssm(state, b3, c3, dec3, dtx3, xs3, z3, dskip_e, sn):
    nb = state.shape[0]
    tb = _token_tile(nb, 4)
    tok3 = lambda i: (i, 0, 0)
    row3 = pl.BlockSpec((tb, 1, D_INNER), tok3)
    state_spec = pl.BlockSpec((tb, N_HEADS, HEAD_DIM, D_STATE), lambda i: (i, 0, 0, 0))
    return pl.pallas_call(
        _sample_ssm_kernel,
        grid=(nb // tb,),
        in_specs=[state_spec,
                  pl.BlockSpec((tb, N_GROUPS, D_STATE), tok3),
                  pl.BlockSpec((tb, N_GROUPS, D_STATE), tok3),
                  pl.BlockSpec((tb, 1, DT_PAD), tok3),
                  row3, row3, row3,
                  _const_spec(dskip_e.shape), _const_spec(sn.shape)],
        out_specs=[row3, state_spec],
        out_shape=[jax.ShapeDtypeStruct((nb, 1, D_INNER), F32),
                   jax.ShapeDtypeStruct(state.shape, F32)],
        compiler_params=pltpu.CompilerParams(dimension_semantics=("arbitrary",),
                                             vmem_limit_bytes=VMEM_LIMIT),
        name="sample_ssm",
    )(state, b3, c3, dec3, dtx3, xs3, z3, dskip_e, sn)


def _out_math(x, ya_in, yb_in, gpre_ref, wg_ref, wa_ref, wb_ref, wo_ref, gpost_ref, fpre_ref, w1_ref, w2_ref,
              fpost_ref):
    xn = _rms(x, gpre_ref[...]).astype(BF16)
    ya = _dot(ya_in.astype(BF16), wa_ref[...])
    yb = _dot(yb_in, wb_ref[...])
    merged = (jax.nn.sigmoid(_dot(xn, wg_ref[:, 0:D_MODEL])) * ya
              + jax.nn.sigmoid(_dot(xn, wg_ref[:, D_MODEL:])) * yb)
    mo = _dot(merged.astype(BF16), wo_ref[...])
    h = x + _rms(mo, gpost_ref[...])
    hn = _rms(h, fpre_ref[...]).astype(BF16)
    f = None
    half = D_FF // 2
    for c0 in (0, half):
        gate = _dot(hn, w1_ref[:, c0:c0 + half])
        up = _dot(hn, w1_ref[:, D_FF + c0:D_FF + c0 + half])
        part = _dot((_silu(gate) * up).astype(BF16), w2_ref[c0:c0 + half, :])
        f = part if f is None else f + part
    return h + _rms(f, fpost_ref[...])


def _out_kernel(x_ref, ya_ref, yb_ref, xs_ref, yas_ref, ybs_ref, gpre_ref, wint_hbm, wa_hbm, wb_hbm, wo_hbm, gpost_ref,
                fpre_ref, w1_hbm, w2_hbm, fpost_ref, o_ref, os_ref,
                wg_ref, wa_ref, wb_ref, wo_ref, w1_ref, w2_ref, stage, ssem):
    i = pl.program_id(0)
    n_prompt = pl.num_programs(0) - 1
    params = (gpre_ref, wg_ref, wa_ref, wb_ref, wo_ref, gpost_ref, fpre_ref, w1_ref, w2_ref, fpost_ref)

    @pl.when(i == 0)
    def _():
        chunks = _transposed_chunks(wint_hbm, IN_O_GATE, 2 * D_MODEL, wg_ref, 0)
        for src, dst in ((wa_hbm, wa_ref), (wb_hbm, wb_ref), (wo_hbm, wo_ref), (w1_hbm, w1_ref), (w2_hbm, w2_ref)):
            chunks += _plain_chunks(src, dst)
        _prep_weights(chunks, stage, ssem)

    @pl.when(i < n_prompt)
    def _():
        o_ref[...] = _out_math(x_ref[...], ya_ref[...], yb_ref[...], *params)

    @pl.when(i == n_prompt)
    def _():
        os_ref[...] = _out_math(xs_ref[...], yas_ref[...], ybs_ref[...], *params)


def _out(x, ya, yb, xs, yas, ybs, consts, tm):
    m = x.shape[0]
    n = m // tm
    nb = xs.shape[0]
    tok = lambda i: (jnp.minimum(i, n - 1), 0)
    big = [a.shape[0] >= D_MODEL and a.shape[1] >= D_MODEL for a in consts]
    resident = [(D_MODEL, 2 * D_MODEL)] + [a.shape for a, b in zip(consts, big) if b][1:]
    once = lambda a: pl.BlockSpec(a.shape, lambda i: (0, 0), pipeline_mode=pl.Buffered(1))
    return pl.pallas_call(
        _out_kernel,
        grid=(n + 1,),
        in_specs=[pl.BlockSpec((tm, D_MODEL), tok), pl.BlockSpec((tm, D_INNER), tok),
                  pl.BlockSpec((tm, POOL_DIM), tok), once(xs), once(yas), once(ybs)]
                 + [_HBM_SPEC if b else _const_spec(a.shape) for a, b in zip(consts, big)],
        out_specs=[pl.BlockSpec((tm, D_MODEL), tok), pl.BlockSpec((nb, D_MODEL), lambda i: (0, 0))],
        out_shape=[jax.ShapeDtypeStruct((m, D_MODEL), F32), jax.ShapeDtypeStruct((nb, D_MODEL), F32)],
        scratch_shapes=[pltpu.VMEM(shape, BF16) for shape in resident]
                       + [pltpu.VMEM((2, STAGE_ROWS, STAGE_COLS), F32), pltpu.SemaphoreType.DMA((2,))],
        compiler_params=pltpu.CompilerParams(dimension_semantics=("arbitrary",),
                                             vmem_limit_bytes=VMEM_LIMIT_OUT),
        name="out",
    )(x, ya, yb, xs, yas, ybs, *consts)


def _token_tile(m, cap):
    t = min(m, cap)
    assert m % t == 0
    return t


def kernel(x_prompt, x_sample, state_conv, state_ssm, state_pool, norm_mix_pre, norm_mix_post, norm_ffn_pre,
           norm_ffn_post, w_in, conv_w, conv_b, dt_bias, a_log, d_skip, ssm_norm, w_pool_group, pool_scale,
           w_branch_a, w_branch_b, w_out, w_ffn_in, w_ffn_out):
    bsz, seq, _ = x_prompt.shape
    nb, dec_seq, _ = x_sample.shape
    assert dec_seq == 1 and norm_mix_pre.shape[0] == 1
    l = 0
    wit = jnp.swapaxes(w_in[l], 0, 1)
    pad_h = lambda v: jnp.pad(v.astype(F32), (0, DT_PAD - N_HEADS)).reshape(1, DT_PAD)
    row = lambda v: v.astype(F32).reshape(1, -1)
    dtb, alog = pad_h(dt_bias[l]), pad_h(a_log[l])
    dskip_e = jnp.repeat(d_skip[l].astype(F32), HEAD_DIM).reshape(1, D_INNER)
    cw, cb = conv_w[l].astype(F32), row(conv_b[l])
    wp, ps = w_pool_group[l].astype(BF16), row(pool_scale[l])
    g_pre, sn = row(norm_mix_pre[l]), row(ssm_norm[l])
    out_consts = (g_pre, wit, w_branch_a[l], w_branch_b[l], w_out[l],
                  row(norm_mix_post[l]), row(norm_ffn_pre[l]), w_ffn_in[l], w_ffn_out[l],
                  row(norm_ffn_post[l]))

    xp = x_prompt.reshape(bsz * seq, D_MODEL)
    xs2 = x_sample.reshape(nb, D_MODEL)
    ya, yb, nconv_p, nssm_p, npool_p, z, xbc, u, dt = _mix_prompt(xp, xs2, g_pre, wit, cw, cb, dtb, alog, dskip_e, wp,
                                                                  ps, sn, bsz, seq)
    sconv = jnp.swapaxes(state_conv[l], 0, 1)
    spool = jnp.swapaxes(state_pool[l], 0, 1)
    xc, dec, dtx, nconv_s, npool_s, yb_s = _sample_tok(xbc, dt, u, sconv, spool, cw, cb, dtb, alog, wp, ps)
    b3 = xc[:, D_INNER:D_INNER + N_GROUPS * D_STATE].reshape(nb, N_GROUPS, D_STATE)
    c3 = xc[:, D_INNER + N_GROUPS * D_STATE:].reshape(nb, N_GROUPS, D_STATE)
    xs3 = xc[:, :D_INNER].reshape(nb, 1, D_INNER)
    ya3, nssm_s = _sample_ssm(state_ssm[l], b3, c3, dec.reshape(nb, 1, DT_PAD), dtx.reshape(nb, 1, D_INNER),
                              xs3, z.reshape(nb, 1, D_INNER), dskip_e, sn)
    out_p, out_s = _out(xp, ya, yb, xs2, ya3.reshape(nb, D_INNER), yb_s, out_consts, _token_tile(bsz * seq, 256))
    out_p = out_p.reshape(bsz, seq, D_MODEL)
    out_s = out_s.reshape(nb, 1, D_MODEL)

    return (out_p, out_s,
            nconv_p[None], nssm_p[None], npool_p[None],
            jnp.swapaxes(nconv_s, 0, 1)[None], nssm_s[None],
            jnp.swapaxes(npool_s, 0, 1)[None])
```

```python
import functools

import jax
import jax.numpy as jnp
from jax import lax
from jax.experimental import pallas as pl
from jax.experimental.pallas import tpu as pltpu

D_MODEL = 1024
D_INNER = 2048
HEAD_DIM = 64
N_HEADS = 32
N_GROUPS = 8
HEADS_PER_GROUP = 4
D_STATE = 128
CONV_WIDTH = 4
CONV_DIM = 4096
CHUNK = 128
POOL_DIM = 1024
POOL_WINDOWS = (2, 4, 8, 16)
POOL_GROUP_DIM = 256
POOL_BUF = 15
D_FF = 2816
EPS = 1e-6
PAST_LEN = 16384
LOG2E = 1.4426950408889634

LANES = 128
SUBLANES = 8
TILES = CHUNK // SUBLANES
GROUP_COLS = HEADS_PER_GROUP * HEAD_DIM
DT_PAD = LANES
MIX_WIDTH = D_INNER + CONV_DIM + POOL_DIM + DT_PAD
CONV_HALO = (CONV_WIDTH - 1) * SUBLANES
POOL_HALO = CHUNK
VMEM_LIMIT = 56 * 1024 * 1024
VMEM_LIMIT_OUT = 62 * 1024 * 1024

F32 = jnp.float32
BF16 = jnp.bfloat16


def _rms(x, g):
    y = x * lax.rsqrt(jnp.mean(x * x, axis=-1, keepdims=True) + EPS)
    return y * g


def _silu(x):
    return x * jax.nn.sigmoid(x)


def _softplus(x):
    return jnp.maximum(x, 0.0) + jnp.log(1.0 + jnp.exp(-jnp.abs(x)))


def _split3(x):
    hi = x.astype(BF16)
    r = x - hi.astype(F32)
    mid = r.astype(BF16)
    lo = (r - mid.astype(F32)).astype(BF16)
    return hi, mid, lo


def _dot(a, b):
    return jnp.dot(a, b, preferred_element_type=F32)


def _dot_nt(a, b):
    return lax.dot_general(a, b, (((1,), (1,)), ((), ())), preferred_element_type=F32)


def _const_spec(shape):
    nd = len(shape)
    return pl.BlockSpec(shape, lambda *_: (0,) * nd)


_HBM_SPEC = pl.BlockSpec(memory_space=pl.ANY)


def _load_resident(pairs, sem):
    copies = [pltpu.make_async_copy(src, dst, sem.at[i]) for i, (src, dst) in enumerate(pairs)]
    for c in copies:
        c.start()
    for c in copies:
        c.wait()


STAGE_ROWS = 512
STAGE_COLS = 1024
PREP_ROWS = 256


def _prep_weights(chunks, stage, sem):
    def copy(i):
        src, r0, nr, c0, nc, _ = chunks[i]
        return pltpu.make_async_copy(src.at[pl.ds(r0, nr), pl.ds(c0, nc)],
                                     stage.at[i % 2, pl.ds(0, nr), pl.ds(0, nc)], sem.at[i % 2])

    copy(0).start()
    for i, (_, _, nr, _, nc, emit) in enumerate(chunks):
        if i + 1 < len(chunks):
            copy(i + 1).start()
        copy(i).wait()
        step = min(PREP_ROWS, nr)
        for r in range(0, nr, step):
            emit(slice(r, r + step), stage[i % 2, r:r + step, 0:nc])


def _plain_chunks(src, dst):
    rows, cols = src.shape

    def emit_at(r0, c0, nc):
        def emit(rs, v):
            dst[r0 + rs.start:r0 + rs.stop, c0:c0 + nc] = v.astype(BF16)
        return emit

    return [(src, r0, min(STAGE_ROWS, rows - r0), c0, min(STAGE_COLS, cols - c0),
             emit_at(r0, c0, min(STAGE_COLS, cols - c0)))
            for r0 in range(0, rows, STAGE_ROWS) for c0 in range(0, cols, STAGE_COLS)]


IN_O_DT = D_INNER + CONV_DIM
IN_O_POOL = IN_O_DT + N_HEADS
IN_O_GATE = IN_O_POOL + POOL_DIM


def _transposed_chunks(src_t, row0, n_rows, dst, dst_col0):
    def emit_at(j0):
        def emit(rs, v):
            dst[:, dst_col0 + j0 + rs.start:dst_col0 + j0 + rs.stop] = v.T.astype(BF16)
        return emit

    return [(src_t, row0 + j0, min(STAGE_ROWS, n_rows - j0), 0, D_MODEL, emit_at(j0))
            for j0 in range(0, n_rows, STAGE_ROWS)]


def _mix_weight_chunks(w_in_t, w_ref):
    def emit_dt(rs, v):
        vt = v.T
        lane = lax.broadcasted_iota(jnp.int32, vt.shape, 1)
        w_ref[:, MIX_WIDTH - DT_PAD:MIX_WIDTH] = jnp.where(lane < N_HEADS, vt, 0.0).astype(BF16)

    chunks = _transposed_chunks(w_in_t, 0, IN_O_DT, w_ref, 0)
    chunks += _transposed_chunks(w_in_t, IN_O_POOL, POOL_DIM, w_ref, IN_O_DT)
    chunks.append((w_in_t, IN_O_DT, DT_PAD, 0, D_MODEL, emit_dt))
    return chunks


def _time_of_row(r):
    return (r & (SUBLANES - 1)) * TILES + (r >> 3)


def _conv_silu(taps, cw_ref, cb_ref, cols):
    acc = cb_ref[:, cols]
    for k, t in enumerate(taps):
        acc = acc + t * cw_ref[k:k + 1, cols]
    return _silu(acc)


def _pool_mix(win_sum, u, cnt, wp_ref, ps_ref, gi):
    cols = slice(gi * POOL_GROUP_DIM, (gi + 1) * POOL_GROUP_DIM)
    d = win_sum / cnt - u
    mixed = _dot(d.astype(BF16), wp_ref[gi])
    return (mixed * ps_ref[:, cols]).astype(BF16)


def _project(x_ref, gpre_ref, w_ref, dtb_ref, to_perm, xnp_buf, xbc_buf, z_buf, u_buf, dt_buf):
    xn = _rms(x_ref[...], gpre_ref[...]).astype(BF16)
    xnp_buf[...] = _dot(to_perm, xn).astype(BF16)
    yield

    def proj(c0, width):
        tile = 256
        return jnp.concatenate([_dot(xnp_buf[...], w_ref[:, c:c + min(tile, c0 + width - c)])
                                for c in range(c0, c0 + width, tile)], axis=1)

    piece = 512
    for c0 in range(0, CONV_DIM, piece):
        xbc_buf[CONV_HALO:, c0:c0 + piece] = proj(D_INNER + c0, piece)
        yield
    o_u = D_INNER + CONV_DIM
    for c0 in range(0, POOL_DIM, piece):
        u_buf[POOL_HALO:, c0:c0 + piece] = proj(o_u + c0, piece)
        yield
    dt_buf[...] = _softplus(proj(MIX_WIDTH - DT_PAD, DT_PAD) + dtb_ref[...])
    for c0 in range(0, D_INNER, piece):
        z_buf[:, c0:c0 + piece] = proj(c0, piece)
        yield


def _run(gen):
    for _ in gen:
        pass


N_CONV_PIECES = CONV_DIM // 512


def _interleave(main, side, side_after):
    next(side)
    i = 0
    while next(main, StopIteration) is not StopIteration:
        for _ in range(side_after(i)):
            next(side, None)
        i += 1
    _run(side)


def _project_pieces_after(i):
    if i < N_CONV_PIECES:
        return 1
    return 1 if (i - N_CONV_PIECES) % 2 == 0 else 0


def _sequence(xbc_buf, z_buf, u_buf, dt_buf, xc_buf, y_buf, ctail, ptail, ht_ref,
              cw_ref, cb_ref, alog_ref, dskip_ref, wp_ref, ps_ref, sn_ref, masks, ya_ref, yb_ref, rows, tile_in_seq):
    q = CHUNK
    to_nat, causal, tri, lo_half, t_col = masks

    for n, j in enumerate(range(TILES - (CONV_WIDTH - 1), TILES)):
        src = CONV_HALO + j * SUBLANES
        xbc_buf[n * SUBLANES + 1:(n + 1) * SUBLANES, :] = xbc_buf[src:src + SUBLANES - 1, :]
        xbc_buf[n * SUBLANES:n * SUBLANES + 1, :] = ctail[n:n + 1, :]
    for c0 in range(0, CONV_DIM, 512):
        cols = slice(c0, c0 + 512)
        taps = []
        for k in range(CONV_WIDTH):
            start = CONV_HALO - (CONV_WIDTH - 1 - k) * SUBLANES
            taps.append(xbc_buf[start:start + q, cols])
        xc = _conv_silu(taps, cw_ref, cb_ref, cols)
        xc_buf[:, cols] = xc
        yield
    for n, j in enumerate(range(TILES - (CONV_WIDTH - 1), TILES)):
        src = CONV_HALO + j * SUBLANES + SUBLANES - 1
        ctail[n:n + 1, :] = xbc_buf[src:src + 1, :]

    dt = dt_buf[...]
    da = dt * (-jnp.exp(alog_ref[...]))
    acum = sum(_dot(tri, part) for part in _split3(da)) * LOG2E
    acum_last = acum[q - 1:q, :]
    e_last = jnp.exp2(acum_last)
    w_end_t = (jnp.exp2(acum_last - acum) * dt).T
    acum_t = acum.T
    dt_t = dt.T
    lo_half_row = lo_half[0:1, :]

    ssq = jnp.zeros((q, 1), F32)
    for g in range(N_GROUPS):
        b_g = xc_buf[:, D_INNER + g * D_STATE:D_INNER + (g + 1) * D_STATE]
        c_off = D_INNER + N_GROUPS * D_STATE + g * D_STATE
        c_b = xc_buf[:, c_off:c_off + D_STATE].astype(BF16)
        cb = _dot_nt(c_b, b_g.astype(BF16))
        b_t = b_g.T
        y_inter = _dot(c_b, ht_ref[g].astype(BF16))
        for pair in range(HEADS_PER_GROUP // 2):
            h0 = g * HEADS_PER_GROUP + 2 * pair
            h1 = h0 + 1
            pcols = slice(h0 * HEAD_DIM, (h0 + 2) * HEAD_DIM)
            scols = slice(pair * LANES, (pair + 1) * LANES)
            x_pair = xc_buf[:, pcols]
            ms, bws, e_cols = [], [], []
            for h in (h0, h1):
                a_col = jnp.broadcast_to(acum[:, h:h + 1], (q, q))
                seg = a_col - acum_t[h:h + 1, :]
                decay = jnp.exp2(jnp.where(causal, seg, -jnp.inf))
                ms.append((cb * decay * dt_t[h:h + 1, :]).astype(BF16))
                bws.append((b_t * w_end_t[h:h + 1, :]).astype(BF16))
                e_cols.append(jnp.exp2(a_col))
            x_top = jnp.where(lo_half, x_pair, 0.0).astype(BF16)
            x_bot = jnp.where(lo_half, 0.0, x_pair).astype(BF16)
            x_diag = jnp.concatenate([x_top, x_bot], axis=0)
            y_intra = _dot(jnp.concatenate(ms, axis=1), x_diag)
            e_pair = jnp.where(lo_half, e_cols[0], e_cols[1])
            y_pair = y_intra + y_inter[:, scols] * e_pair + dskip_ref[:, pcols] * x_pair
            gated = y_pair * _silu(z_buf[:, pcols])
            y_buf[:, pcols] = gated
            ssq = ssq + jnp.sum(gated * gated, axis=-1, keepdims=True)
            upd = _dot(jnp.concatenate(bws, axis=1), x_diag)
            e_last_pair = jnp.where(lo_half_row, e_last[:, h0:h0 + 1], e_last[:, h1:h1 + 1])
            ht_ref[g, :, scols] = ht_ref[g, :, scols] * e_last_pair + upd
            yield

    inv = lax.rsqrt(ssq * (1.0 / D_INNER) + EPS)
    for c0 in range(0, D_INNER, 1024):
        cols = slice(c0, c0 + 1024)
        ya = (y_buf[:, cols] * inv * sn_ref[:, cols]).astype(BF16)
        ya_ref[rows, cols] = _dot(to_nat, ya).astype(BF16)
        yield

    u_buf[1:q, :] = u_buf[POOL_HALO:POOL_HALO + q - 1, :]
    for j in range(1, TILES):
        u_buf[j * SUBLANES:j * SUBLANES + 1, :] = ptail[j - 1:j, :]
    pos = tile_in_seq * q + t_col
    yb_parts = []
    for gi, w in enumerate(POOL_WINDOWS):
        cols = slice(gi * POOL_GROUP_DIM, (gi + 1) * POOL_GROUP_DIM)
        u_g = u_buf[POOL_HALO:POOL_HALO + q, cols]
        s = u_g
        for k in range(1, w):
            s = s + u_buf[POOL_HALO - k * SUBLANES:POOL_HALO - k * SUBLANES + q, cols]
        cnt = jnp.minimum(pos + 1, w).astype(F32)
        yb_parts.append(_pool_mix(s, u_g, cnt, wp_ref, ps_ref, gi))
        yield
    yb_ref[rows, :] = _dot(to_nat, jnp.concatenate(yb_parts, axis=1)).astype(BF16)
    for j in range(1, TILES):
        src = POOL_HALO + j * SUBLANES + SUBLANES - 1
        ptail[j - 1:j, :] = u_buf[src:src + 1, :]


def _project_plain(x_ref, g_ref, w_ref, out_refs):
    xn = _rms(x_ref[...], g_ref[...]).astype(BF16)
    off = 0
    for o_ref in out_refs:
        width = o_ref.shape[1]
        for c in range(0, width, 1024):
            cw = min(1024, width - c)
            o_ref[:, c:c + cw] = _dot(xn, w_ref[:, off + c:off + c + cw])
        off += width


def _mix_kernel(x0_ref, x1_ref, x2_ref, xs_ref, gpre_ref, w_hbm, cw_ref, cb_ref, dtb_ref, alog_ref, dskip_ref,
                wp_hbm, ps_ref, sn_ref,
                ya_ref, yb_ref, nconv_ref, nssm_ref, npool_ref, zs_ref, xbcs_ref, us_ref, dts_ref,
                xbc0, z0, u0, dt0, xbc1, z1, u1, dt1, xc0, y0, xc1, y1, ctail, ptail, ht_ref,
                w_ref, wp_ref, wsem, xnp_buf, stage, ssem, *, tiles_per_seq):
    q = CHUNK
    k = pl.program_id(0)
    tile_in_seq = lax.rem(2 * k, tiles_per_seq)
    row = lax.broadcasted_iota(jnp.int32, (q, q), 0)
    col = lax.broadcasted_iota(jnp.int32, (q, q), 1)
    to_perm = jnp.where(col == _time_of_row(row), 1.0, 0.0).astype(BF16)
    to_nat = jnp.where(_time_of_row(col) == row, 1.0, 0.0).astype(BF16)
    causal = _time_of_row(row) >= _time_of_row(col)
    tri = jnp.where(causal, 1.0, 0.0).astype(BF16)
    t_col = _time_of_row(lax.broadcasted_iota(jnp.int32, (q, 1), 0))
    masks = (to_nat, causal, tri, col < HEAD_DIM, t_col)
    set0 = (xbc0, z0, u0, dt0)
    set1 = (xbc1, z1, u1, dt1)
    state = (ctail, ptail, ht_ref)
    consts = (cw_ref, cb_ref, alog_ref, dskip_ref, wp_ref, ps_ref, sn_ref, masks)

    @pl.when(k == 0)
    def _():
        _load_resident([(wp_hbm, wp_ref)], wsem)
        _prep_weights(_mix_weight_chunks(w_hbm, w_ref), stage, ssem)
        _run(_project(x0_ref, gpre_ref, w_ref, dtb_ref, to_perm, xnp_buf, *set0))

    @pl.when(tile_in_seq == 0)
    def _():
        ctail[...] = jnp.zeros(ctail.shape, F32)
        ptail[...] = jnp.zeros(ptail.shape, F32)
        ht_ref[...] = jnp.zeros(ht_ref.shape, F32)

    _interleave(_sequence(*set0, xc0, y0, *state, *consts, ya_ref, yb_ref, slice(0, q), tile_in_seq),
                _project(x1_ref, gpre_ref, w_ref, dtb_ref, to_perm, xnp_buf, *set1), _project_pieces_after)
    _interleave(_sequence(*set1, xc1, y1, *state, *consts, ya_ref, yb_ref, slice(q, 2 * q), tile_in_seq + 1),
                _project(x2_ref, gpre_ref, w_ref, dtb_ref, to_perm, xnp_buf, *set0), _project_pieces_after)

    @pl.when(tile_in_seq == tiles_per_seq - 2)
    def _():
        nconv_ref[0] = ctail[0:CONV_WIDTH - 1, :]
        npool_ref[0] = ptail[0:POOL_BUF, :]
        for g in range(N_GROUPS):
            nssm_ref[0, g * HEADS_PER_GROUP:(g + 1) * HEADS_PER_GROUP] = (
                ht_ref[g].T.reshape(HEADS_PER_GROUP, HEAD_DIM, D_STATE))

    @pl.when(k == pl.num_programs(0) - 1)
    def _():
        _project_plain(xs_ref, gpre_ref, w_ref, (zs_ref, xbcs_ref, us_ref, dts_ref))


def _mix_prompt(x, xs, gpre, w, cw, cb, dtb, alog, dskip_e, wp, ps, sn, bsz, seq):
    q = CHUNK
    nt = seq // q
    n_tiles = bsz * nt
    assert seq % (2 * q) == 0
    steps_per_seq = nt // 2
    m = bsz * seq
    consts = (gpre, w, cw, cb, dtb, alog, dskip_e, wp, ps, sn)
    nb = xs.shape[0]
    sample_widths = (D_INNER, CONV_DIM, POOL_DIM, DT_PAD)
    pair = lambda k: (k, 0)
    per_b = lambda k: (k // steps_per_seq, 0, 0)
    proj_set = [pltpu.VMEM((CONV_HALO + q, CONV_DIM), F32), pltpu.VMEM((q, D_INNER), F32),
                pltpu.VMEM((POOL_HALO + q, POOL_DIM), F32), pltpu.VMEM((q, DT_PAD), F32)]
    seq_set = [pltpu.VMEM((q, CONV_DIM), F32), pltpu.VMEM((q, D_INNER), F32)]
    return pl.pallas_call(
        functools.partial(_mix_kernel, tiles_per_seq=nt),
        grid=(n_tiles // 2,),
        in_specs=[pl.BlockSpec((q, D_MODEL), lambda k: (0, 0)),
                  pl.BlockSpec((q, D_MODEL), lambda k: (2 * k + 1, 0)),
                  pl.BlockSpec((q, D_MODEL), lambda k: (jnp.minimum(2 * k + 2, n_tiles - 1), 0)),
                  _const_spec(xs.shape)]
                 + [_HBM_SPEC if a is w or a is wp else _const_spec(a.shape) for a in consts],
        out_specs=[pl.BlockSpec((2 * q, D_INNER), pair),
                   pl.BlockSpec((2 * q, POOL_DIM), pair),
                   pl.BlockSpec((1, CONV_WIDTH - 1, CONV_DIM), per_b),
                   pl.BlockSpec((1, N_HEADS, HEAD_DIM, D_STATE), lambda k: (k // steps_per_seq, 0, 0, 0)),
                   pl.BlockSpec((1, POOL_BUF, POOL_DIM), per_b)]
                  + [_const_spec((nb, wd)) for wd in sample_widths],
        out_shape=[jax.ShapeDtypeStruct((m, D_INNER), BF16),
                   jax.ShapeDtypeStruct((m, POOL_DIM), BF16),
                   jax.ShapeDtypeStruct((bsz, CONV_WIDTH - 1, CONV_DIM), F32),
                   jax.ShapeDtypeStruct((bsz, N_HEADS, HEAD_DIM, D_STATE), F32),
                   jax.ShapeDtypeStruct((bsz, POOL_BUF, POOL_DIM), F32)]
                  + [jax.ShapeDtypeStruct((nb, wd), F32) for wd in sample_widths],
        scratch_shapes=proj_set + proj_set + seq_set + seq_set + [
            pltpu.VMEM((SUBLANES, CONV_DIM), F32),
            pltpu.VMEM((2 * SUBLANES, POOL_DIM), F32),
            pltpu.VMEM((N_GROUPS, D_STATE, GROUP_COLS), F32),
            pltpu.VMEM((D_MODEL, MIX_WIDTH), BF16), pltpu.VMEM(wp.shape, BF16), pltpu.SemaphoreType.DMA((1,)),
            pltpu.VMEM((q, D_MODEL), BF16),
            pltpu.VMEM((2, STAGE_ROWS, STAGE_COLS), F32), pltpu.SemaphoreType.DMA((2,))],
        compiler_params=pltpu.CompilerParams(dimension_semantics=("arbitrary",),
                                             vmem_limit_bytes=VMEM_LIMIT),
        name="mix_prompt",
    )(x, x, x, xs, *consts)


def _sample_tok_kernel(xbc_ref, dt_ref, u_ref, sconv_ref, spool_ref, cw_ref, cb_ref, dtb_ref, alog_ref,
                       wp_ref, ps_ref,
                       xc_ref, dec_ref, dtx_ref, nconv_ref, npool_ref, yb_ref):
    for c0 in range(0, CONV_DIM, 512):
        cols = slice(c0, c0 + 512)
        taps = [sconv_ref[k, :, cols] for k in range(CONV_WIDTH - 1)]
        taps.append(xbc_ref[:, cols])
        xc_ref[:, cols] = _conv_silu(taps, cw_ref, cb_ref, cols)
    for k in range(CONV_WIDTH - 2):
        nconv_ref[k] = sconv_ref[k + 1]
    nconv_ref[CONV_WIDTH - 2] = xbc_ref[...]

    dt = _softplus(dt_ref[...] + dtb_ref[...])
    dec_ref[...] = jnp.exp(dt * (-jnp.exp(alog_ref[...])))
    hrow = lax.broadcasted_iota(jnp.int32, (DT_PAD, D_INNER), 0)
    ccol = lax.broadcasted_iota(jnp.int32, (DT_PAD, D_INNER), 1)
    expand = jnp.where((ccol >= hrow * HEAD_DIM) & (ccol < (hrow + 1) * HEAD_DIM), 1.0, 0.0).astype(BF16)
    dt_e = sum(_dot(part, expand) for part in _split3(dt))
    dtx_ref[...] = dt_e * xc_ref[:, 0:D_INNER]

    cnt_base = PAST_LEN + 1
    for gi, w in enumerate(POOL_WINDOWS):
        cols = slice(gi * POOL_GROUP_DIM, (gi + 1) * POOL_GROUP_DIM)
        u_g = u_ref[:, cols]
        s = u_g
        for k in range(1, w):
            s = s + spool_ref[POOL_BUF - k, :, cols]
        yb_ref[:, cols] = _pool_mix(s, u_g, float(min(cnt_base, w)), wp_ref, ps_ref, gi)
    for k in range(POOL_BUF - 1):
        npool_ref[k] = spool_ref[k + 1]
    npool_ref[POOL_BUF - 1] = u_ref[...]


def _sample_tok(xbc, dt, u, sconv, spool, cw, cb, dtb, alog, wp, ps):
    nb = xbc.shape[0]
    full = lambda a: pl.BlockSpec(a.shape, lambda i: (0,) * a.ndim)
    args = (xbc, dt, u, sconv, spool, cw, cb, dtb, alog, wp, ps)
    out_shape = [jax.ShapeDtypeStruct((nb, CONV_DIM), F32),
                 jax.ShapeDtypeStruct((nb, DT_PAD), F32),
                 jax.ShapeDtypeStruct((nb, D_INNER), F32),
                 jax.ShapeDtypeStruct(sconv.shape, F32),
                 jax.ShapeDtypeStruct(spool.shape, F32),
                 jax.ShapeDtypeStruct((nb, POOL_DIM), BF16)]
    return pl.pallas_call(
        _sample_tok_kernel,
        grid=(1,),
        in_specs=[full(a) for a in args],
        out_specs=[pl.BlockSpec(s.shape, lambda i, nd=len(s.shape): (0,) * nd) for s in out_shape],
        out_shape=out_shape,
        compiler_params=pltpu.CompilerParams(dimension_semantics=("arbitrary",),
                                             vmem_limit_bytes=VMEM_LIMIT),
        name="sample_tok",
    )(*args)


def _sample_ssm_kernel(st_ref, b_ref, c_ref, dec_ref, dtx_ref, xs_ref, z_ref, dskip_ref, sn_ref, ya_ref, nst_ref):
    grow = lax.broadcasted_iota(jnp.int32, (N_GROUPS, D_INNER), 0)
    gcol = lax.broadcasted_iota(jnp.int32, (N_GROUPS, D_INNER), 1)
    gmask = (gcol >= grow * GROUP_COLS) & (gcol < (grow + 1) * GROUP_COLS)
    for t in range(st_ref.shape[0]):
        dtx = jnp.where(gmask, dtx_ref[t], 0.0)
        x_hi, x_mid, _ = _split3(dtx)
        b_hi, b_mid, _ = _split3(b_ref[t])
        lhs = jnp.concatenate([x_hi.astype(F32), x_mid.astype(F32), x_hi.astype(F32)], axis=0)
        rhs = jnp.concatenate([b_hi.astype(F32), b_hi.astype(F32), b_mid.astype(F32)], axis=0)
        upd = lax.dot_general(lhs, rhs, (((0,), (0,)), ((), ())), preferred_element_type=F32)
        dec = dec_ref[t]
        c_b = c_ref[t].astype(BF16)
        y_parts = []
        for g in range(N_GROUPS):
            new_g = []
            for r in range(HEADS_PER_GROUP):
                h = g * HEADS_PER_GROUP + r
                scale = jnp.broadcast_to(dec[:, h:h + 1], (HEAD_DIM, D_STATE))
                new = st_ref[t, h] * scale + upd[h * HEAD_DIM:(h + 1) * HEAD_DIM, :]
                nst_ref[t, h] = new
                new_g.append(new.astype(BF16))
            y_g = _dot_nt(c_b, jnp.concatenate(new_g, axis=0))
            gcols = slice(g * GROUP_COLS, (g + 1) * GROUP_COLS)
            y_parts.append(y_g[g:g + 1, :] + dskip_ref[:, gcols] * xs_ref[t, :, gcols])
        y = jnp.concatenate(y_parts, axis=1)
        ya_ref[t] = _rms(y * _silu(z_ref[t]), sn_ref[...])


def _sample_ssm(state, b3, c3, dec3, dtx3, xs3, z3, dskip_e, sn):
    nb = state.shape[0]
    tb = _token_tile(nb, 4)
    tok3 = lambda i: (i, 0, 0)
    row3 = pl.BlockSpec((tb, 1, D_INNER), tok3)
    state_spec = pl.BlockSpec((tb, N_HEADS, HEAD_DIM, D_STATE), lambda i: (i, 0, 0, 0))
    return pl.pallas_call(
        _sample_ssm_kernel,
        grid=(nb // tb,),
        in_specs=[state_spec,
                  pl.BlockSpec((tb, N_GROUPS, D_STATE), tok3),
                  pl.BlockSpec((tb, N_GROUPS, D_STATE), tok3),
                  pl.BlockSpec((tb, 1, DT_PAD), tok3),
                  row3, row3, row3,
                  _const_spec(dskip_e.shape), _const_spec(sn.shape)],
        out_specs=[row3, state_spec],
        out_shape=[jax.ShapeDtypeStruct((nb, 1, D_INNER), F32),
                   jax.ShapeDtypeStruct(state.shape, F32)],
        compiler_params=pltpu.CompilerParams(dimension_semantics=("arbitrary",),
                                             vmem_limit_bytes=VMEM_LIMIT),
        name="sample_ssm",
    )(state, b3, c3, dec3, dtx3, xs3, z3, dskip_e, sn)


def _out_math(x, ya_in, yb_in, gpre_ref, wg_ref, wa_ref, wb_ref, wo_ref, gpost_ref, fpre_ref, w1_ref, w2_ref,
              fpost_ref):
    xn = _rms(x, gpre_ref[...]).astype(BF16)
    ya = _dot(ya_in.astype(BF16), wa_ref[...])
    yb = _dot(yb_in, wb_ref[...])
    merged = (jax.nn.sigmoid(_dot(xn, wg_ref[:, 0:D_MODEL])) * ya
              + jax.nn.sigmoid(_dot(xn, wg_ref[:, D_MODEL:])) * yb)
    mo = _dot(merged.astype(BF16), wo_ref[...])
    h = x + _rms(mo, gpost_ref[...])
    hn = _rms(h, fpre_ref[...]).astype(BF16)
    gate = _dot(hn, w1_ref[:, 0:D_FF])
    up = _dot(hn, w1_ref[:, D_FF:])
    f = _dot((_silu(gate) * up).astype(BF16), w2_ref[...])
    return h + _rms(f, fpost_ref[...])


def _out_kernel(x_ref, ya_ref, yb_ref, xs_ref, yas_ref, ybs_ref, gpre_ref, wint_hbm, wa_hbm, wb_hbm, wo_hbm, gpost_ref,
                fpre_ref, w1_hbm, w2_hbm, fpost_ref, o_ref, os_ref,
                wg_ref, wa_ref, wb_ref, wo_ref, w1_ref, w2_ref, stage, ssem):
    i = pl.program_id(0)
    n_prompt = pl.num_programs(0) - 1
    params = (gpre_ref, wg_ref, wa_ref, wb_ref, wo_ref, gpost_ref, fpre_ref, w1_ref, w2_ref, fpost_ref)

    @pl.when(i == 0)
    def _():
        chunks = _transposed_chunks(wint_hbm, IN_O_GATE, 2 * D_MODEL, wg_ref, 0)
        for src, dst in ((wa_hbm, wa_ref), (wb_hbm, wb_ref), (wo_hbm, wo_ref), (w1_hbm, w1_ref), (w2_hbm, w2_ref)):
            chunks += _plain_chunks(src, dst)
        _prep_weights(chunks, stage, ssem)

    @pl.when(i < n_prompt)
    def _():
        o_ref[...] = _out_math(x_ref[...], ya_ref[...], yb_ref[...], *params)

    @pl.when(i == n_prompt)
    def _():
        os_ref[...] = _out_math(xs_ref[...], yas_ref[...], ybs_ref[...], *params)


def _out(x, ya, yb, xs, yas, ybs, consts, tm):
    m = x.shape[0]
    n = m // tm
    nb = xs.shape[0]
    tok = lambda i: (jnp.minimum(i, n - 1), 0)
    big = [a.shape[0] >= D_MODEL and a.shape[1] >= D_MODEL for a in consts]
    resident = [(D_MODEL, 2 * D_MODEL)] + [a.shape for a, b in zip(consts, big) if b][1:]
    once = lambda a: pl.BlockSpec(a.shape, lambda i: (0, 0), pipeline_mode=pl.Buffered(1))
    return pl.pallas_call(
        _out_kernel,
        grid=(n + 1,),
        in_specs=[pl.BlockSpec((tm, D_MODEL), tok), pl.BlockSpec((tm, D_INNER), tok),
                  pl.BlockSpec((tm, POOL_DIM), tok), once(xs), once(yas), once(ybs)]
                 + [_HBM_SPEC if b else _const_spec(a.shape) for a, b in zip(consts, big)],
        out_specs=[pl.BlockSpec((tm, D_MODEL), tok), pl.BlockSpec((nb, D_MODEL), lambda i: (0, 0))],
        out_shape=[jax.ShapeDtypeStruct((m, D_MODEL), F32), jax.ShapeDtypeStruct((nb, D_MODEL), F32)],
        scratch_shapes=[pltpu.VMEM(shape, BF16) for shape in resident]
                       + [pltpu.VMEM((2, STAGE_ROWS, STAGE_COLS), F32), pltpu.SemaphoreType.DMA((2,))],
        compiler_params=pltpu.CompilerParams(dimension_semantics=("arbitrary",),
                                             vmem_limit_bytes=VMEM_LIMIT_OUT),
        name="out",
    )(x, ya, yb, xs, yas, ybs, *consts)


def _token_tile(m, cap):
    t = min(m, cap)
    assert m % t == 0
    return t


def kernel(x_prompt, x_sample, state_conv, state_ssm, state_pool, norm_mix_pre, norm_mix_post, norm_ffn_pre,
           norm_ffn_post, w_in, conv_w, conv_b, dt_bias, a_log, d_skip, ssm_norm, w_pool_group, pool_scale,
           w_branch_a, w_branch_b, w_out, w_ffn_in, w_ffn_out):
    bsz, seq, _ = x_prompt.shape
    nb, dec_seq, _ = x_sample.shape
    assert dec_seq == 1 and norm_mix_pre.shape[0] == 1
    l = 0
    wit = jnp.swapaxes(w_in[l], 0, 1)
    pad_h = lambda v: jnp.pad(v.astype(F32), (0, DT_PAD - N_HEADS)).reshape(1, DT_PAD)
    row = lambda v: v.astype(F32).reshape(1, -1)
    dtb, alog = pad_h(dt_bias[l]), pad_h(a_log[l])
    dskip_e = jnp.repeat(d_skip[l].astype(F32), HEAD_DIM).reshape(1, D_INNER)
    cw, cb = conv_w[l].astype(F32), row(conv_b[l])
    wp, ps = w_pool_group[l].astype(BF16), row(pool_scale[l])
    g_pre, sn = row(norm_mix_pre[l]), row(ssm_norm[l])
    out_consts = (g_pre, wit, w_branch_a[l], w_branch_b[l], w_out[l],
                  row(norm_mix_post[l]), row(norm_ffn_pre[l]), w_ffn_in[l], w_ffn_out[l],
                  row(norm_ffn_post[l]))

    xp = x_prompt.reshape(bsz * seq, D_MODEL)
    xs2 = x_sample.reshape(nb, D_MODEL)
    ya, yb, nconv_p, nssm_p, npool_p, z, xbc, u, dt = _mix_prompt(xp, xs2, g_pre, wit, cw, cb, dtb, alog, dskip_e, wp,
                                                                  ps, sn, bsz, seq)
    sconv = jnp.swapaxes(state_conv[l], 0, 1)
    spool = jnp.swapaxes(state_pool[l], 0, 1)
    xc, dec, dtx, nconv_s, npool_s, yb_s = _sample_tok(xbc, dt, u, sconv, spool, cw, cb, dtb, alog, wp, ps)
    b3 = xc[:, D_INNER:D_INNER + N_GROUPS * D_STATE].reshape(nb, N_GROUPS, D_STATE)
    c3 = xc[:, D_INNER + N_GROUPS * D_STATE:].reshape(nb, N_GROUPS, D_STATE)
    xs3 = xc[:, :D_INNER].reshape(nb, 1, D_INNER)
    ya3, nssm_s = _sample_ssm(state_ssm[l], b3, c3, dec.reshape(nb, 1, DT_PAD), dtx.reshape(nb, 1, D_INNER),
                              xs3, z.reshape(nb, 1, D_INNER), dskip_e, sn)
    out_p, out_s = _out(xp, ya, yb, xs2, ya3.reshape(nb, D_INNER), yb_s, out_consts, _token_tile(bsz * seq, 256))
    out_p = out_p.reshape(bsz, seq, D_MODEL)
    out_s = out_s.reshape(nb, 1, D_MODEL)

    return (out_p, out_s,
            nconv_p[None], nssm_p[None], npool_p[None],
            jnp.swapaxes(nconv_s, 0, 1)[None], nssm_s[None],
            jnp.swapaxes(npool_s, 0, 1)[None])
```

```python
import functools

import jax
import jax.numpy as jnp
from jax import lax
from jax.experimental import pallas as pl
from jax.experimental.pallas import tpu as pltpu

D_MODEL = 1024
D_INNER = 2048
HEAD_DIM = 64
N_HEADS = 32
N_GROUPS = 8
HEADS_PER_GROUP = 4
D_STATE = 128
CONV_WIDTH = 4
CONV_DIM = 4096
CHUNK = 128
POOL_DIM = 1024
POOL_WINDOWS = (2, 4, 8, 16)
POOL_GROUP_DIM = 256
POOL_BUF = 15
D_FF = 2816
EPS = 1e-6
PAST_LEN = 16384
LOG2E = 1.4426950408889634

LANES = 128
SUBLANES = 8
TILES = CHUNK // SUBLANES
GROUP_COLS = HEADS_PER_GROUP * HEAD_DIM
DT_PAD = LANES
MIX_WIDTH = D_INNER + CONV_DIM + POOL_DIM + DT_PAD
CONV_HALO = (CONV_WIDTH - 1) * SUBLANES
POOL_HALO = CHUNK
VMEM_LIMIT = 56 * 1024 * 1024
VMEM_LIMIT_OUT = 62 * 1024 * 1024

F32 = jnp.float32
BF16 = jnp.bfloat16


def _rms(x, g):
    y = x * lax.rsqrt(jnp.mean(x * x, axis=-1, keepdims=True) + EPS)
    return y * g


def _silu(x):
    return x * jax.nn.sigmoid(x)


def _softplus(x):
    return jnp.maximum(x, 0.0) + jnp.log(1.0 + jnp.exp(-jnp.abs(x)))


def _split3(x):
    hi = x.astype(BF16)
    r = x - hi.astype(F32)
    mid = r.astype(BF16)
    lo = (r - mid.astype(F32)).astype(BF16)
    return hi, mid, lo


def _dot(a, b):
    return jnp.dot(a, b, preferred_element_type=F32)


def _dot_nt(a, b):
    return lax.dot_general(a, b, (((1,), (1,)), ((), ())), preferred_element_type=F32)


def _const_spec(shape):
    nd = len(shape)
    return pl.BlockSpec(shape, lambda *_: (0,) * nd)


_HBM_SPEC = pl.BlockSpec(memory_space=pl.ANY)


def _load_resident(pairs, sem):
    copies = [pltpu.make_async_copy(src, dst, sem.at[i]) for i, (src, dst) in enumerate(pairs)]
    for c in copies:
        c.start()
    for c in copies:
        c.wait()


STAGE_ROWS = 512
STAGE_COLS = 1024
PREP_ROWS = 256


def _prep_weights(chunks, stage, sem):
    def copy(i):
        src, r0, nr, c0, nc, _ = chunks[i]
        return pltpu.make_async_copy(src.at[pl.ds(r0, nr), pl.ds(c0, nc)],
                                     stage.at[i % 2, pl.ds(0, nr), pl.ds(0, nc)], sem.at[i % 2])

    copy(0).start()
    for i, (_, _, nr, _, nc, emit) in enumerate(chunks):
        if i + 1 < len(chunks):
            copy(i + 1).start()
        copy(i).wait()
        step = min(PREP_ROWS, nr)
        for r in range(0, nr, step):
            emit(slice(r, r + step), stage[i % 2, r:r + step, 0:nc])


def _plain_chunks(src, dst):
    rows, cols = src.shape

    def emit_at(r0, c0, nc):
        def emit(rs, v):
            dst[r0 + rs.start:r0 + rs.stop, c0:c0 + nc] = v.astype(BF16)
        return emit

    return [(src, r0, min(STAGE_ROWS, rows - r0), c0, min(STAGE_COLS, cols - c0),
             emit_at(r0, c0, min(STAGE_COLS, cols - c0)))
            for r0 in range(0, rows, STAGE_ROWS) for c0 in range(0, cols, STAGE_COLS)]


IN_O_DT = D_INNER + CONV_DIM
IN_O_POOL = IN_O_DT + N_HEADS
IN_O_GATE = IN_O_POOL + POOL_DIM


def _transposed_chunks(src_t, row0, n_rows, dst, dst_col0):
    def emit_at(j0):
        def emit(rs, v):
            dst[:, dst_col0 + j0 + rs.start:dst_col0 + j0 + rs.stop] = v.T.astype(BF16)
        return emit

    return [(src_t, row0 + j0, min(STAGE_ROWS, n_rows - j0), 0, D_MODEL, emit_at(j0))
            for j0 in range(0, n_rows, STAGE_ROWS)]


def _mix_weight_chunks(w_in_t, w_ref):
    def emit_dt(rs, v):
        vt = v.T
        lane = lax.broadcasted_iota(jnp.int32, vt.shape, 1)
        w_ref[:, MIX_WIDTH - DT_PAD:MIX_WIDTH] = jnp.where(lane < N_HEADS, vt, 0.0).astype(BF16)

    chunks = _transposed_chunks(w_in_t, 0, IN_O_DT, w_ref, 0)
    chunks += _transposed_chunks(w_in_t, IN_O_POOL, POOL_DIM, w_ref, IN_O_DT)
    chunks.append((w_in_t, IN_O_DT, DT_PAD, 0, D_MODEL, emit_dt))
    return chunks


def _time_of_row(r):
    return (r & (SUBLANES - 1)) * TILES + (r >> 3)


def _conv_silu(taps, cw_ref, cb_ref, cols):
    acc = cb_ref[:, cols]
    for k, t in enumerate(taps):
        acc = acc + t * cw_ref[k:k + 1, cols]
    return _silu(acc)


def _pool_mix(win_sum, u, cnt, wp_ref, ps_ref, gi):
    cols = slice(gi * POOL_GROUP_DIM, (gi + 1) * POOL_GROUP_DIM)
    d = win_sum / cnt - u
    mixed = _dot(d.astype(BF16), wp_ref[gi])
    return (mixed * ps_ref[:, cols]).astype(BF16)


def _project(x_ref, gpre_ref, w_ref, dtb_ref, to_perm, xnp_buf, xbc_buf, z_buf, u_buf, dt_buf):
    xn = _rms(x_ref[...], gpre_ref[...]).astype(BF16)
    xnp_buf[...] = _dot(to_perm, xn).astype(BF16)
    yield

    def proj(c0, width):
        tile = 256
        return jnp.concatenate([_dot(xnp_buf[...], w_ref[:, c:c + min(tile, c0 + width - c)])
                                for c in range(c0, c0 + width, tile)], axis=1)

    piece = 512
    for c0 in range(0, CONV_DIM, piece):
        xbc_buf[CONV_HALO:, c0:c0 + piece] = proj(D_INNER + c0, piece)
        yield
    o_u = D_INNER + CONV_DIM
    for c0 in range(0, POOL_DIM, piece):
        u_buf[POOL_HALO:, c0:c0 + piece] = proj(o_u + c0, piece)
        yield
    dt_buf[...] = _softplus(proj(MIX_WIDTH - DT_PAD, DT_PAD) + dtb_ref[...])
    for c0 in range(0, D_INNER, piece):
        z_buf[:, c0:c0 + piece] = proj(c0, piece)
        yield


def _run(gen):
    for _ in gen:
        pass


def _interleave(main, side, side_after):
    next(side)
    i = 0
    while next(main, StopIteration) is not StopIteration:
        for _ in range(side_after(i)):
            next(side, None)
        i += 1
    _run(side)


def _project_pieces_after(i):
    return i % 2


def _sequence(xbc_buf, z_buf, u_buf, dt_buf, xc_buf, y_buf, ctail, ptail, ht_ref,
              cw_ref, cb_ref, alog_ref, dskip_ref, wp_ref, ps_ref, sn_ref, masks, ya_ref, yb_ref, rows, tile_in_seq):
    q = CHUNK
    to_nat, causal, tri, lo_half, t_col = masks

    for n, j in enumerate(range(TILES - (CONV_WIDTH - 1), TILES)):
        src = CONV_HALO + j * SUBLANES
        xbc_buf[n * SUBLANES + 1:(n + 1) * SUBLANES, :] = xbc_buf[src:src + SUBLANES - 1, :]
        xbc_buf[n * SUBLANES:n * SUBLANES + 1, :] = ctail[n:n + 1, :]
    for c0 in range(0, CONV_DIM, 512):
        cols = slice(c0, c0 + 512)
        taps = []
        for k in range(CONV_WIDTH):
            start = CONV_HALO - (CONV_WIDTH - 1 - k) * SUBLANES
            taps.append(xbc_buf[start:start + q, cols])
        xc = _conv_silu(taps, cw_ref, cb_ref, cols)
        xc_buf[:, cols] = xc
        yield
    for n, j in enumerate(range(TILES - (CONV_WIDTH - 1), TILES)):
        src = CONV_HALO + j * SUBLANES + SUBLANES - 1
        ctail[n:n + 1, :] = xbc_buf[src:src + 1, :]

    dt = dt_buf[...]
    da = dt * (-jnp.exp(alog_ref[...]))
    acum = sum(_dot(tri, part) for part in _split3(da)) * LOG2E
    acum_last = acum[q - 1:q, :]
    e_last = jnp.exp2(acum_last)
    w_end_t = (jnp.exp2(acum_last - acum) * dt).T
    acum_t = acum.T
    dt_t = dt.T
    lo_half_row = lo_half[0:1, :]

    ssq = jnp.zeros((q, 1), F32)
    for g in range(N_GROUPS):
        b_g = xc_buf[:, D_INNER + g * D_STATE:D_INNER + (g + 1) * D_STATE]
        c_off = D_INNER + N_GROUPS * D_STATE + g * D_STATE
        c_b = xc_buf[:, c_off:c_off + D_STATE].astype(BF16)
        cb = _dot_nt(c_b, b_g.astype(BF16))
        b_t = b_g.T
        y_inter = _dot(c_b, ht_ref[g].astype(BF16))
        for pair in range(HEADS_PER_GROUP // 2):
            h0 = g * HEADS_PER_GROUP + 2 * pair
            h1 = h0 + 1
            pcols = slice(h0 * HEAD_DIM, (h0 + 2) * HEAD_DIM)
            scols = slice(pair * LANES, (pair + 1) * LANES)
            x_pair = xc_buf[:, pcols]
            ms, bws, e_cols = [], [], []
            for h in (h0, h1):
                a_col = jnp.broadcast_to(acum[:, h:h + 1], (q, q))
                seg = a_col - acum_t[h:h + 1, :]
                decay = jnp.exp2(jnp.where(causal, seg, -jnp.inf))
                ms.append((cb * decay * dt_t[h:h + 1, :]).astype(BF16))
                bws.append((b_t * w_end_t[h:h + 1, :]).astype(BF16))
                e_cols.append(jnp.exp2(a_col))
            x_top = jnp.where(lo_half, x_pair, 0.0).astype(BF16)
            x_bot = jnp.where(lo_half, 0.0, x_pair).astype(BF16)
            x_diag = jnp.concatenate([x_top, x_bot], axis=0)
            y_intra = _dot(jnp.concatenate(ms, axis=1), x_diag)
            e_pair = jnp.where(lo_half, e_cols[0], e_cols[1])
            y_pair = y_intra + y_inter[:, scols] * e_pair + dskip_ref[:, pcols] * x_pair
            gated = y_pair * _silu(z_buf[:, pcols])
            y_buf[:, pcols] = gated
            ssq = ssq + jnp.sum(gated * gated, axis=-1, keepdims=True)
            upd = _dot(jnp.concatenate(bws, axis=1), x_diag)
            e_last_pair = jnp.where(lo_half_row, e_last[:, h0:h0 + 1], e_last[:, h1:h1 + 1])
            ht_ref[g, :, scols] = ht_ref[g, :, scols] * e_last_pair + upd
            yield

    inv = lax.rsqrt(ssq * (1.0 / D_INNER) + EPS)
    for c0 in range(0, D_INNER, 1024):
        cols = slice(c0, c0 + 1024)
        ya = (y_buf[:, cols] * inv * sn_ref[:, cols]).astype(BF16)
        ya_ref[rows, cols] = _dot(to_nat, ya).astype(BF16)
        yield

    u_buf[1:q, :] = u_buf[POOL_HALO:POOL_HALO + q - 1, :]
    for j in range(1, TILES):
        u_buf[j * SUBLANES:j * SUBLANES + 1, :] = ptail[j - 1:j, :]
    pos = tile_in_seq * q + t_col
    yb_parts = []
    for gi, w in enumerate(POOL_WINDOWS):
        cols = slice(gi * POOL_GROUP_DIM, (gi + 1) * POOL_GROUP_DIM)
        u_g = u_buf[POOL_HALO:POOL_HALO + q, cols]
        s = u_g
        for k in range(1, w):
            s = s + u_buf[POOL_HALO - k * SUBLANES:POOL_HALO - k * SUBLANES + q, cols]
        cnt = jnp.minimum(pos + 1, w).astype(F32)
        yb_parts.append(_pool_mix(s, u_g, cnt, wp_ref, ps_ref, gi))
        yield
    yb_ref[rows, :] = _dot(to_nat, jnp.concatenate(yb_parts, axis=1)).astype(BF16)
    for j in range(1, TILES):
        src = POOL_HALO + j * SUBLANES + SUBLANES - 1
        ptail[j - 1:j, :] = u_buf[src:src + 1, :]


def _project_plain(x_ref, g_ref, w_ref, out_refs):
    xn = _rms(x_ref[...], g_ref[...]).astype(BF16)
    off = 0
    for o_ref in out_refs:
        width = o_ref.shape[1]
        for c in range(0, width, 1024):
            cw = min(1024, width - c)
            o_ref[:, c:c + cw] = _dot(xn, w_ref[:, off + c:off + c + cw])
        off += width


def _mix_kernel(x0_ref, x1_ref, x2_ref, xs_ref, gpre_ref, w_hbm, cw_ref, cb_ref, dtb_ref, alog_ref, dskip_ref,
                wp_hbm, ps_ref, sn_ref,
                ya_ref, yb_ref, nconv_ref, nssm_ref, npool_ref, zs_ref, xbcs_ref, us_ref, dts_ref,
                xbc0, z0, u0, dt0, xbc1, z1, u1, dt1, xc0, y0, xc1, y1, ctail, ptail, ht_ref,
                w_ref, wp_ref, wsem, xnp_buf, stage, ssem, *, tiles_per_seq):
    q = CHUNK
    k = pl.program_id(0)
    tile_in_seq = lax.rem(2 * k, tiles_per_seq)
    row = lax.broadcasted_iota(jnp.int32, (q, q), 0)
    col = lax.broadcasted_iota(jnp.int32, (q, q), 1)
    to_perm = jnp.where(col == _time_of_row(row), 1.0, 0.0).astype(BF16)
    to_nat = jnp.where(_time_of_row(col) == row, 1.0, 0.0).astype(BF16)
    causal = _time_of_row(row) >= _time_of_row(col)
    tri = jnp.where(causal, 1.0, 0.0).astype(BF16)
    t_col = _time_of_row(lax.broadcasted_iota(jnp.int32, (q, 1), 0))
    masks = (to_nat, causal, tri, col < HEAD_DIM, t_col)
    set0 = (xbc0, z0, u0, dt0)
    set1 = (xbc1, z1, u1, dt1)
    state = (ctail, ptail, ht_ref)
    consts = (cw_ref, cb_ref, alog_ref, dskip_ref, wp_ref, ps_ref, sn_ref, masks)

    @pl.when(k == 0)
    def _():
        _load_resident([(wp_hbm, wp_ref)], wsem)
        _prep_weights(_mix_weight_chunks(w_hbm, w_ref), stage, ssem)
        _run(_project(x0_ref, gpre_ref, w_ref, dtb_ref, to_perm, xnp_buf, *set0))

    @pl.when(tile_in_seq == 0)
    def _():
        ctail[...] = jnp.zeros(ctail.shape, F32)
        ptail[...] = jnp.zeros(ptail.shape, F32)
        ht_ref[...] = jnp.zeros(ht_ref.shape, F32)

    _interleave(_sequence(*set0, xc0, y0, *state, *consts, ya_ref, yb_ref, slice(0, q), tile_in_seq),
                _project(x1_ref, gpre_ref, w_ref, dtb_ref, to_perm, xnp_buf, *set1), _project_pieces_after)
    _interleave(_sequence(*set1, xc1, y1, *state, *consts, ya_ref, yb_ref, slice(q, 2 * q), tile_in_seq + 1),
                _project(x2_ref, gpre_ref, w_ref, dtb_ref, to_perm, xnp_buf, *set0), _project_pieces_after)

    @pl.when(tile_in_seq == tiles_per_seq - 2)
    def _():
        nconv_ref[0] = ctail[0:CONV_WIDTH - 1, :]
        npool_ref[0] = ptail[0:POOL_BUF, :]
        for g in range(N_GROUPS):
            nssm_ref[0, g * HEADS_PER_GROUP:(g + 1) * HEADS_PER_GROUP] = (
                ht_ref[g].T.reshape(HEADS_PER_GROUP, HEAD_DIM, D_STATE))

    @pl.when(k == pl.num_programs(0) - 1)
    def _():
        _project_plain(xs_ref, gpre_ref, w_ref, (zs_ref, xbcs_ref, us_ref, dts_ref))


def _mix_prompt(x, xs, gpre, w, cw, cb, dtb, alog, dskip_e, wp, ps, sn, bsz, seq):
    q = CHUNK
    nt = seq // q
    n_tiles = bsz * nt
    assert seq % (2 * q) == 0
    steps_per_seq = nt // 2
    m = bsz * seq
    consts = (gpre, w, cw, cb, dtb, alog, dskip_e, wp, ps, sn)
    nb = xs.shape[0]
    sample_widths = (D_INNER, CONV_DIM, POOL_DIM, DT_PAD)
    pair = lambda k: (k, 0)
    per_b = lambda k: (k // steps_per_seq, 0, 0)
    proj_set = [pltpu.VMEM((CONV_HALO + q, CONV_DIM), F32), pltpu.VMEM((q, D_INNER), F32),
                pltpu.VMEM((POOL_HALO + q, POOL_DIM), F32), pltpu.VMEM((q, DT_PAD), F32)]
    seq_set = [pltpu.VMEM((q, CONV_DIM), F32), pltpu.VMEM((q, D_INNER), F32)]
    return pl.pallas_call(
        functools.partial(_mix_kernel, tiles_per_seq=nt),
        grid=(n_tiles // 2,),
        in_specs=[pl.BlockSpec((q, D_MODEL), lambda k: (0, 0)),
                  pl.BlockSpec((q, D_MODEL), lambda k: (2 * k + 1, 0)),
                  pl.BlockSpec((q, D_MODEL), lambda k: (jnp.minimum(2 * k + 2, n_tiles - 1), 0)),
                  _const_spec(xs.shape)]
                 + [_HBM_SPEC if a is w or a is wp else _const_spec(a.shape) for a in consts],
        out_specs=[pl.BlockSpec((2 * q, D_INNER), pair),
                   pl.BlockSpec((2 * q, POOL_DIM), pair),
                   pl.BlockSpec((1, CONV_WIDTH - 1, CONV_DIM), per_b),
                   pl.BlockSpec((1, N_HEADS, HEAD_DIM, D_STATE), lambda k: (k // steps_per_seq, 0, 0, 0)),
                   pl.BlockSpec((1, POOL_BUF, POOL_DIM), per_b)]
                  + [_const_spec((nb, wd)) for wd in sample_widths],
        out_shape=[jax.ShapeDtypeStruct((m, D_INNER), BF16),
                   jax.ShapeDtypeStruct((m, POOL_DIM), BF16),
                   jax.ShapeDtypeStruct((bsz, CONV_WIDTH - 1, CONV_DIM), F32),
                   jax.ShapeDtypeStruct((bsz, N_HEADS, HEAD_DIM, D_STATE), F32),
                   jax.ShapeDtypeStruct((bsz, POOL_BUF, POOL_DIM), F32)]
                  + [jax.ShapeDtypeStruct((nb, wd), F32) for wd in sample_widths],
        scratch_shapes=proj_set + proj_set + seq_set + seq_set + [
            pltpu.VMEM((SUBLANES, CONV_DIM), F32),
            pltpu.VMEM((2 * SUBLANES, POOL_DIM), F32),
            pltpu.VMEM((N_GROUPS, D_STATE, GROUP_COLS), F32),
            pltpu.VMEM((D_MODEL, MIX_WIDTH), BF16), pltpu.VMEM(wp.shape, BF16), pltpu.SemaphoreType.DMA((1,)),
            pltpu.VMEM((q, D_MODEL), BF16),
            pltpu.VMEM((2, STAGE_ROWS, STAGE_COLS), F32), pltpu.SemaphoreType.DMA((2,))],
        compiler_params=pltpu.CompilerParams(dimension_semantics=("arbitrary",),
                                             vmem_limit_bytes=VMEM_LIMIT),
        name="mix_prompt",
    )(x, x, x, xs, *consts)


def _sample_tok_kernel(xbc_ref, dt_ref, u_ref, sconv_ref, spool_ref, cw_ref, cb_ref, dtb_ref, alog_ref,
                       wp_ref, ps_ref,
                       xc_ref, dec_ref, dtx_ref, nconv_ref, npool_ref, yb_ref):
    for c0 in range(0, CONV_DIM, 512):
        cols = slice(c0, c0 + 512)
        taps = [sconv_ref[k, :, cols] for k in range(CONV_WIDTH - 1)]
        taps.append(xbc_ref[:, cols])
        xc_ref[:, cols] = _conv_silu(taps, cw_ref, cb_ref, cols)
    for k in range(CONV_WIDTH - 2):
        nconv_ref[k] = sconv_ref[k + 1]
    nconv_ref[CONV_WIDTH - 2] = xbc_ref[...]

    dt = _softplus(dt_ref[...] + dtb_ref[...])
    dec_ref[...] = jnp.exp(dt * (-jnp.exp(alog_ref[...])))
    hrow = lax.broadcasted_iota(jnp.int32, (DT_PAD, D_INNER), 0)
    ccol = lax.broadcasted_iota(jnp.int32, (DT_PAD, D_INNER), 1)
    expand = jnp.where((ccol >= hrow * HEAD_DIM) & (ccol < (hrow + 1) * HEAD_DIM), 1.0, 0.0).astype(BF16)
    dt_e = sum(_dot(part, expand) for part in _split3(dt))
    dtx_ref[...] = dt_e * xc_ref[:, 0:D_INNER]

    cnt_base = PAST_LEN + 1
    for gi, w in enumerate(POOL_WINDOWS):
        cols = slice(gi * POOL_GROUP_DIM, (gi + 1) * POOL_GROUP_DIM)
        u_g = u_ref[:, cols]
        s = u_g
        for k in range(1, w):
            s = s + spool_ref[POOL_BUF - k, :, cols]
        yb_ref[:, cols] = _pool_mix(s, u_g, float(min(cnt_base, w)), wp_ref, ps_ref, gi)
    for k in range(POOL_BUF - 1):
        npool_ref[k] = spool_ref[k + 1]
    npool_ref[POOL_BUF - 1] = u_ref[...]


def _sample_tok(xbc, dt, u, sconv, spool, cw, cb, dtb, alog, wp, ps):
    nb = xbc.shape[0]
    full = lambda a: pl.BlockSpec(a.shape, lambda i: (0,) * a.ndim)
    args = (xbc, dt, u, sconv, spool, cw, cb, dtb, alog, wp, ps)
    out_shape = [jax.ShapeDtypeStruct((nb, CONV_DIM), F32),
                 jax.ShapeDtypeStruct((nb, DT_PAD), F32),
                 jax.ShapeDtypeStruct((nb, D_INNER), F32),
                 jax.ShapeDtypeStruct(sconv.shape, F32),
                 jax.ShapeDtypeStruct(spool.shape, F32),
                 jax.ShapeDtypeStruct((nb, POOL_DIM), BF16)]
    return pl.pallas_call(
        _sample_tok_kernel,
        grid=(1,),
        in_specs=[full(a) for a in args],
        out_specs=[pl.BlockSpec(s.shape, lambda i, nd=len(s.shape): (0,) * nd) for s in out_shape],
        out_shape=out_shape,
        compiler_params=pltpu.CompilerParams(dimension_semantics=("arbitrary",),
                                             vmem_limit_bytes=VMEM_LIMIT),
        name="sample_tok",
    )(*args)


def _sample_ssm_kernel(st_ref, b_ref, c_ref, dec_ref, dtx_ref, xc_ref, z_ref, dskip_ref, sn_ref, ya_ref, nst_ref):
    grow = lax.broadcasted_iota(jnp.int32, (N_GROUPS, D_INNER), 0)
    gcol = lax.broadcasted_iota(jnp.int32, (N_GROUPS, D_INNER), 1)
    gmask = (gcol >= grow * GROUP_COLS) & (gcol < (grow + 1) * GROUP_COLS)
    for t in range(st_ref.shape[0]):
        tok = slice(t, t + 1)
        dtx = jnp.where(gmask, dtx_ref[tok, :], 0.0)
        x_hi, x_mid, _ = _split3(dtx)
        b_hi, b_mid, _ = _split3(b_ref[t])
        lhs = jnp.concatenate([x_hi.astype(F32), x_mid.astype(F32), x_hi.astype(F32)], axis=0)
        rhs = jnp.concatenate([b_hi.astype(F32), b_hi.astype(F32), b_mid.astype(F32)], axis=0)
        upd = lax.dot_general(lhs, rhs, (((0,), (0,)), ((), ())), preferred_element_type=F32)
        dec = dec_ref[tok, :]
        c_b = c_ref[t].astype(BF16)
        y_parts = []
        for g in range(N_GROUPS):
            new_g = []
            for r in range(HEADS_PER_GROUP):
                h = g * HEADS_PER_GROUP + r
                scale = jnp.broadcast_to(dec[:, h:h + 1], (HEAD_DIM, D_STATE))
                new = st_ref[t, h] * scale + upd[h * HEAD_DIM:(h + 1) * HEAD_DIM, :]
                nst_ref[t, h] = new
                new_g.append(new.astype(BF16))
            y_g = _dot_nt(c_b, jnp.concatenate(new_g, axis=0))
            gcols = slice(g * GROUP_COLS, (g + 1) * GROUP_COLS)
            y_parts.append(y_g[g:g + 1, :] + dskip_ref[:, gcols] * xc_ref[tok, gcols])
        y = jnp.concatenate(y_parts, axis=1)
        ya_ref[tok, :] = _rms(y * _silu(z_ref[tok, :]), sn_ref[...])


def _sample_ssm(state, b3, c3, dec, dtx, xc, z, dskip_e, sn):
    nb = state.shape[0]
    tb = _token_tile(nb, SUBLANES)
    tok2 = lambda i: (i, 0)
    tok3 = lambda i: (i, 0, 0)
    state_spec = pl.BlockSpec((tb, N_HEADS, HEAD_DIM, D_STATE), lambda i: (i, 0, 0, 0))
    return pl.pallas_call(
        _sample_ssm_kernel,
        grid=(nb // tb,),
        in_specs=[state_spec,
                  pl.BlockSpec((tb, N_GROUPS, D_STATE), tok3),
                  pl.BlockSpec((tb, N_GROUPS, D_STATE), tok3),
                  pl.BlockSpec((tb, DT_PAD), tok2),
                  pl.BlockSpec((tb, D_INNER), tok2),
                  pl.BlockSpec((tb, D_INNER), tok2),
                  pl.BlockSpec((tb, D_INNER), tok2),
                  _const_spec(dskip_e.shape), _const_spec(sn.shape)],
        out_specs=[pl.BlockSpec((tb, D_INNER), tok2), state_spec],
        out_shape=[jax.ShapeDtypeStruct((nb, D_INNER), F32),
                   jax.ShapeDtypeStruct(state.shape, F32)],
        compiler_params=pltpu.CompilerParams(dimension_semantics=("arbitrary",),
                                             vmem_limit_bytes=VMEM_LIMIT),
        name="sample_ssm",
    )(state, b3, c3, dec, dtx, xc, z, dskip_e, sn)


def _out_math(x, ya_in, yb_in, gpre_ref, wg_ref, wa_ref, wb_ref, wo_ref, gpost_ref, fpre_ref, w1_ref, w2_ref,
              fpost_ref):
    xn = _rms(x, gpre_ref[...]).astype(BF16)
    ya = _dot(ya_in.astype(BF16), wa_ref[...])
    yb = _dot(yb_in, wb_ref[...])
    merged = (jax.nn.sigmoid(_dot(xn, wg_ref[:, 0:D_MODEL])) * ya
              + jax.nn.sigmoid(_dot(xn, wg_ref[:, D_MODEL:])) * yb)
    mo = _dot(merged.astype(BF16), wo_ref[...])
    h = x + _rms(mo, gpost_ref[...])
    hn = _rms(h, fpre_ref[...]).astype(BF16)
    gate = _dot(hn, w1_ref[:, 0:D_FF])
    up = _dot(hn, w1_ref[:, D_FF:])
    f = _dot((_silu(gate) * up).astype(BF16), w2_ref[...])
    return h + _rms(f, fpost_ref[...])


def _out_kernel(x_ref, ya_ref, yb_ref, xs_ref, yas_ref, ybs_ref, gpre_ref, wint_hbm, wa_hbm, wb_hbm, wo_hbm, gpost_ref,
                fpre_ref, w1_hbm, w2_hbm, fpost_ref, o_ref, os_ref,
                wg_ref, wa_ref, wb_ref, wo_ref, w1_ref, w2_ref, stage, ssem):
    i = pl.program_id(0)
    n_prompt = pl.num_programs(0) - 1
    params = (gpre_ref, wg_ref, wa_ref, wb_ref, wo_ref, gpost_ref, fpre_ref, w1_ref, w2_ref, fpost_ref)

    @pl.when(i == 0)
    def _():
        chunks = _transposed_chunks(wint_hbm, IN_O_GATE, 2 * D_MODEL, wg_ref, 0)
        for src, dst in ((wa_hbm, wa_ref), (wb_hbm, wb_ref), (wo_hbm, wo_ref), (w1_hbm, w1_ref), (w2_hbm, w2_ref)):
            chunks += _plain_chunks(src, dst)
        _prep_weights(chunks, stage, ssem)

    @pl.when(i < n_prompt)
    def _():
        o_ref[...] = _out_math(x_ref[...], ya_ref[...], yb_ref[...], *params)

    @pl.when(i == n_prompt)
    def _():
        os_ref[...] = _out_math(xs_ref[...], yas_ref[...], ybs_ref[...], *params)


def _out(x, ya, yb, xs, yas, ybs, consts, tm):
    m = x.shape[0]
    n = m // tm
    nb = xs.shape[0]
    tok = lambda i: (jnp.minimum(i, n - 1), 0)
    big = [a.shape[0] >= D_MODEL and a.shape[1] >= D_MODEL for a in consts]
    resident = [(D_MODEL, 2 * D_MODEL)] + [a.shape for a, b in zip(consts, big) if b][1:]
    once = lambda a: pl.BlockSpec(a.shape, lambda i: (0, 0), pipeline_mode=pl.Buffered(1))
    return pl.pallas_call(
        _out_kernel,
        grid=(n + 1,),
        in_specs=[pl.BlockSpec((tm, D_MODEL), tok), pl.BlockSpec((tm, D_INNER), tok),
                  pl.BlockSpec((tm, POOL_DIM), tok), once(xs), once(yas), once(ybs)]
                 + [_HBM_SPEC if b else _const_spec(a.shape) for a, b in zip(consts, big)],
        out_specs=[pl.BlockSpec((tm, D_MODEL), tok), pl.BlockSpec((nb, D_MODEL), lambda i: (0, 0))],
        out_shape=[jax.ShapeDtypeStruct((m, D_MODEL), F32), jax.ShapeDtypeStruct((nb, D_MODEL), F32)],
        scratch_shapes=[pltpu.VMEM(shape, BF16) for shape in resident]
                       + [pltpu.VMEM((2, STAGE_ROWS, STAGE_COLS), F32), pltpu.SemaphoreType.DMA((2,))],
        compiler_params=pltpu.CompilerParams(dimension_semantics=("arbitrary",),
                                             vmem_limit_bytes=VMEM_LIMIT_OUT),
        name="out",
    )(x, ya, yb, xs, yas, ybs, *consts)


def _token_tile(m, cap):
    t = min(m, cap)
    assert m % t == 0
    return t


def kernel(x_prompt, x_sample, state_conv, state_ssm, state_pool, norm_mix_pre, norm_mix_post, norm_ffn_pre,
           norm_ffn_post, w_in, conv_w, conv_b, dt_bias, a_log, d_skip, ssm_norm, w_pool_group, pool_scale,
           w_branch_a, w_branch_b, w_out, w_ffn_in, w_ffn_out):
    bsz, seq, _ = x_prompt.shape
    nb, dec_seq, _ = x_sample.shape
    assert dec_seq == 1 and norm_mix_pre.shape[0] == 1
    l = 0
    wit = jnp.swapaxes(w_in[l], 0, 1)
    pad_h = lambda v: jnp.pad(v.astype(F32), (0, DT_PAD - N_HEADS)).reshape(1, DT_PAD)
    row = lambda v: v.astype(F32).reshape(1, -1)
    dtb, alog = pad_h(dt_bias[l]), pad_h(a_log[l])
    dskip_e = jnp.repeat(d_skip[l].astype(F32), HEAD_DIM).reshape(1, D_INNER)
    cw, cb = conv_w[l].astype(F32), row(conv_b[l])
    wp, ps = w_pool_group[l].astype(BF16), row(pool_scale[l])
    g_pre, sn = row(norm_mix_pre[l]), row(ssm_norm[l])
    out_consts = (g_pre, wit, w_branch_a[l], w_branch_b[l], w_out[l],
                  row(norm_mix_post[l]), row(norm_ffn_pre[l]), w_ffn_in[l], w_ffn_out[l],
                  row(norm_ffn_post[l]))

    xp = x_prompt.reshape(bsz * seq, D_MODEL)
    xs2 = x_sample.reshape(nb, D_MODEL)
    ya, yb, nconv_p, nssm_p, npool_p, z, xbc, u, dt = _mix_prompt(xp, xs2, g_pre, wit, cw, cb, dtb, alog, dskip_e, wp,
                                                                  ps, sn, bsz, seq)
    sconv = jnp.swapaxes(state_conv[l], 0, 1)
    spool = jnp.swapaxes(state_pool[l], 0, 1)
    xc, dec, dtx, nconv_s, npool_s, yb_s = _sample_tok(xbc, dt, u, sconv, spool, cw, cb, dtb, alog, wp, ps)
    b3 = xc[:, D_INNER:D_INNER + N_GROUPS * D_STATE].reshape(nb, N_GROUPS, D_STATE)
    c3 = xc[:, D_INNER + N_GROUPS * D_STATE:].reshape(nb, N_GROUPS, D_STATE)
    ya_s, nssm_s = _sample_ssm(state_ssm[l], b3, c3, dec, dtx, xc, z, dskip_e, sn)
    out_p, out_s = _out(xp, ya, yb, xs2, ya_s, yb_s, out_consts, _token_tile(bsz * seq, 256))
    out_p = out_p.reshape(bsz, seq, D_MODEL)
    out_s = out_s.reshape(nb, 1, D_MODEL)

    return (out_p, out_s,
            nconv_p[None], nssm_p[None], npool_p[None],
            jnp.swapaxes(nconv_s, 0, 1)[None], nssm_s[None],
            jnp.swapaxes(npool_s, 0, 1)[None])
```

```python
import functools

import jax
import jax.numpy as jnp
from jax import lax
from jax.experimental import pallas as pl
from jax.experimental.pallas import tpu as pltpu

D_MODEL = 1024
D_INNER = 2048
HEAD_DIM = 64
N_HEADS = 32
N_GROUPS = 8
HEADS_PER_GROUP = 4
D_STATE = 128
CONV_WIDTH = 4
CONV_DIM = 4096
CHUNK = 128
POOL_DIM = 1024
POOL_WINDOWS = (2, 4, 8, 16)
POOL_GROUP_DIM = 256
POOL_BUF = 15
D_FF = 2816
EPS = 1e-6
PAST_LEN = 16384
LOG2E = 1.4426950408889634

LANES = 128
SUBLANES = 8
TILES = CHUNK // SUBLANES
GROUP_COLS = HEADS_PER_GROUP * HEAD_DIM
DT_PAD = LANES
MIX_WIDTH = D_INNER + CONV_DIM + POOL_DIM + DT_PAD
CONV_HALO = (CONV_WIDTH - 1) * SUBLANES
POOL_HALO = CHUNK
VMEM_LIMIT = 56 * 1024 * 1024
VMEM_LIMIT_OUT = 62 * 1024 * 1024

F32 = jnp.float32
BF16 = jnp.bfloat16


def _rms(x, g):
    y = x * lax.rsqrt(jnp.mean(x * x, axis=-1, keepdims=True) + EPS)
    return y * g


def _silu(x):
    return x * jax.nn.sigmoid(x)


def _softplus(x):
    return jnp.maximum(x, 0.0) + jnp.log(1.0 + jnp.exp(-jnp.abs(x)))


def _split3(x):
    hi = x.astype(BF16)
    r = x - hi.astype(F32)
    mid = r.astype(BF16)
    lo = (r - mid.astype(F32)).astype(BF16)
    return hi, mid, lo


def _dot(a, b):
    return jnp.dot(a, b, preferred_element_type=F32)


def _dot_nt(a, b):
    return lax.dot_general(a, b, (((1,), (1,)), ((), ())), preferred_element_type=F32)


def _const_spec(shape):
    nd = len(shape)
    return pl.BlockSpec(shape, lambda *_: (0,) * nd)


_HBM_SPEC = pl.BlockSpec(memory_space=pl.ANY)


def _load_resident(pairs, sem):
    copies = [pltpu.make_async_copy(src, dst, sem.at[i]) for i, (src, dst) in enumerate(pairs)]
    for c in copies:
        c.start()
    for c in copies:
        c.wait()


STAGE_ROWS = 512
STAGE_COLS = 1024
PREP_ROWS = 256


def _prep_weights(chunks, stage, sem):
    def copy(i):
        src, r0, nr, c0, nc, _ = chunks[i]
        return pltpu.make_async_copy(src.at[pl.ds(r0, nr), pl.ds(c0, nc)],
                                     stage.at[i % 2, pl.ds(0, nr), pl.ds(0, nc)], sem.at[i % 2])

    copy(0).start()
    for i, (_, _, nr, _, nc, emit) in enumerate(chunks):
        if i + 1 < len(chunks):
            copy(i + 1).start()
        copy(i).wait()
        step = min(PREP_ROWS, nr)
        for r in range(0, nr, step):
            emit(slice(r, r + step), stage[i % 2, r:r + step, 0:nc])


def _plain_chunks(src, dst):
    rows, cols = src.shape

    def emit_at(r0, c0, nc):
        def emit(rs, v):
            dst[r0 + rs.start:r0 + rs.stop, c0:c0 + nc] = v.astype(BF16)
        return emit

    return [(src, r0, min(STAGE_ROWS, rows - r0), c0, min(STAGE_COLS, cols - c0),
             emit_at(r0, c0, min(STAGE_COLS, cols - c0)))
            for r0 in range(0, rows, STAGE_ROWS) for c0 in range(0, cols, STAGE_COLS)]


IN_O_DT = D_INNER + CONV_DIM
IN_O_POOL = IN_O_DT + N_HEADS
IN_O_GATE = IN_O_POOL + POOL_DIM


def _transposed_chunks(src_t, row0, n_rows, dst, dst_col0):
    def emit_at(j0):
        def emit(rs, v):
            dst[:, dst_col0 + j0 + rs.start:dst_col0 + j0 + rs.stop] = v.T.astype(BF16)
        return emit

    return [(src_t, row0 + j0, min(STAGE_ROWS, n_rows - j0), 0, D_MODEL, emit_at(j0))
            for j0 in range(0, n_rows, STAGE_ROWS)]


def _mix_weight_chunks(w_in_t, w_ref):
    def emit_dt(rs, v):
        vt = v.T
        lane = lax.broadcasted_iota(jnp.int32, vt.shape, 1)
        w_ref[:, MIX_WIDTH - DT_PAD:MIX_WIDTH] = jnp.where(lane < N_HEADS, vt, 0.0).astype(BF16)

    chunks = _transposed_chunks(w_in_t, 0, IN_O_DT, w_ref, 0)
    chunks += _transposed_chunks(w_in_t, IN_O_POOL, POOL_DIM, w_ref, IN_O_DT)
    chunks.append((w_in_t, IN_O_DT, DT_PAD, 0, D_MODEL, emit_dt))
    return chunks


def _time_of_row(r):
    return (r & (SUBLANES - 1)) * TILES + (r >> 3)


def _conv_silu(taps, cw_ref, cb_ref, cols):
    acc = cb_ref[:, cols]
    for k, t in enumerate(taps):
        acc = acc + t * cw_ref[k:k + 1, cols]
    return _silu(acc)


def _pool_mix(win_sum, u, cnt, wp_ref, ps_ref, gi):
    cols = slice(gi * POOL_GROUP_DIM, (gi + 1) * POOL_GROUP_DIM)
    d = win_sum / cnt - u
    mixed = _dot(d.astype(BF16), wp_ref[gi])
    return (mixed * ps_ref[:, cols]).astype(BF16)


def _project(x_ref, gpre_ref, w_ref, dtb_ref, to_perm, xnp_buf, xbc_buf, z_buf, u_buf, dt_buf):
    xn = _rms(x_ref[...], gpre_ref[...]).astype(BF16)
    xnp_buf[...] = _dot(to_perm, xn).astype(BF16)
    yield

    def proj(c0, width):
        tile = 256
        return jnp.concatenate([_dot(xnp_buf[...], w_ref[:, c:c + min(tile, c0 + width - c)])
                                for c in range(c0, c0 + width, tile)], axis=1)

    piece = 512
    for c0 in range(0, CONV_DIM, piece):
        xbc_buf[CONV_HALO:, c0:c0 + piece] = proj(D_INNER + c0, piece)
        yield
    o_u = D_INNER + CONV_DIM
    for c0 in range(0, POOL_DIM, piece):
        u_buf[POOL_HALO:, c0:c0 + piece] = proj(o_u + c0, piece)
        yield
    dt_buf[...] = _softplus(proj(MIX_WIDTH - DT_PAD, DT_PAD) + dtb_ref[...])
    for c0 in range(0, D_INNER, piece):
        z_buf[:, c0:c0 + piece] = proj(c0, piece)
        yield


def _run(gen):
    for _ in gen:
        pass


def _interleave(main, side, side_after):
    next(side)
    i = 0
    while next(main, StopIteration) is not StopIteration:
        for _ in range(side_after(i)):
            next(side, None)
        i += 1
    _run(side)


def _project_pieces_after(i):
    return i % 2


def _sequence(xbc_buf, z_buf, u_buf, dt_buf, xc_buf, y_buf, ctail, ptail, ht_ref,
              cw_ref, cb_ref, alog_ref, dskip_ref, wp_ref, ps_ref, sn_ref, masks, ya_ref, yb_ref, rows, tile_in_seq):
    q = CHUNK
    to_nat, causal, tri, lo_half, t_col = masks

    for n, j in enumerate(range(TILES - (CONV_WIDTH - 1), TILES)):
        src = CONV_HALO + j * SUBLANES
        xbc_buf[n * SUBLANES + 1:(n + 1) * SUBLANES, :] = xbc_buf[src:src + SUBLANES - 1, :]
        xbc_buf[n * SUBLANES:n * SUBLANES + 1, :] = ctail[n:n + 1, :]
    for c0 in range(0, CONV_DIM, 512):
        cols = slice(c0, c0 + 512)
        taps = []
        for k in range(CONV_WIDTH):
            start = CONV_HALO - (CONV_WIDTH - 1 - k) * SUBLANES
            taps.append(xbc_buf[start:start + q, cols])
        xc = _conv_silu(taps, cw_ref, cb_ref, cols)
        xc_buf[:, cols] = xc
        yield
    for n, j in enumerate(range(TILES - (CONV_WIDTH - 1), TILES)):
        src = CONV_HALO + j * SUBLANES + SUBLANES - 1
        ctail[n:n + 1, :] = xbc_buf[src:src + 1, :]

    dt = dt_buf[...]
    da = dt * (-jnp.exp(alog_ref[...]))
    acum = sum(_dot(tri, part) for part in _split3(da)) * LOG2E
    acum_last = acum[q - 1:q, :]
    e_last = jnp.exp2(acum_last)
    w_end_t = (jnp.exp2(acum_last - acum) * dt).T
    acum_t = acum.T
    dt_t = dt.T
    lo_half_row = lo_half[0:1, :]

    ssq = jnp.zeros((q, 1), F32)
    for g in range(N_GROUPS):
        b_g = xc_buf[:, D_INNER + g * D_STATE:D_INNER + (g + 1) * D_STATE]
        c_off = D_INNER + N_GROUPS * D_STATE + g * D_STATE
        c_b = xc_buf[:, c_off:c_off + D_STATE].astype(BF16)
        cb = _dot_nt(c_b, b_g.astype(BF16))
        b_t = b_g.T
        y_inter = _dot(c_b, ht_ref[g].astype(BF16))
        for pair in range(HEADS_PER_GROUP // 2):
            h0 = g * HEADS_PER_GROUP + 2 * pair
            h1 = h0 + 1
            pcols = slice(h0 * HEAD_DIM, (h0 + 2) * HEAD_DIM)
            scols = slice(pair * LANES, (pair + 1) * LANES)
            x_pair = xc_buf[:, pcols]
            ms, bws, e_cols = [], [], []
            for h in (h0, h1):
                a_col = jnp.broadcast_to(acum[:, h:h + 1], (q, q))
                seg = a_col - acum_t[h:h + 1, :]
                decay = jnp.exp2(jnp.where(causal, seg, -jnp.inf))
                ms.append((cb * decay * dt_t[h:h + 1, :]).astype(BF16))
                bws.append((b_t * w_end_t[h:h + 1, :]).astype(BF16))
                e_cols.append(jnp.exp2(a_col))
            x_top = jnp.where(lo_half, x_pair, 0.0).astype(BF16)
            x_bot = jnp.where(lo_half, 0.0, x_pair).astype(BF16)
            x_diag = jnp.concatenate([x_top, x_bot], axis=0)
            y_intra = _dot(jnp.concatenate(ms, axis=1), x_diag)
            e_pair = jnp.where(lo_half, e_cols[0], e_cols[1])
            y_pair = y_intra + y_inter[:, scols] * e_pair + dskip_ref[:, pcols] * x_pair
            gated = y_pair * _silu(z_buf[:, pcols])
            y_buf[:, pcols] = gated
            ssq = ssq + jnp.sum(gated * gated, axis=-1, keepdims=True)
            upd = _dot(jnp.concatenate(bws, axis=1), x_diag)
            e_last_pair = jnp.where(lo_half_row, e_last[:, h0:h0 + 1], e_last[:, h1:h1 + 1])
            ht_ref[g, :, scols] = ht_ref[g, :, scols] * e_last_pair + upd
            yield

    inv = lax.rsqrt(ssq * (1.0 / D_INNER) + EPS)
    for c0 in range(0, D_INNER, 1024):
        cols = slice(c0, c0 + 1024)
        ya = (y_buf[:, cols] * inv * sn_ref[:, cols]).astype(BF16)
        ya_ref[rows, cols] = _dot(to_nat, ya).astype(BF16)
        yield

    u_buf[1:q, :] = u_buf[POOL_HALO:POOL_HALO + q - 1, :]
    for j in range(1, TILES):
        u_buf[j * SUBLANES:j * SUBLANES + 1, :] = ptail[j - 1:j, :]
    pos = tile_in_seq * q + t_col
    yb_parts = []
    for gi, w in enumerate(POOL_WINDOWS):
        cols = slice(gi * POOL_GROUP_DIM, (gi + 1) * POOL_GROUP_DIM)
        u_g = u_buf[POOL_HALO:POOL_HALO + q, cols]
        s = u_buf[POOL_HALO - (w - 1) * SUBLANES:POOL_HALO + q, cols]
        shift = SUBLANES
        while shift < w * SUBLANES:
            s = s[shift:, :] + s[:-shift, :]
            shift *= 2
        cnt = jnp.minimum(pos + 1, w).astype(F32)
        yb_parts.append(_pool_mix(s, u_g, cnt, wp_ref, ps_ref, gi))
        yield
    yb_ref[rows, :] = _dot(to_nat, jnp.concatenate(yb_parts, axis=1)).astype(BF16)
    for j in range(1, TILES):
        src = POOL_HALO + j * SUBLANES + SUBLANES - 1
        ptail[j - 1:j, :] = u_buf[src:src + 1, :]


def _project_plain(x_ref, g_ref, w_ref, out_refs):
    xn = _rms(x_ref[...], g_ref[...]).astype(BF16)
    off = 0
    for o_ref in out_refs:
        width = o_ref.shape[1]
        for c in range(0, width, 1024):
            cw = min(1024, width - c)
            o_ref[:, c:c + cw] = _dot(xn, w_ref[:, off + c:off + c + cw])
        off += width


def _mix_kernel(x0_ref, x1_ref, x2_ref, xs_ref, gpre_ref, w_hbm, cw_ref, cb_ref, dtb_ref, alog_ref, dskip_ref,
                wp_hbm, ps_ref, sn_ref,
                ya_ref, yb_ref, nconv_ref, nssm_ref, npool_ref, zs_ref, xbcs_ref, us_ref, dts_ref,
                xbc0, z0, u0, dt0, xbc1, z1, u1, dt1, xc0, y0, xc1, y1, ctail, ptail, ht_ref,
                w_ref, wp_ref, wsem, xnp_buf, stage, ssem, *, tiles_per_seq):
    q = CHUNK
    k = pl.program_id(0)
    tile_in_seq = lax.rem(2 * k, tiles_per_seq)
    row = lax.broadcasted_iota(jnp.int32, (q, q), 0)
    col = lax.broadcasted_iota(jnp.int32, (q, q), 1)
    to_perm = jnp.where(col == _time_of_row(row), 1.0, 0.0).astype(BF16)
    to_nat = jnp.where(_time_of_row(col) == row, 1.0, 0.0).astype(BF16)
    causal = _time_of_row(row) >= _time_of_row(col)
    tri = jnp.where(causal, 1.0, 0.0).astype(BF16)
    t_col = _time_of_row(lax.broadcasted_iota(jnp.int32, (q, 1), 0))
    masks = (to_nat, causal, tri, col < HEAD_DIM, t_col)
    set0 = (xbc0, z0, u0, dt0)
    set1 = (xbc1, z1, u1, dt1)
    state = (ctail, ptail, ht_ref)
    consts = (cw_ref, cb_ref, alog_ref, dskip_ref, wp_ref, ps_ref, sn_ref, masks)

    @pl.when(k == 0)
    def _():
        _load_resident([(wp_hbm, wp_ref)], wsem)
        _prep_weights(_mix_weight_chunks(w_hbm, w_ref), stage, ssem)
        _run(_project(x0_ref, gpre_ref, w_ref, dtb_ref, to_perm, xnp_buf, *set0))

    @pl.when(tile_in_seq == 0)
    def _():
        ctail[...] = jnp.zeros(ctail.shape, F32)
        ptail[...] = jnp.zeros(ptail.shape, F32)
        ht_ref[...] = jnp.zeros(ht_ref.shape, F32)

    _interleave(_sequence(*set0, xc0, y0, *state, *consts, ya_ref, yb_ref, slice(0, q), tile_in_seq),
                _project(x1_ref, gpre_ref, w_ref, dtb_ref, to_perm, xnp_buf, *set1), _project_pieces_after)
    _interleave(_sequence(*set1, xc1, y1, *state, *consts, ya_ref, yb_ref, slice(q, 2 * q), tile_in_seq + 1),
                _project(x2_ref, gpre_ref, w_ref, dtb_ref, to_perm, xnp_buf, *set0), _project_pieces_after)

    @pl.when(tile_in_seq == tiles_per_seq - 2)
    def _():
        nconv_ref[0] = ctail[0:CONV_WIDTH - 1, :]
        npool_ref[0] = ptail[0:POOL_BUF, :]
        for g in range(N_GROUPS):
            nssm_ref[0, g * HEADS_PER_GROUP:(g + 1) * HEADS_PER_GROUP] = (
                ht_ref[g].T.reshape(HEADS_PER_GROUP, HEAD_DIM, D_STATE))

    @pl.when(k == pl.num_programs(0) - 1)
    def _():
        _project_plain(xs_ref, gpre_ref, w_ref, (zs_ref, xbcs_ref, us_ref, dts_ref))


def _mix_prompt(x, xs, gpre, w, cw, cb, dtb, alog, dskip_e, wp, ps, sn, bsz, seq):
    q = CHUNK
    nt = seq // q
    n_tiles = bsz * nt
    assert seq % (2 * q) == 0
    steps_per_seq = nt // 2
    m = bsz * seq
    consts = (gpre, w, cw, cb, dtb, alog, dskip_e, wp, ps, sn)
    nb = xs.shape[0]
    sample_widths = (D_INNER, CONV_DIM, POOL_DIM, DT_PAD)
    pair = lambda k: (k, 0)
    per_b = lambda k: (k // steps_per_seq, 0, 0)
    proj_set = [pltpu.VMEM((CONV_HALO + q, CONV_DIM), F32), pltpu.VMEM((q, D_INNER), F32),
                pltpu.VMEM((POOL_HALO + q, POOL_DIM), F32), pltpu.VMEM((q, DT_PAD), F32)]
    seq_set = [pltpu.VMEM((q, CONV_DIM), F32), pltpu.VMEM((q, D_INNER), F32)]
    return pl.pallas_call(
        functools.partial(_mix_kernel, tiles_per_seq=nt),
        grid=(n_tiles // 2,),
        in_specs=[pl.BlockSpec((q, D_MODEL), lambda k: (0, 0)),
                  pl.BlockSpec((q, D_MODEL), lambda k: (2 * k + 1, 0)),
                  pl.BlockSpec((q, D_MODEL), lambda k: (jnp.minimum(2 * k + 2, n_tiles - 1), 0)),
                  _const_spec(xs.shape)]
                 + [_HBM_SPEC if a is w or a is wp else _const_spec(a.shape) for a in consts],
        out_specs=[pl.BlockSpec((2 * q, D_INNER), pair),
                   pl.BlockSpec((2 * q, POOL_DIM), pair),
                   pl.BlockSpec((1, CONV_WIDTH - 1, CONV_DIM), per_b),
                   pl.BlockSpec((1, N_HEADS, HEAD_DIM, D_STATE), lambda k: (k // steps_per_seq, 0, 0, 0)),
                   pl.BlockSpec((1, POOL_BUF, POOL_DIM), per_b)]
                  + [_const_spec((nb, wd)) for wd in sample_widths],
        out_shape=[jax.ShapeDtypeStruct((m, D_INNER), BF16),
                   jax.ShapeDtypeStruct((m, POOL_DIM), BF16),
                   jax.ShapeDtypeStruct((bsz, CONV_WIDTH - 1, CONV_DIM), F32),
                   jax.ShapeDtypeStruct((bsz, N_HEADS, HEAD_DIM, D_STATE), F32),
                   jax.ShapeDtypeStruct((bsz, POOL_BUF, POOL_DIM), F32)]
                  + [jax.ShapeDtypeStruct((nb, wd), F32) for wd in sample_widths],
        scratch_shapes=proj_set + proj_set + seq_set + seq_set + [
            pltpu.VMEM((SUBLANES, CONV_DIM), F32),
            pltpu.VMEM((2 * SUBLANES, POOL_DIM), F32),
            pltpu.VMEM((N_GROUPS, D_STATE, GROUP_COLS), F32),
            pltpu.VMEM((D_MODEL, MIX_WIDTH), BF16), pltpu.VMEM(wp.shape, BF16), pltpu.SemaphoreType.DMA((1,)),
            pltpu.VMEM((q, D_MODEL), BF16),
            pltpu.VMEM((2, STAGE_ROWS, STAGE_COLS), F32), pltpu.SemaphoreType.DMA((2,))],
        compiler_params=pltpu.CompilerParams(dimension_semantics=("arbitrary",),
                                             vmem_limit_bytes=VMEM_LIMIT),
        name="mix_prompt",
    )(x, x, x, xs, *consts)


def _sample_tok_kernel(xbc_ref, dt_ref, u_ref, sconv_ref, spool_ref, cw_ref, cb_ref, dtb_ref, alog_ref,
                       wp_ref, ps_ref,
                       xc_ref, dec_ref, dtx_ref, nconv_ref, npool_ref, yb_ref):
    for c0 in range(0, CONV_DIM, 512):
        cols = slice(c0, c0 + 512)
        taps = [sconv_ref[k, :, cols] for k in range(CONV_WIDTH - 1)]
        taps.append(xbc_ref[:, cols])
        xc_ref[:, cols] = _conv_silu(taps, cw_ref, cb_ref, cols)
    for k in range(CONV_WIDTH - 2):
        nconv_ref[k] = sconv_ref[k + 1]
    nconv_ref[CONV_WIDTH - 2] = xbc_ref[...]

    dt = _softplus(dt_ref[...] + dtb_ref[...])
    dec_ref[...] = jnp.exp(dt * (-jnp.exp(alog_ref[...])))
    hrow = lax.broadcasted_iota(jnp.int32, (DT_PAD, D_INNER), 0)
    ccol = lax.broadcasted_iota(jnp.int32, (DT_PAD, D_INNER), 1)
    expand = jnp.where((ccol >= hrow * HEAD_DIM) & (ccol < (hrow + 1) * HEAD_DIM), 1.0, 0.0).astype(BF16)
    dt_e = sum(_dot(part, expand) for part in _split3(dt))
    dtx_ref[...] = dt_e * xc_ref[:, 0:D_INNER]

    cnt_base = PAST_LEN + 1
    for gi, w in enumerate(POOL_WINDOWS):
        cols = slice(gi * POOL_GROUP_DIM, (gi + 1) * POOL_GROUP_DIM)
        u_g = u_ref[:, cols]
        s = u_g
        for k in range(1, w):
            s = s + spool_ref[POOL_BUF - k, :, cols]
        yb_ref[:, cols] = _pool_mix(s, u_g, float(min(cnt_base, w)), wp_ref, ps_ref, gi)
    for k in range(POOL_BUF - 1):
        npool_ref[k] = spool_ref[k + 1]
    npool_ref[POOL_BUF - 1] = u_ref[...]


def _sample_tok(xbc, dt, u, sconv, spool, cw, cb, dtb, alog, wp, ps):
    nb = xbc.shape[0]
    full = lambda a: pl.BlockSpec(a.shape, lambda i: (0,) * a.ndim)
    args = (xbc, dt, u, sconv, spool, cw, cb, dtb, alog, wp, ps)
    out_shape = [jax.ShapeDtypeStruct((nb, CONV_DIM), F32),
                 jax.ShapeDtypeStruct((nb, DT_PAD), F32),
                 jax.ShapeDtypeStruct((nb, D_INNER), F32),
                 jax.ShapeDtypeStruct(sconv.shape, F32),
                 jax.ShapeDtypeStruct(spool.shape, F32),
                 jax.ShapeDtypeStruct((nb, POOL_DIM), BF16)]
    return pl.pallas_call(
        _sample_tok_kernel,
        grid=(1,),
        in_specs=[full(a) for a in args],
        out_specs=[pl.BlockSpec(s.shape, lambda i, nd=len(s.shape): (0,) * nd) for s in out_shape],
        out_shape=out_shape,
        compiler_params=pltpu.CompilerParams(dimension_semantics=("arbitrary",),
                                             vmem_limit_bytes=VMEM_LIMIT),
        name="sample_tok",
    )(*args)


def _sample_ssm_kernel(st_ref, b_ref, c_ref, dec_ref, dtx_ref, xc_ref, z_ref, dskip_ref, sn_ref, ya_ref, nst_ref):
    grow = lax.broadcasted_iota(jnp.int32, (N_GROUPS, D_INNER), 0)
    gcol = lax.broadcasted_iota(jnp.int32, (N_GROUPS, D_INNER), 1)
    gmask = (gcol >= grow * GROUP_COLS) & (gcol < (grow + 1) * GROUP_COLS)
    for t in range(st_ref.shape[0]):
        tok = slice(t, t + 1)
        dtx = jnp.where(gmask, dtx_ref[tok, :], 0.0)
        x_hi, x_mid, _ = _split3(dtx)
        b_hi, b_mid, _ = _split3(b_ref[t])
        lhs = jnp.concatenate([x_hi.astype(F32), x_mid.astype(F32), x_hi.astype(F32)], axis=0)
        rhs = jnp.concatenate([b_hi.astype(F32), b_hi.astype(F32), b_mid.astype(F32)], axis=0)
        upd = lax.dot_general(lhs, rhs, (((0,), (0,)), ((), ())), preferred_element_type=F32)
        dec = dec_ref[tok, :]
        c_b = c_ref[t].astype(BF16)
        y_parts = []
        for g in range(N_GROUPS):
            new_g = []
            for r in range(HEADS_PER_GROUP):
                h = g * HEADS_PER_GROUP + r
                scale = jnp.broadcast_to(dec[:, h:h + 1], (HEAD_DIM, D_STATE))
                new = st_ref[t, h] * scale + upd[h * HEAD_DIM:(h + 1) * HEAD_DIM, :]
                nst_ref[t, h] = new
                new_g.append(new.astype(BF16))
            y_g = _dot_nt(c_b, jnp.concatenate(new_g, axis=0))
            gcols = slice(g * GROUP_COLS, (g + 1) * GROUP_COLS)
            y_parts.append(y_g[g:g + 1, :] + dskip_ref[:, gcols] * xc_ref[tok, gcols])
        y = jnp.concatenate(y_parts, axis=1)
        ya_ref[tok, :] = _rms(y * _silu(z_ref[tok, :]), sn_ref[...])


def _sample_ssm(state, b3, c3, dec, dtx, xc, z, dskip_e, sn):
    nb = state.shape[0]
    tb = _token_tile(nb, SUBLANES)
    tok2 = lambda i: (i, 0)
    tok3 = lambda i: (i, 0, 0)
    state_spec = pl.BlockSpec((tb, N_HEADS, HEAD_DIM, D_STATE), lambda i: (i, 0, 0, 0))
    return pl.pallas_call(
        _sample_ssm_kernel,
        grid=(nb // tb,),
        in_specs=[state_spec,
                  pl.BlockSpec((tb, N_GROUPS, D_STATE), tok3),
                  pl.BlockSpec((tb, N_GROUPS, D_STATE), tok3),
                  pl.BlockSpec((tb, DT_PAD), tok2),
                  pl.BlockSpec((tb, D_INNER), tok2),
                  pl.BlockSpec((tb, D_INNER), tok2),
                  pl.BlockSpec((tb, D_INNER), tok2),
                  _const_spec(dskip_e.shape), _const_spec(sn.shape)],
        out_specs=[pl.BlockSpec((tb, D_INNER), tok2), state_spec],
        out_shape=[jax.ShapeDtypeStruct((nb, D_INNER), F32),
                   jax.ShapeDtypeStruct(state.shape, F32)],
        compiler_params=pltpu.CompilerParams(dimension_semantics=("arbitrary",),
                                             vmem_limit_bytes=VMEM_LIMIT),
        name="sample_ssm",
    )(state, b3, c3, dec, dtx, xc, z, dskip_e, sn)


def _out_math(x, ya_in, yb_in, gpre_ref, wg_ref, wa_ref, wb_ref, wo_ref, gpost_ref, fpre_ref, w1_ref, w2_ref,
              fpost_ref):
    xn = _rms(x, gpre_ref[...]).astype(BF16)
    ya = _dot(ya_in.astype(BF16), wa_ref[...])
    yb = _dot(yb_in, wb_ref[...])
    merged = (jax.nn.sigmoid(_dot(xn, wg_ref[:, 0:D_MODEL])) * ya
              + jax.nn.sigmoid(_dot(xn, wg_ref[:, D_MODEL:])) * yb)
    mo = _dot(merged.astype(BF16), wo_ref[...])
    h = x + _rms(mo, gpost_ref[...])
    hn = _rms(h, fpre_ref[...]).astype(BF16)
    gate = _dot(hn, w1_ref[:, 0:D_FF])
    up = _dot(hn, w1_ref[:, D_FF:])
    f = _dot((_silu(gate) * up).astype(BF16), w2_ref[...])
    return h + _rms(f, fpost_ref[...])


def _out_kernel(x_ref, ya_ref, yb_ref, xs_ref, yas_ref, ybs_ref, gpre_ref, wint_hbm, wa_hbm, wb_hbm, wo_hbm, gpost_ref,
                fpre_ref, w1_hbm, w2_hbm, fpost_ref, o_ref, os_ref,
                wg_ref, wa_ref, wb_ref, wo_ref, w1_ref, w2_ref, stage, ssem):
    i = pl.program_id(0)
    n_prompt = pl.num_programs(0) - 1
    params = (gpre_ref, wg_ref, wa_ref, wb_ref, wo_ref, gpost_ref, fpre_ref, w1_ref, w2_ref, fpost_ref)

    @pl.when(i == 0)
    def _():
        chunks = _transposed_chunks(wint_hbm, IN_O_GATE, 2 * D_MODEL, wg_ref, 0)
        for src, dst in ((wa_hbm, wa_ref), (wb_hbm, wb_ref), (wo_hbm, wo_ref), (w1_hbm, w1_ref), (w2_hbm, w2_ref)):
            chunks += _plain_chunks(src, dst)
        _prep_weights(chunks, stage, ssem)

    @pl.when(i < n_prompt)
    def _():
        o_ref[...] = _out_math(x_ref[...], ya_ref[...], yb_ref[...], *params)

    @pl.when(i == n_prompt)
    def _():
        os_ref[...] = _out_math(xs_ref[...], yas_ref[...], ybs_ref[...], *params)


def _out(x, ya, yb, xs, yas, ybs, consts, tm):
    m = x.shape[0]
    n = m // tm
    nb = xs.shape[0]
    tok = lambda i: (jnp.minimum(i, n - 1), 0)
    big = [a.shape[0] >= D_MODEL and a.shape[1] >= D_MODEL for a in consts]
    resident = [(D_MODEL, 2 * D_MODEL)] + [a.shape for a, b in zip(consts, big) if b][1:]
    once = lambda a: pl.BlockSpec(a.shape, lambda i: (0, 0), pipeline_mode=pl.Buffered(1))
    return pl.pallas_call(
        _out_kernel,
        grid=(n + 1,),
        in_specs=[pl.BlockSpec((tm, D_MODEL), tok), pl.BlockSpec((tm, D_INNER), tok),
                  pl.BlockSpec((tm, POOL_DIM), tok), once(xs), once(yas), once(ybs)]
                 + [_HBM_SPEC if b else _const_spec(a.shape) for a, b in zip(consts, big)],
        out_specs=[pl.BlockSpec((tm, D_MODEL), tok), pl.BlockSpec((nb, D_MODEL), lambda i: (0, 0))],
        out_shape=[jax.ShapeDtypeStruct((m, D_MODEL), F32), jax.ShapeDtypeStruct((nb, D_MODEL), F32)],
        scratch_shapes=[pltpu.VMEM(shape, BF16) for shape in resident]
                       + [pltpu.VMEM((2, STAGE_ROWS, STAGE_COLS), F32), pltpu.SemaphoreType.DMA((2,))],
        compiler_params=pltpu.CompilerParams(dimension_semantics=("arbitrary",),
                                             vmem_limit_bytes=VMEM_LIMIT_OUT),
        name="out",
    )(x, ya, yb, xs, yas, ybs, *consts)


def _token_tile(m, cap):
    t = min(m, cap)
    assert m % t == 0
    return t


def kernel(x_prompt, x_sample, state_conv, state_ssm, state_pool, norm_mix_pre, norm_mix_post, norm_ffn_pre,
           norm_ffn_post, w_in, conv_w, conv_b, dt_bias, a_log, d_skip, ssm_norm, w_pool_group, pool_scale,
           w_branch_a, w_branch_b, w_out, w_ffn_in, w_ffn_out):
    bsz, seq, _ = x_prompt.shape
    nb, dec_seq, _ = x_sample.shape
    assert dec_seq == 1 and norm_mix_pre.shape[0] == 1
    l = 0
    wit = jnp.swapaxes(w_in[l], 0, 1)
    pad_h = lambda v: jnp.pad(v.astype(F32), (0, DT_PAD - N_HEADS)).reshape(1, DT_PAD)
    row = lambda v: v.astype(F32).reshape(1, -1)
    dtb, alog = pad_h(dt_bias[l]), pad_h(a_log[l])
    dskip_e = jnp.repeat(d_skip[l].astype(F32), HEAD_DIM).reshape(1, D_INNER)
    cw, cb = conv_w[l].astype(F32), row(conv_b[l])
    wp, ps = w_pool_group[l].astype(BF16), row(pool_scale[l])
    g_pre, sn = row(norm_mix_pre[l]), row(ssm_norm[l])
    out_consts = (g_pre, wit, w_branch_a[l], w_branch_b[l], w_out[l],
                  row(norm_mix_post[l]), row(norm_ffn_pre[l]), w_ffn_in[l], w_ffn_out[l],
                  row(norm_ffn_post[l]))

    xp = x_prompt.reshape(bsz * seq, D_MODEL)
    xs2 = x_sample.reshape(nb, D_MODEL)
    ya, yb, nconv_p, nssm_p, npool_p, z, xbc, u, dt = _mix_prompt(xp, xs2, g_pre, wit, cw, cb, dtb, alog, dskip_e, wp,
                                                                  ps, sn, bsz, seq)
    sconv = jnp.swapaxes(state_conv[l], 0, 1)
    spool = jnp.swapaxes(state_pool[l], 0, 1)
    xc, dec, dtx, nconv_s, npool_s, yb_s = _sample_tok(xbc, dt, u, sconv, spool, cw, cb, dtb, alog, wp, ps)
    b3 = xc[:, D_INNER:D_INNER + N_GROUPS * D_STATE].reshape(nb, N_GROUPS, D_STATE)
    c3 = xc[:, D_INNER + N_GROUPS * D_STATE:].reshape(nb, N_GROUPS, D_STATE)
    ya_s, nssm_s = _sample_ssm(state_ssm[l], b3, c3, dec, dtx, xc, z, dskip_e, sn)
    out_p, out_s = _out(xp, ya, yb, xs2, ya_s, yb_s, out_consts, _token_tile(bsz * seq, 256))
    out_p = out_p.reshape(bsz, seq, D_MODEL)
    out_s = out_s.reshape(nb, 1, D_MODEL)

    return (out_p, out_s,
            nconv_p[None], nssm_p[None], npool_p[None],
            jnp.swapaxes(nconv_s, 0, 1)[None], nssm_s[None],
            jnp.swapaxes(npool_s, 0, 1)[None])
```

```python
import functools

import jax
import jax.numpy as jnp
from jax import lax
from jax.experimental import pallas as pl
from jax.experimental.pallas import tpu as pltpu

D_MODEL = 1024
D_INNER = 2048
HEAD_DIM = 64
N_HEADS = 32
N_GROUPS = 8
HEADS_PER_GROUP = 4
D_STATE = 128
CONV_WIDTH = 4
CONV_DIM = 4096
CHUNK = 128
POOL_DIM = 1024
POOL_WINDOWS = (2, 4, 8, 16)
POOL_GROUP_DIM = 256
POOL_BUF = 15
D_FF = 2816
EPS = 1e-6
PAST_LEN = 16384
LOG2E = 1.4426950408889634

LANES = 128
SUBLANES = 8
TILES = CHUNK // SUBLANES
GROUP_COLS = HEADS_PER_GROUP * HEAD_DIM
DT_PAD = LANES
MIX_WIDTH = D_INNER + CONV_DIM + POOL_DIM + DT_PAD
CONV_HALO = (CONV_WIDTH - 1) * SUBLANES
POOL_HALO = CHUNK
VMEM_LIMIT = 56 * 1024 * 1024
TILES_PER_STEP = 4
VMEM_LIMIT_OUT = 62 * 1024 * 1024

F32 = jnp.float32
BF16 = jnp.bfloat16


def _rms(x, g):
    y = x * lax.rsqrt(jnp.mean(x * x, axis=-1, keepdims=True) + EPS)
    return y * g


def _silu(x):
    return x * jax.nn.sigmoid(x)


def _softplus(x):
    return jnp.maximum(x, 0.0) + jnp.log(1.0 + jnp.exp(-jnp.abs(x)))


def _split3(x):
    hi = x.astype(BF16)
    r = x - hi.astype(F32)
    mid = r.astype(BF16)
    lo = (r - mid.astype(F32)).astype(BF16)
    return hi, mid, lo


def _dot(a, b):
    return jnp.dot(a, b, preferred_element_type=F32)


def _dot_nt(a, b):
    return lax.dot_general(a, b, (((1,), (1,)), ((), ())), preferred_element_type=F32)


def _const_spec(shape):
    nd = len(shape)
    return pl.BlockSpec(shape, lambda *_: (0,) * nd)


_HBM_SPEC = pl.BlockSpec(memory_space=pl.ANY)


def _load_resident(pairs, sem):
    copies = [pltpu.make_async_copy(src, dst, sem.at[i]) for i, (src, dst) in enumerate(pairs)]
    for c in copies:
        c.start()
    for c in copies:
        c.wait()


STAGE_ROWS = 512
STAGE_COLS = 1024
PREP_ROWS = 256


def _prep_weights(chunks, stage, sem):
    def copy(i):
        src, r0, nr, c0, nc, _ = chunks[i]
        return pltpu.make_async_copy(src.at[pl.ds(r0, nr), pl.ds(c0, nc)],
                                     stage.at[i % 2, pl.ds(0, nr), pl.ds(0, nc)], sem.at[i % 2])

    copy(0).start()
    for i, (_, _, nr, _, nc, emit) in enumerate(chunks):
        if i + 1 < len(chunks):
            copy(i + 1).start()
        copy(i).wait()
        step = min(PREP_ROWS, nr)
        for r in range(0, nr, step):
            emit(slice(r, r + step), stage[i % 2, r:r + step, 0:nc])


def _plain_chunks(src, dst):
    rows, cols = src.shape

    def emit_at(r0, c0, nc):
        def emit(rs, v):
            dst[r0 + rs.start:r0 + rs.stop, c0:c0 + nc] = v.astype(BF16)
        return emit

    return [(src, r0, min(STAGE_ROWS, rows - r0), c0, min(STAGE_COLS, cols - c0),
             emit_at(r0, c0, min(STAGE_COLS, cols - c0)))
            for r0 in range(0, rows, STAGE_ROWS) for c0 in range(0, cols, STAGE_COLS)]


IN_O_DT = D_INNER + CONV_DIM
IN_O_POOL = IN_O_DT + N_HEADS
IN_O_GATE = IN_O_POOL + POOL_DIM


def _transposed_chunks(src_t, row0, n_rows, dst, dst_col0):
    def emit_at(j0):
        def emit(rs, v):
            dst[:, dst_col0 + j0 + rs.start:dst_col0 + j0 + rs.stop] = v.T.astype(BF16)
        return emit

    return [(src_t, row0 + j0, min(STAGE_ROWS, n_rows - j0), 0, D_MODEL, emit_at(j0))
            for j0 in range(0, n_rows, STAGE_ROWS)]


def _mix_weight_chunks(w_in_t, w_ref):
    def emit_dt(rs, v):
        vt = v.T
        lane = lax.broadcasted_iota(jnp.int32, vt.shape, 1)
        w_ref[:, MIX_WIDTH - DT_PAD:MIX_WIDTH] = jnp.where(lane < N_HEADS, vt, 0.0).astype(BF16)

    chunks = _transposed_chunks(w_in_t, 0, IN_O_DT, w_ref, 0)
    chunks += _transposed_chunks(w_in_t, IN_O_POOL, POOL_DIM, w_ref, IN_O_DT)
    chunks.append((w_in_t, IN_O_DT, DT_PAD, 0, D_MODEL, emit_dt))
    return chunks


def _time_of_row(r):
    return (r & (SUBLANES - 1)) * TILES + (r >> 3)


def _conv_silu(taps, cw_ref, cb_ref, cols):
    acc = cb_ref[:, cols]
    for k, t in enumerate(taps):
        acc = acc + t * cw_ref[k:k + 1, cols]
    return _silu(acc)


def _pool_mix(win_sum, u, cnt, wp_ref, ps_ref, gi):
    cols = slice(gi * POOL_GROUP_DIM, (gi + 1) * POOL_GROUP_DIM)
    d = win_sum / cnt - u
    mixed = _dot(d.astype(BF16), wp_ref[gi])
    return (mixed * ps_ref[:, cols]).astype(BF16)


def _project(x_ref, gpre_ref, w_ref, dtb_ref, to_perm, xnp_buf, xbc_buf, z_buf, u_buf, dt_buf):
    xn = _rms(x_ref[...], gpre_ref[...]).astype(BF16)
    xnp_buf[...] = _dot(to_perm, xn).astype(BF16)
    yield

    def proj(c0, width):
        tile = 256
        return jnp.concatenate([_dot(xnp_buf[...], w_ref[:, c:c + min(tile, c0 + width - c)])
                                for c in range(c0, c0 + width, tile)], axis=1)

    piece = 512
    for c0 in range(0, CONV_DIM, piece):
        xbc_buf[CONV_HALO:, c0:c0 + piece] = proj(D_INNER + c0, piece)
        yield
    o_u = D_INNER + CONV_DIM
    for c0 in range(0, POOL_DIM, piece):
        u_buf[POOL_HALO:, c0:c0 + piece] = proj(o_u + c0, piece)
        yield
    dt_buf[...] = _softplus(proj(MIX_WIDTH - DT_PAD, DT_PAD) + dtb_ref[...])
    for c0 in range(0, D_INNER, piece):
        z_buf[:, c0:c0 + piece] = proj(c0, piece)
        yield


def _run(gen):
    for _ in gen:
        pass


def _interleave(main, side, side_after):
    next(side)
    i = 0
    while next(main, StopIteration) is not StopIteration:
        for _ in range(side_after(i)):
            next(side, None)
        i += 1
    _run(side)


def _project_pieces_after(i):
    return i % 2


def _sequence(xbc_buf, z_buf, u_buf, dt_buf, xc_buf, y_buf, ctail, ptail, ht_ref,
              cw_ref, cb_ref, alog_ref, dskip_ref, wp_ref, ps_ref, sn_ref, masks, ya_ref, yb_ref, rows, tile_in_seq):
    q = CHUNK
    to_nat, causal, tri, lo_half, t_col = masks

    for n, j in enumerate(range(TILES - (CONV_WIDTH - 1), TILES)):
        src = CONV_HALO + j * SUBLANES
        xbc_buf[n * SUBLANES + 1:(n + 1) * SUBLANES, :] = xbc_buf[src:src + SUBLANES - 1, :]
        xbc_buf[n * SUBLANES:n * SUBLANES + 1, :] = ctail[n:n + 1, :]
    for c0 in range(0, CONV_DIM, 512):
        cols = slice(c0, c0 + 512)
        taps = []
        for k in range(CONV_WIDTH):
            start = CONV_HALO - (CONV_WIDTH - 1 - k) * SUBLANES
            taps.append(xbc_buf[start:start + q, cols])
        xc = _conv_silu(taps, cw_ref, cb_ref, cols)
        xc_buf[:, cols] = xc
        yield
    for n, j in enumerate(range(TILES - (CONV_WIDTH - 1), TILES)):
        src = CONV_HALO + j * SUBLANES + SUBLANES - 1
        ctail[n:n + 1, :] = xbc_buf[src:src + 1, :]

    dt = dt_buf[...]
    da = dt * (-jnp.exp(alog_ref[...]))
    acum = sum(_dot(tri, part) for part in _split3(da)) * LOG2E
    acum_last = acum[q - 1:q, :]
    e_last = jnp.exp2(acum_last)
    w_end_t = (jnp.exp2(acum_last - acum) * dt).T
    acum_t = acum.T
    dt_t = dt.T
    lo_half_row = lo_half[0:1, :]

    ssq = jnp.zeros((q, 1), F32)
    for g in range(N_GROUPS):
        b_g = xc_buf[:, D_INNER + g * D_STATE:D_INNER + (g + 1) * D_STATE]
        c_off = D_INNER + N_GROUPS * D_STATE + g * D_STATE
        c_b = xc_buf[:, c_off:c_off + D_STATE].astype(BF16)
        cb = _dot_nt(c_b, b_g.astype(BF16))
        b_t = b_g.T
        y_inter = _dot(c_b, ht_ref[g].astype(BF16))
        for pair in range(HEADS_PER_GROUP // 2):
            h0 = g * HEADS_PER_GROUP + 2 * pair
            h1 = h0 + 1
            pcols = slice(h0 * HEAD_DIM, (h0 + 2) * HEAD_DIM)
            scols = slice(pair * LANES, (pair + 1) * LANES)
            x_pair = xc_buf[:, pcols]
            ms, bws, e_cols = [], [], []
            for h in (h0, h1):
                a_col = jnp.broadcast_to(acum[:, h:h + 1], (q, q))
                seg = a_col - acum_t[h:h + 1, :]
                decay = jnp.exp2(jnp.where(causal, seg, -jnp.inf))
                ms.append((cb * decay * dt_t[h:h + 1, :]).astype(BF16))
                bws.append((b_t * w_end_t[h:h + 1, :]).astype(BF16))
                e_cols.append(jnp.exp2(a_col))
            x_top = jnp.where(lo_half, x_pair, 0.0).astype(BF16)
            x_bot = jnp.where(lo_half, 0.0, x_pair).astype(BF16)
            x_diag = jnp.concatenate([x_top, x_bot], axis=0)
            y_intra = _dot(jnp.concatenate(ms, axis=1), x_diag)
            e_pair = jnp.where(lo_half, e_cols[0], e_cols[1])
            y_pair = y_intra + y_inter[:, scols] * e_pair + dskip_ref[:, pcols] * x_pair
            gated = y_pair * _silu(z_buf[:, pcols])
            y_buf[:, pcols] = gated
            ssq = ssq + jnp.sum(gated * gated, axis=-1, keepdims=True)
            upd = _dot(jnp.concatenate(bws, axis=1), x_diag)
            e_last_pair = jnp.where(lo_half_row, e_last[:, h0:h0 + 1], e_last[:, h1:h1 + 1])
            ht_ref[g, :, scols] = ht_ref[g, :, scols] * e_last_pair + upd
            yield

    inv = lax.rsqrt(ssq * (1.0 / D_INNER) + EPS)
    for c0 in range(0, D_INNER, 1024):
        cols = slice(c0, c0 + 1024)
        ya = (y_buf[:, cols] * inv * sn_ref[:, cols]).astype(BF16)
        ya_ref[rows, cols] = _dot(to_nat, ya).astype(BF16)
        yield

    u_buf[1:q, :] = u_buf[POOL_HALO:POOL_HALO + q - 1, :]
    for j in range(1, TILES):
        u_buf[j * SUBLANES:j * SUBLANES + 1, :] = ptail[j - 1:j, :]
    pos = tile_in_seq * q + t_col
    yb_parts = []
    for gi, w in enumerate(POOL_WINDOWS):
        cols = slice(gi * POOL_GROUP_DIM, (gi + 1) * POOL_GROUP_DIM)
        u_g = u_buf[POOL_HALO:POOL_HALO + q, cols]
        s = u_buf[POOL_HALO - (w - 1) * SUBLANES:POOL_HALO + q, cols]
        shift = SUBLANES
        while shift < w * SUBLANES:
            s = s[shift:, :] + s[:-shift, :]
            shift *= 2
        cnt = jnp.minimum(pos + 1, w).astype(F32)
        yb_parts.append(_pool_mix(s, u_g, cnt, wp_ref, ps_ref, gi))
        yield
    yb_ref[rows, :] = _dot(to_nat, jnp.concatenate(yb_parts, axis=1)).astype(BF16)
    for j in range(1, TILES):
        src = POOL_HALO + j * SUBLANES + SUBLANES - 1
        ptail[j - 1:j, :] = u_buf[src:src + 1, :]


def _project_plain(x_ref, g_ref, w_ref, out_refs):
    xn = _rms(x_ref[...], g_ref[...]).astype(BF16)
    off = 0
    for o_ref in out_refs:
        width = o_ref.shape[1]
        for c in range(0, width, 1024):
            cw = min(1024, width - c)
            o_ref[:, c:c + cw] = _dot(xn, w_ref[:, off + c:off + c + cw])
        off += width


def _mix_kernel(x0_ref, x1_ref, x2_ref, x3_ref, x4_ref, xs_ref, gpre_ref, w_hbm, cw_ref, cb_ref, dtb_ref, alog_ref,
                dskip_ref, wp_hbm, ps_ref, sn_ref,
                ya_ref, yb_ref, nconv_ref, nssm_ref, npool_ref, zs_ref, xbcs_ref, us_ref, dts_ref,
                xbc0, z0, u0, dt0, xbc1, z1, u1, dt1, xc0, y0, xc1, y1, ctail, ptail, ht_ref,
                w_ref, wp_ref, wsem, xnp_buf, stage, ssem, *, tiles_per_seq):
    q = CHUNK
    k = pl.program_id(0)
    tile_in_seq = lax.rem(TILES_PER_STEP * k, tiles_per_seq)
    row = lax.broadcasted_iota(jnp.int32, (q, q), 0)
    col = lax.broadcasted_iota(jnp.int32, (q, q), 1)
    to_perm = jnp.where(col == _time_of_row(row), 1.0, 0.0).astype(BF16)
    to_nat = jnp.where(_time_of_row(col) == row, 1.0, 0.0).astype(BF16)
    causal = _time_of_row(row) >= _time_of_row(col)
    tri = jnp.where(causal, 1.0, 0.0).astype(BF16)
    t_col = _time_of_row(lax.broadcasted_iota(jnp.int32, (q, 1), 0))
    masks = (to_nat, causal, tri, col < HEAD_DIM, t_col)
    set0 = (xbc0, z0, u0, dt0)
    set1 = (xbc1, z1, u1, dt1)
    state = (ctail, ptail, ht_ref)
    consts = (cw_ref, cb_ref, alog_ref, dskip_ref, wp_ref, ps_ref, sn_ref, masks)

    @pl.when(k == 0)
    def _():
        _load_resident([(wp_hbm, wp_ref)], wsem)
        _prep_weights(_mix_weight_chunks(w_hbm, w_ref), stage, ssem)
        _run(_project(x0_ref, gpre_ref, w_ref, dtb_ref, to_perm, xnp_buf, *set0))

    @pl.when(tile_in_seq == 0)
    def _():
        ctail[...] = jnp.zeros(ctail.shape, F32)
        ptail[...] = jnp.zeros(ptail.shape, F32)
        ht_ref[...] = jnp.zeros(ht_ref.shape, F32)

    sets = (set0, set1)
    seq_scratch = ((xc0, y0), (xc1, y1))
    for p, x_next in enumerate((x1_ref, x2_ref, x3_ref, x4_ref)):
        _interleave(_sequence(*sets[p % 2], *seq_scratch[p % 2], *state, *consts, ya_ref, yb_ref,
                              slice(p * q, (p + 1) * q), tile_in_seq + p),
                    _project(x_next, gpre_ref, w_ref, dtb_ref, to_perm, xnp_buf, *sets[(p + 1) % 2]),
                    _project_pieces_after)

    @pl.when(tile_in_seq == tiles_per_seq - TILES_PER_STEP)
    def _():
        nconv_ref[0] = ctail[0:CONV_WIDTH - 1, :]
        npool_ref[0] = ptail[0:POOL_BUF, :]
        for g in range(N_GROUPS):
            nssm_ref[0, g * HEADS_PER_GROUP:(g + 1) * HEADS_PER_GROUP] = (
                ht_ref[g].T.reshape(HEADS_PER_GROUP, HEAD_DIM, D_STATE))

    @pl.when(k == pl.num_programs(0) - 1)
    def _():
        _project_plain(xs_ref, gpre_ref, w_ref, (zs_ref, xbcs_ref, us_ref, dts_ref))


def _mix_prompt(x, xs, gpre, w, cw, cb, dtb, alog, dskip_e, wp, ps, sn, bsz, seq):
    q = CHUNK
    nt = seq // q
    n_tiles = bsz * nt
    tps = TILES_PER_STEP
    assert nt % tps == 0
    steps_per_seq = nt // tps
    m = bsz * seq
    consts = (gpre, w, cw, cb, dtb, alog, dskip_e, wp, ps, sn)
    nb = xs.shape[0]
    sample_widths = (D_INNER, CONV_DIM, POOL_DIM, DT_PAD)
    pair = lambda k: (k, 0)
    per_b = lambda k: (k // steps_per_seq, 0, 0)
    proj_set = [pltpu.VMEM((CONV_HALO + q, CONV_DIM), F32), pltpu.VMEM((q, D_INNER), F32),
                pltpu.VMEM((POOL_HALO + q, POOL_DIM), F32), pltpu.VMEM((q, DT_PAD), F32)]
    seq_set = [pltpu.VMEM((q, CONV_DIM), F32), pltpu.VMEM((q, D_INNER), F32)]
    return pl.pallas_call(
        functools.partial(_mix_kernel, tiles_per_seq=nt),
        grid=(n_tiles // tps,),
        in_specs=[pl.BlockSpec((q, D_MODEL), lambda k: (0, 0)),
                  *[pl.BlockSpec((q, D_MODEL), lambda k, p=p: (jnp.minimum(tps * k + p, n_tiles - 1), 0))
                    for p in range(1, tps + 1)],
                  _const_spec(xs.shape)]
                 + [_HBM_SPEC if a is w or a is wp else _const_spec(a.shape) for a in consts],
        out_specs=[pl.BlockSpec((tps * q, D_INNER), pair),
                   pl.BlockSpec((tps * q, POOL_DIM), pair),
                   pl.BlockSpec((1, CONV_WIDTH - 1, CONV_DIM), per_b),
                   pl.BlockSpec((1, N_HEADS, HEAD_DIM, D_STATE), lambda k: (k // steps_per_seq, 0, 0, 0)),
                   pl.BlockSpec((1, POOL_BUF, POOL_DIM), per_b)]
                  + [_const_spec((nb, wd)) for wd in sample_widths],
        out_shape=[jax.ShapeDtypeStruct((m, D_INNER), BF16),
                   jax.ShapeDtypeStruct((m, POOL_DIM), BF16),
                   jax.ShapeDtypeStruct((bsz, CONV_WIDTH - 1, CONV_DIM), F32),
                   jax.ShapeDtypeStruct((bsz, N_HEADS, HEAD_DIM, D_STATE), F32),
                   jax.ShapeDtypeStruct((bsz, POOL_BUF, POOL_DIM), F32)]
                  + [jax.ShapeDtypeStruct((nb, wd), F32) for wd in sample_widths],
        scratch_shapes=proj_set + proj_set + seq_set + seq_set + [
            pltpu.VMEM((SUBLANES, CONV_DIM), F32),
            pltpu.VMEM((2 * SUBLANES, POOL_DIM), F32),
            pltpu.VMEM((N_GROUPS, D_STATE, GROUP_COLS), F32),
            pltpu.VMEM((D_MODEL, MIX_WIDTH), BF16), pltpu.VMEM(wp.shape, BF16), pltpu.SemaphoreType.DMA((1,)),
            pltpu.VMEM((q, D_MODEL), BF16),
            pltpu.VMEM((2, STAGE_ROWS, STAGE_COLS), F32), pltpu.SemaphoreType.DMA((2,))],
        compiler_params=pltpu.CompilerParams(dimension_semantics=("arbitrary",),
                                             vmem_limit_bytes=VMEM_LIMIT),
        name="mix_prompt",
    )(*([x] * (tps + 1)), xs, *consts)


def _sample_tok_kernel(xbc_ref, dt_ref, u_ref, sconv_ref, spool_ref, cw_ref, cb_ref, dtb_ref, alog_ref,
                       wp_ref, ps_ref,
                       xc_ref, dec_ref, dtx_ref, nconv_ref, npool_ref, yb_ref):
    for c0 in range(0, CONV_DIM, 512):
        cols = slice(c0, c0 + 512)
        taps = [sconv_ref[k, :, cols] for k in range(CONV_WIDTH - 1)]
        taps.append(xbc_ref[:, cols])
        xc_ref[:, cols] = _conv_silu(taps, cw_ref, cb_ref, cols)
    for k in range(CONV_WIDTH - 2):
        nconv_ref[k] = sconv_ref[k + 1]
    nconv_ref[CONV_WIDTH - 2] = xbc_ref[...]

    dt = _softplus(dt_ref[...] + dtb_ref[...])
    dec_ref[...] = jnp.exp(dt * (-jnp.exp(alog_ref[...])))
    hrow = lax.broadcasted_iota(jnp.int32, (DT_PAD, D_INNER), 0)
    ccol = lax.broadcasted_iota(jnp.int32, (DT_PAD, D_INNER), 1)
    expand = jnp.where((ccol >= hrow * HEAD_DIM) & (ccol < (hrow + 1) * HEAD_DIM), 1.0, 0.0).astype(BF16)
    dt_e = sum(_dot(part, expand) for part in _split3(dt))
    dtx_ref[...] = dt_e * xc_ref[:, 0:D_INNER]

    cnt_base = PAST_LEN + 1
    for gi, w in enumerate(POOL_WINDOWS):
        cols = slice(gi * POOL_GROUP_DIM, (gi + 1) * POOL_GROUP_DIM)
        u_g = u_ref[:, cols]
        s = u_g
        for k in range(1, w):
            s = s + spool_ref[POOL_BUF - k, :, cols]
        yb_ref[:, cols] = _pool_mix(s, u_g, float(min(cnt_base, w)), wp_ref, ps_ref, gi)
    for k in range(POOL_BUF - 1):
        npool_ref[k] = spool_ref[k + 1]
    npool_ref[POOL_BUF - 1] = u_ref[...]


def _sample_tok(xbc, dt, u, sconv, spool, cw, cb, dtb, alog, wp, ps):
    nb = xbc.shape[0]
    tb = _token_tile(nb, 32)

    def spec(shape, per_token):
        if not per_token:
            return _const_spec(shape)
        if len(shape) == 2:
            return pl.BlockSpec((tb, shape[1]), lambda i: (i, 0))
        return pl.BlockSpec((shape[0], tb, shape[2]), lambda i: (0, i, 0))

    args = (xbc, dt, u, sconv, spool, cw, cb, dtb, alog, wp, ps)
    out_shape = [jax.ShapeDtypeStruct((nb, CONV_DIM), F32),
                 jax.ShapeDtypeStruct((nb, DT_PAD), F32),
                 jax.ShapeDtypeStruct((nb, D_INNER), F32),
                 jax.ShapeDtypeStruct(sconv.shape, F32),
                 jax.ShapeDtypeStruct(spool.shape, F32),
                 jax.ShapeDtypeStruct((nb, POOL_DIM), BF16)]
    return pl.pallas_call(
        _sample_tok_kernel,
        grid=(nb // tb,),
        in_specs=[spec(a.shape, n < 5) for n, a in enumerate(args)],
        out_specs=[spec(s.shape, True) for s in out_shape],
        out_shape=out_shape,
        compiler_params=pltpu.CompilerParams(dimension_semantics=("arbitrary",),
                                             vmem_limit_bytes=VMEM_LIMIT),
        name="sample_tok",
    )(*args)


def _sample_ssm_kernel(st_ref, b_ref, c_ref, dec_ref, dtx_ref, xc_ref, z_ref, dskip_ref, sn_ref, ya_ref, nst_ref):
    grow = lax.broadcasted_iota(jnp.int32, (N_GROUPS, D_INNER), 0)
    gcol = lax.broadcasted_iota(jnp.int32, (N_GROUPS, D_INNER), 1)
    gmask = (gcol >= grow * GROUP_COLS) & (gcol < (grow + 1) * GROUP_COLS)
    for t in range(st_ref.shape[0]):
        tok = slice(t, t + 1)
        dtx = jnp.where(gmask, dtx_ref[tok, :], 0.0)
        x_hi, x_mid, _ = _split3(dtx)
        b_hi, b_mid, _ = _split3(b_ref[t])
        lhs = jnp.concatenate([x_hi.astype(F32), x_mid.astype(F32), x_hi.astype(F32)], axis=0)
        rhs = jnp.concatenate([b_hi.astype(F32), b_hi.astype(F32), b_mid.astype(F32)], axis=0)
        upd = lax.dot_general(lhs, rhs, (((0,), (0,)), ((), ())), preferred_element_type=F32)
        dec = dec_ref[tok, :]
        c_b = c_ref[t].astype(BF16)
        y_parts = []
        for g in range(N_GROUPS):
            new_g = []
            for r in range(HEADS_PER_GROUP):
                h = g * HEADS_PER_GROUP + r
                scale = jnp.broadcast_to(dec[:, h:h + 1], (HEAD_DIM, D_STATE))
                new = st_ref[t, h] * scale + upd[h * HEAD_DIM:(h + 1) * HEAD_DIM, :]
                nst_ref[t, h] = new
                new_g.append(new.astype(BF16))
            y_g = _dot_nt(c_b, jnp.concatenate(new_g, axis=0))
            gcols = slice(g * GROUP_COLS, (g + 1) * GROUP_COLS)
            y_parts.append(y_g[g:g + 1, :] + dskip_ref[:, gcols] * xc_ref[tok, gcols])
        y = jnp.concatenate(y_parts, axis=1)
        ya_ref[tok, :] = _rms(y * _silu(z_ref[tok, :]), sn_ref[...])


def _sample_ssm(state, b3, c3, dec, dtx, xc, z, dskip_e, sn):
    nb = state.shape[0]
    tb = _token_tile(nb, SUBLANES)
    tok2 = lambda i: (i, 0)
    tok3 = lambda i: (i, 0, 0)
    state_spec = pl.BlockSpec((tb, N_HEADS, HEAD_DIM, D_STATE), lambda i: (i, 0, 0, 0))
    return pl.pallas_call(
        _sample_ssm_kernel,
        grid=(nb // tb,),
        in_specs=[state_spec,
                  pl.BlockSpec((tb, N_GROUPS, D_STATE), tok3),
                  pl.BlockSpec((tb, N_GROUPS, D_STATE), tok3),
                  pl.BlockSpec((tb, DT_PAD), tok2),
                  pl.BlockSpec((tb, D_INNER), tok2),
                  pl.BlockSpec((tb, D_INNER), tok2),
                  pl.BlockSpec((tb, D_INNER), tok2),
                  _const_spec(dskip_e.shape), _const_spec(sn.shape)],
        out_specs=[pl.BlockSpec((tb, D_INNER), tok2), state_spec],
        out_shape=[jax.ShapeDtypeStruct((nb, D_INNER), F32),
                   jax.ShapeDtypeStruct(state.shape, F32)],
        compiler_params=pltpu.CompilerParams(dimension_semantics=("arbitrary",),
                                             vmem_limit_bytes=VMEM_LIMIT),
        name="sample_ssm",
    )(state, b3, c3, dec, dtx, xc, z, dskip_e, sn)


def _out_math(x, ya_in, yb_in, gpre_ref, wg_ref, wa_ref, wb_ref, wo_ref, gpost_ref, fpre_ref, w1_ref, w2_ref,
              fpost_ref):
    xn = _rms(x, gpre_ref[...]).astype(BF16)
    ya = _dot(ya_in.astype(BF16), wa_ref[...])
    yb = _dot(yb_in, wb_ref[...])
    merged = (jax.nn.sigmoid(_dot(xn, wg_ref[:, 0:D_MODEL])) * ya
              + jax.nn.sigmoid(_dot(xn, wg_ref[:, D_MODEL:])) * yb)
    mo = _dot(merged.astype(BF16), wo_ref[...])
    h = x + _rms(mo, gpost_ref[...])
    hn = _rms(h, fpre_ref[...]).astype(BF16)
    gate = _dot(hn, w1_ref[:, 0:D_FF])
    up = _dot(hn, w1_ref[:, D_FF:])
    f = _dot((_silu(gate) * up).astype(BF16), w2_ref[...])
    return h + _rms(f, fpost_ref[...])


def _out_kernel(x_ref, ya_ref, yb_ref, xs_ref, yas_ref, ybs_ref, gpre_ref, wint_hbm, wa_hbm, wb_hbm, wo_hbm, gpost_ref,
                fpre_ref, w1_hbm, w2_hbm, fpost_ref, o_ref, os_ref,
                wg_ref, wa_ref, wb_ref, wo_ref, w1_ref, w2_ref, stage, ssem):
    i = pl.program_id(0)
    n_prompt = pl.num_programs(0) - 1
    params = (gpre_ref, wg_ref, wa_ref, wb_ref, wo_ref, gpost_ref, fpre_ref, w1_ref, w2_ref, fpost_ref)

    @pl.when(i == 0)
    def _():
        chunks = _transposed_chunks(wint_hbm, IN_O_GATE, 2 * D_MODEL, wg_ref, 0)
        for src, dst in ((wa_hbm, wa_ref), (wb_hbm, wb_ref), (wo_hbm, wo_ref), (w1_hbm, w1_ref), (w2_hbm, w2_ref)):
            chunks += _plain_chunks(src, dst)
        _prep_weights(chunks, stage, ssem)

    @pl.when(i < n_prompt)
    def _():
        o_ref[...] = _out_math(x_ref[...], ya_ref[...], yb_ref[...], *params)

    @pl.when(i == n_prompt)
    def _():
        os_ref[...] = _out_math(xs_ref[...], yas_ref[...], ybs_ref[...], *params)


def _out(x, ya, yb, xs, yas, ybs, consts, tm):
    m = x.shape[0]
    n = m // tm
    nb = xs.shape[0]
    tok = lambda i: (jnp.minimum(i, n - 1), 0)
    big = [a.shape[0] >= D_MODEL and a.shape[1] >= D_MODEL for a in consts]
    resident = [(D_MODEL, 2 * D_MODEL)] + [a.shape for a, b in zip(consts, big) if b][1:]
    once = lambda a: pl.BlockSpec(a.shape, lambda i: (0, 0), pipeline_mode=pl.Buffered(1))
    return pl.pallas_call(
        _out_kernel,
        grid=(n + 1,),
        in_specs=[pl.BlockSpec((tm, D_MODEL), tok), pl.BlockSpec((tm, D_INNER), tok),
                  pl.BlockSpec((tm, POOL_DIM), tok), once(xs), once(yas), once(ybs)]
                 + [_HBM_SPEC if b else _const_spec(a.shape) for a, b in zip(consts, big)],
        out_specs=[pl.BlockSpec((tm, D_MODEL), tok), pl.BlockSpec((nb, D_MODEL), lambda i: (0, 0))],
        out_shape=[jax.ShapeDtypeStruct((m, D_MODEL), F32), jax.ShapeDtypeStruct((nb, D_MODEL), F32)],
        scratch_shapes=[pltpu.VMEM(shape, BF16) for shape in resident]
                       + [pltpu.VMEM((2, STAGE_ROWS, STAGE_COLS), F32), pltpu.SemaphoreType.DMA((2,))],
        compiler_params=pltpu.CompilerParams(dimension_semantics=("arbitrary",),
                                             vmem_limit_bytes=VMEM_LIMIT_OUT),
        name="out",
    )(x, ya, yb, xs, yas, ybs, *consts)


def _token_tile(m, cap):
    t = min(m, cap)
    assert m % t == 0
    return t


def kernel(x_prompt, x_sample, state_conv, state_ssm, state_pool, norm_mix_pre, norm_mix_post, norm_ffn_pre,
           norm_ffn_post, w_in, conv_w, conv_b, dt_bias, a_log, d_skip, ssm_norm, w_pool_group, pool_scale,
           w_branch_a, w_branch_b, w_out, w_ffn_in, w_ffn_out):
    bsz, seq, _ = x_prompt.shape
    nb, dec_seq, _ = x_sample.shape
    assert dec_seq == 1 and norm_mix_pre.shape[0] == 1
    l = 0
    wit = jnp.swapaxes(w_in[l], 0, 1)
    pad_h = lambda v: jnp.pad(v.astype(F32), (0, DT_PAD - N_HEADS)).reshape(1, DT_PAD)
    row = lambda v: v.astype(F32).reshape(1, -1)
    dtb, alog = pad_h(dt_bias[l]), pad_h(a_log[l])
    dskip_e = jnp.repeat(d_skip[l].astype(F32), HEAD_DIM).reshape(1, D_INNER)
    cw, cb = conv_w[l].astype(F32), row(conv_b[l])
    wp, ps = w_pool_group[l].astype(BF16), row(pool_scale[l])
    g_pre, sn = row(norm_mix_pre[l]), row(ssm_norm[l])
    out_consts = (g_pre, wit, w_branch_a[l], w_branch_b[l], w_out[l],
                  row(norm_mix_post[l]), row(norm_ffn_pre[l]), w_ffn_in[l], w_ffn_out[l],
                  row(norm_ffn_post[l]))

    xp = x_prompt.reshape(bsz * seq, D_MODEL)
    xs2 = x_sample.reshape(nb, D_MODEL)
    ya, yb, nconv_p, nssm_p, npool_p, z, xbc, u, dt = _mix_prompt(xp, xs2, g_pre, wit, cw, cb, dtb, alog, dskip_e, wp,
                                                                  ps, sn, bsz, seq)
    sconv = jnp.swapaxes(state_conv[l], 0, 1)
    spool = jnp.swapaxes(state_pool[l], 0, 1)
    xc, dec, dtx, nconv_s, npool_s, yb_s = _sample_tok(xbc, dt, u, sconv, spool, cw, cb, dtb, alog, wp, ps)
    b3 = xc[:, D_INNER:D_INNER + N_GROUPS * D_STATE].reshape(nb, N_GROUPS, D_STATE)
    c3 = xc[:, D_INNER + N_GROUPS * D_STATE:].reshape(nb, N_GROUPS, D_STATE)
    ya_s, nssm_s = _sample_ssm(state_ssm[l], b3, c3, dec, dtx, xc, z, dskip_e, sn)
    out_p, out_s = _out(xp, ya, yb, xs2, ya_s, yb_s, out_consts, _token_tile(bsz * seq, 256))
    out_p = out_p.reshape(bsz, seq, D_MODEL)
    out_s = out_s.reshape(nb, 1, D_MODEL)

    return (out_p, out_s,
            nconv_p[None], nssm_p[None], npool_p[None],
            jnp.swapaxes(nconv_s, 0, 1)[None], nssm_s[None],
            jnp.swapaxes(npool_s, 0, 1)[None])
```

```python
import functools

import jax
import jax.numpy as jnp
from jax import lax
from jax.experimental import pallas as pl
from jax.experimental.pallas import tpu as pltpu

D_MODEL = 1024
D_INNER = 2048
HEAD_DIM = 64
N_HEADS = 32
N_GROUPS = 8
HEADS_PER_GROUP = 4
D_STATE = 128
CONV_WIDTH = 4
CONV_DIM = 4096
CHUNK = 128
POOL_DIM = 1024
POOL_WINDOWS = (2, 4, 8, 16)
POOL_GROUP_DIM = 256
POOL_BUF = 15
D_FF = 2816
EPS = 1e-6
PAST_LEN = 16384
LOG2E = 1.4426950408889634

LANES = 128
SUBLANES = 8
TILES = CHUNK // SUBLANES
GROUP_COLS = HEADS_PER_GROUP * HEAD_DIM
DT_PAD = LANES
MIX_WIDTH = D_INNER + CONV_DIM + POOL_DIM + DT_PAD
CONV_HALO = (CONV_WIDTH - 1) * SUBLANES
POOL_HALO = CHUNK
VMEM_LIMIT = 56 * 1024 * 1024
TILES_PER_STEP = 4
VMEM_LIMIT_OUT = 62 * 1024 * 1024

F32 = jnp.float32
BF16 = jnp.bfloat16


def _rms(x, g):
    y = x * lax.rsqrt(jnp.mean(x * x, axis=-1, keepdims=True) + EPS)
    return y * g


def _silu(x):
    return x * jax.nn.sigmoid(x)


def _softplus(x):
    return jnp.maximum(x, 0.0) + jnp.log(1.0 + jnp.exp(-jnp.abs(x)))


def _split3(x):
    hi = x.astype(BF16)
    r = x - hi.astype(F32)
    mid = r.astype(BF16)
    lo = (r - mid.astype(F32)).astype(BF16)
    return hi, mid, lo


def _dot(a, b):
    return jnp.dot(a, b, preferred_element_type=F32)


def _dot_nt(a, b):
    return lax.dot_general(a, b, (((1,), (1,)), ((), ())), preferred_element_type=F32)


def _const_spec(shape):
    nd = len(shape)
    return pl.BlockSpec(shape, lambda *_: (0,) * nd)


_HBM_SPEC = pl.BlockSpec(memory_space=pl.ANY)


def _load_resident(pairs, sem):
    copies = [pltpu.make_async_copy(src, dst, sem.at[i]) for i, (src, dst) in enumerate(pairs)]
    for c in copies:
        c.start()
    for c in copies:
        c.wait()


STAGE_ROWS = 512
STAGE_COLS = 1024
PREP_ROWS = 256


def _prep_weights(chunks, stage, sem):
    def copy(i):
        src, r0, nr, c0, nc, _ = chunks[i]
        return pltpu.make_async_copy(src.at[pl.ds(r0, nr), pl.ds(c0, nc)],
                                     stage.at[i % 2, pl.ds(0, nr), pl.ds(0, nc)], sem.at[i % 2])

    copy(0).start()
    for i, (_, _, nr, _, nc, emit) in enumerate(chunks):
        if i + 1 < len(chunks):
            copy(i + 1).start()
        copy(i).wait()
        step = min(PREP_ROWS, nr)
        for r in range(0, nr, step):
            emit(slice(r, r + step), stage[i % 2, r:r + step, 0:nc])


def _plain_chunks(src, dst):
    rows, cols = src.shape

    def emit_at(r0, c0, nc):
        def emit(rs, v):
            dst[r0 + rs.start:r0 + rs.stop, c0:c0 + nc] = v.astype(BF16)
        return emit

    return [(src, r0, min(STAGE_ROWS, rows - r0), c0, min(STAGE_COLS, cols - c0),
             emit_at(r0, c0, min(STAGE_COLS, cols - c0)))
            for r0 in range(0, rows, STAGE_ROWS) for c0 in range(0, cols, STAGE_COLS)]


IN_O_DT = D_INNER + CONV_DIM
IN_O_POOL = IN_O_DT + N_HEADS
IN_O_GATE = IN_O_POOL + POOL_DIM


def _transposed_chunks(src_t, row0, n_rows, dst, dst_col0):
    def emit_at(j0):
        def emit(rs, v):
            dst[:, dst_col0 + j0 + rs.start:dst_col0 + j0 + rs.stop] = v.T.astype(BF16)
        return emit

    return [(src_t, row0 + j0, min(STAGE_ROWS, n_rows - j0), 0, D_MODEL, emit_at(j0))
            for j0 in range(0, n_rows, STAGE_ROWS)]


def _mix_weight_chunks(w_in_t, w_ref):
    def emit_dt(rs, v):
        vt = v.T
        lane = lax.broadcasted_iota(jnp.int32, vt.shape, 1)
        w_ref[:, MIX_WIDTH - DT_PAD:MIX_WIDTH] = jnp.where(lane < N_HEADS, vt, 0.0).astype(BF16)

    chunks = _transposed_chunks(w_in_t, 0, IN_O_DT, w_ref, 0)
    chunks += _transposed_chunks(w_in_t, IN_O_POOL, POOL_DIM, w_ref, IN_O_DT)
    chunks.append((w_in_t, IN_O_DT, DT_PAD, 0, D_MODEL, emit_dt))
    return chunks


def _time_of_row(r):
    return (r & (SUBLANES - 1)) * TILES + (r >> 3)


def _conv_silu(taps, cw_ref, cb_ref, cols):
    acc = cb_ref[:, cols]
    for k, t in enumerate(taps):
        acc = acc + t * cw_ref[k:k + 1, cols]
    return _silu(acc)


def _pool_mix(win_sum, u, cnt, wp_ref, ps_ref, gi):
    cols = slice(gi * POOL_GROUP_DIM, (gi + 1) * POOL_GROUP_DIM)
    d = win_sum / cnt - u
    mixed = _dot(d.astype(BF16), wp_ref[gi])
    return (mixed * ps_ref[:, cols]).astype(BF16)


def _project(x_ref, gpre_ref, w_ref, dtb_ref, to_perm, xnp_buf, xbc_buf, z_buf, u_buf, dt_buf):
    xn = _rms(x_ref[...], gpre_ref[...]).astype(BF16)
    xnp_buf[...] = _dot(to_perm, xn).astype(BF16)
    yield

    def proj(c0, width):
        tile = 256
        return jnp.concatenate([_dot(xnp_buf[...], w_ref[:, c:c + min(tile, c0 + width - c)])
                                for c in range(c0, c0 + width, tile)], axis=1)

    piece = 512
    for c0 in range(0, CONV_DIM, piece):
        xbc_buf[CONV_HALO:, c0:c0 + piece] = proj(D_INNER + c0, piece)
        yield
    o_u = D_INNER + CONV_DIM
    for c0 in range(0, POOL_DIM, piece):
        u_buf[POOL_HALO:, c0:c0 + piece] = proj(o_u + c0, piece)
        yield
    dt_buf[...] = _softplus(proj(MIX_WIDTH - DT_PAD, DT_PAD) + dtb_ref[...])
    for c0 in range(0, D_INNER, piece):
        z_buf[:, c0:c0 + piece] = proj(c0, piece)
        yield


def _run(gen):
    for _ in gen:
        pass


def _interleave(main, side, side_after):
    next(side)
    i = 0
    while next(main, StopIteration) is not StopIteration:
        for _ in range(side_after(i)):
            next(side, None)
        i += 1
    _run(side)


def _project_pieces_after(i):
    return i % 2


def _sequence(xbc_buf, z_buf, u_buf, dt_buf, xc_buf, y_buf, head_bufs, ctail, ptail, ht_ref,
              cw_ref, cb_ref, alog_ref, dskip_ref, wp_ref, ps_ref, sn_ref, masks, ya_ref, yb_ref, rows, tile_in_seq):
    q = CHUNK
    to_nat, causal, tri, lo_half, t_col = masks

    for n, j in enumerate(range(TILES - (CONV_WIDTH - 1), TILES)):
        src = CONV_HALO + j * SUBLANES
        xbc_buf[n * SUBLANES + 1:(n + 1) * SUBLANES, :] = xbc_buf[src:src + SUBLANES - 1, :]
        xbc_buf[n * SUBLANES:n * SUBLANES + 1, :] = ctail[n:n + 1, :]
    for c0 in range(0, CONV_DIM, 512):
        cols = slice(c0, c0 + 512)
        taps = []
        for k in range(CONV_WIDTH):
            start = CONV_HALO - (CONV_WIDTH - 1 - k) * SUBLANES
            taps.append(xbc_buf[start:start + q, cols])
        xc = _conv_silu(taps, cw_ref, cb_ref, cols)
        xc_buf[:, cols] = xc
        yield
    for n, j in enumerate(range(TILES - (CONV_WIDTH - 1), TILES)):
        src = CONV_HALO + j * SUBLANES + SUBLANES - 1
        ctail[n:n + 1, :] = xbc_buf[src:src + 1, :]

    dt = dt_buf[...]
    da = dt * (-jnp.exp(alog_ref[...]))
    acum = sum(_dot(tri, part) for part in _split3(da)) * LOG2E
    acum_last = acum[q - 1:q, :]
    e_last = jnp.exp2(acum_last)
    acum_ref, acum_t, dt_t, w_end_t = head_bufs
    acum_ref[...] = acum
    w_end_t[...] = (jnp.exp2(acum_last - acum) * dt).T
    acum_t[...] = acum.T
    dt_t[...] = dt.T
    lo_half_row = lo_half[0:1, :]

    ssq = jnp.zeros((q, 1), F32)
    for g in range(N_GROUPS):
        b_g = xc_buf[:, D_INNER + g * D_STATE:D_INNER + (g + 1) * D_STATE]
        c_off = D_INNER + N_GROUPS * D_STATE + g * D_STATE
        c_b = xc_buf[:, c_off:c_off + D_STATE].astype(BF16)
        cb = _dot_nt(c_b, b_g.astype(BF16))
        b_t = b_g.T
        y_inter = _dot(c_b, ht_ref[g].astype(BF16))
        for pair in range(HEADS_PER_GROUP // 2):
            h0 = g * HEADS_PER_GROUP + 2 * pair
            h1 = h0 + 1
            pcols = slice(h0 * HEAD_DIM, (h0 + 2) * HEAD_DIM)
            scols = slice(pair * LANES, (pair + 1) * LANES)
            x_pair = xc_buf[:, pcols]
            ms, bws, e_cols = [], [], []
            for h in (h0, h1):
                a_col = jnp.broadcast_to(acum_ref[:, h:h + 1], (q, q))
                seg = a_col - acum_t[h:h + 1, :]
                decay = jnp.exp2(jnp.where(causal, seg, -jnp.inf))
                ms.append((cb * decay * dt_t[h:h + 1, :]).astype(BF16))
                bws.append((b_t * w_end_t[h:h + 1, :]).astype(BF16))
                e_cols.append(jnp.exp2(a_col))
            x_top = jnp.where(lo_half, x_pair, 0.0).astype(BF16)
            x_bot = jnp.where(lo_half, 0.0, x_pair).astype(BF16)
            x_diag = jnp.concatenate([x_top, x_bot], axis=0)
            y_intra = _dot(jnp.concatenate(ms, axis=1), x_diag)
            e_pair = jnp.where(lo_half, e_cols[0], e_cols[1])
            y_pair = y_intra + y_inter[:, scols] * e_pair + dskip_ref[:, pcols] * x_pair
            gated = y_pair * _silu(z_buf[:, pcols])
            y_buf[:, pcols] = gated
            ssq = ssq + jnp.sum(gated * gated, axis=-1, keepdims=True)
            upd = _dot(jnp.concatenate(bws, axis=1), x_diag)
            e_last_pair = jnp.where(lo_half_row, e_last[:, h0:h0 + 1], e_last[:, h1:h1 + 1])
            ht_ref[g, :, scols] = ht_ref[g, :, scols] * e_last_pair + upd
            yield

    inv = lax.rsqrt(ssq * (1.0 / D_INNER) + EPS)
    for c0 in range(0, D_INNER, 1024):
        cols = slice(c0, c0 + 1024)
        ya = (y_buf[:, cols] * inv * sn_ref[:, cols]).astype(BF16)
        ya_ref[rows, cols] = _dot(to_nat, ya).astype(BF16)
        yield

    u_buf[1:q, :] = u_buf[POOL_HALO:POOL_HALO + q - 1, :]
    for j in range(1, TILES):
        u_buf[j * SUBLANES:j * SUBLANES + 1, :] = ptail[j - 1:j, :]
    pos = tile_in_seq * q + t_col
    yb_parts = []
    for gi, w in enumerate(POOL_WINDOWS):
        cols = slice(gi * POOL_GROUP_DIM, (gi + 1) * POOL_GROUP_DIM)
        u_g = u_buf[POOL_HALO:POOL_HALO + q, cols]
        s = u_buf[POOL_HALO - (w - 1) * SUBLANES:POOL_HALO + q, cols]
        shift = SUBLANES
        while shift < w * SUBLANES:
            s = s[shift:, :] + s[:-shift, :]
            shift *= 2
        cnt = jnp.minimum(pos + 1, w).astype(F32)
        yb_parts.append(_pool_mix(s, u_g, cnt, wp_ref, ps_ref, gi))
        yield
    yb_ref[rows, :] = _dot(to_nat, jnp.concatenate(yb_parts, axis=1)).astype(BF16)
    for j in range(1, TILES):
        src = POOL_HALO + j * SUBLANES + SUBLANES - 1
        ptail[j - 1:j, :] = u_buf[src:src + 1, :]


def _project_plain(x_ref, g_ref, w_ref, out_refs):
    xn = _rms(x_ref[...], g_ref[...]).astype(BF16)
    off = 0
    for o_ref in out_refs:
        width = o_ref.shape[1]
        for c in range(0, width, 1024):
            cw = min(1024, width - c)
            o_ref[:, c:c + cw] = _dot(xn, w_ref[:, off + c:off + c + cw])
        off += width


def _mix_kernel(x0_ref, x1_ref, x2_ref, x3_ref, x4_ref, xs_ref, gpre_ref, w_hbm, cw_ref, cb_ref, dtb_ref, alog_ref, dskip_ref,
                wp_hbm, ps_ref, sn_ref,
                ya_ref, yb_ref, nconv_ref, nssm_ref, npool_ref, zs_ref, xbcs_ref, us_ref, dts_ref,
                xbc0, z0, u0, dt0, xbc1, z1, u1, dt1, xc0, y0, xc1, y1, ctail, ptail, ht_ref,
                w_ref, wp_ref, wsem, xnp_buf, stage, ssem, hb0, hb1, hb2, hb3, hb4, hb5, hb6, hb7, *, tiles_per_seq):
    q = CHUNK
    k = pl.program_id(0)
    tile_in_seq = lax.rem(TILES_PER_STEP * k, tiles_per_seq)
    row = lax.broadcasted_iota(jnp.int32, (q, q), 0)
    col = lax.broadcasted_iota(jnp.int32, (q, q), 1)
    to_perm = jnp.where(col == _time_of_row(row), 1.0, 0.0).astype(BF16)
    to_nat = jnp.where(_time_of_row(col) == row, 1.0, 0.0).astype(BF16)
    causal = _time_of_row(row) >= _time_of_row(col)
    tri = jnp.where(causal, 1.0, 0.0).astype(BF16)
    t_col = _time_of_row(lax.broadcasted_iota(jnp.int32, (q, 1), 0))
    masks = (to_nat, causal, tri, col < HEAD_DIM, t_col)
    set0 = (xbc0, z0, u0, dt0)
    set1 = (xbc1, z1, u1, dt1)
    state = (ctail, ptail, ht_ref)
    consts = (cw_ref, cb_ref, alog_ref, dskip_ref, wp_ref, ps_ref, sn_ref, masks)

    @pl.when(k == 0)
    def _():
        _load_resident([(wp_hbm, wp_ref)], wsem)
        _prep_weights(_mix_weight_chunks(w_hbm, w_ref), stage, ssem)
        _run(_project(x0_ref, gpre_ref, w_ref, dtb_ref, to_perm, xnp_buf, *set0))

    @pl.when(tile_in_seq == 0)
    def _():
        ctail[...] = jnp.zeros(ctail.shape, F32)
        ptail[...] = jnp.zeros(ptail.shape, F32)
        ht_ref[...] = jnp.zeros(ht_ref.shape, F32)

    sets = (set0, set1)
    head_bufs = ((hb0, hb1, hb2, hb3), (hb4, hb5, hb6, hb7))
    seq_scratch = ((xc0, y0), (xc1, y1))
    for p, x_next in enumerate((x1_ref, x2_ref, x3_ref, x4_ref)):
        _interleave(_sequence(*sets[p % 2], *seq_scratch[p % 2], head_bufs[p % 2], *state, *consts, ya_ref, yb_ref,
                              slice(p * q, (p + 1) * q), tile_in_seq + p),
                    _project(x_next, gpre_ref, w_ref, dtb_ref, to_perm, xnp_buf, *sets[(p + 1) % 2]),
                    _project_pieces_after)

    @pl.when(tile_in_seq == tiles_per_seq - TILES_PER_STEP)
    def _():
        nconv_ref[0] = ctail[0:CONV_WIDTH - 1, :]
        npool_ref[0] = ptail[0:POOL_BUF, :]
        for g in range(N_GROUPS):
            nssm_ref[0, g * HEADS_PER_GROUP:(g + 1) * HEADS_PER_GROUP] = (
                ht_ref[g].T.reshape(HEADS_PER_GROUP, HEAD_DIM, D_STATE))

    @pl.when(k == pl.num_programs(0) - 1)
    def _():
        _project_plain(xs_ref, gpre_ref, w_ref, (zs_ref, xbcs_ref, us_ref, dts_ref))


def _mix_prompt(x, xs, gpre, w, cw, cb, dtb, alog, dskip_e, wp, ps, sn, bsz, seq):
    q = CHUNK
    nt = seq // q
    n_tiles = bsz * nt
    tps = TILES_PER_STEP
    assert nt % tps == 0
    steps_per_seq = nt // tps
    m = bsz * seq
    consts = (gpre, w, cw, cb, dtb, alog, dskip_e, wp, ps, sn)
    nb = xs.shape[0]
    sample_widths = (D_INNER, CONV_DIM, POOL_DIM, DT_PAD)
    pair = lambda k: (k, 0)
    per_b = lambda k: (k // steps_per_seq, 0, 0)
    proj_set = [pltpu.VMEM((CONV_HALO + q, CONV_DIM), F32), pltpu.VMEM((q, D_INNER), F32),
                pltpu.VMEM((POOL_HALO + q, POOL_DIM), F32), pltpu.VMEM((q, DT_PAD), F32)]
    seq_set = [pltpu.VMEM((q, CONV_DIM), F32), pltpu.VMEM((q, D_INNER), F32)]
    return pl.pallas_call(
        functools.partial(_mix_kernel, tiles_per_seq=nt),
        grid=(n_tiles // tps,),
        in_specs=[pl.BlockSpec((q, D_MODEL), lambda k: (0, 0)),
                  *[pl.BlockSpec((q, D_MODEL), lambda k, p=p: (jnp.minimum(tps * k + p, n_tiles - 1), 0))
                    for p in range(1, tps + 1)],
                  _const_spec(xs.shape)]
                 + [_HBM_SPEC if a is w or a is wp else _const_spec(a.shape) for a in consts],
        out_specs=[pl.BlockSpec((tps * q, D_INNER), pair),
                   pl.BlockSpec((tps * q, POOL_DIM), pair),
                   pl.BlockSpec((1, CONV_WIDTH - 1, CONV_DIM), per_b),
                   pl.BlockSpec((1, N_HEADS, HEAD_DIM, D_STATE), lambda k: (k // steps_per_seq, 0, 0, 0)),
                   pl.BlockSpec((1, POOL_BUF, POOL_DIM), per_b)]
                  + [_const_spec((nb, wd)) for wd in sample_widths],
        out_shape=[jax.ShapeDtypeStruct((m, D_INNER), BF16),
                   jax.ShapeDtypeStruct((m, POOL_DIM), BF16),
                   jax.ShapeDtypeStruct((bsz, CONV_WIDTH - 1, CONV_DIM), F32),
                   jax.ShapeDtypeStruct((bsz, N_HEADS, HEAD_DIM, D_STATE), F32),
                   jax.ShapeDtypeStruct((bsz, POOL_BUF, POOL_DIM), F32)]
                  + [jax.ShapeDtypeStruct((nb, wd), F32) for wd in sample_widths],
        scratch_shapes=proj_set + proj_set + seq_set + seq_set + [
            pltpu.VMEM((SUBLANES, CONV_DIM), F32),
            pltpu.VMEM((2 * SUBLANES, POOL_DIM), F32),
            pltpu.VMEM((N_GROUPS, D_STATE, GROUP_COLS), F32),
            pltpu.VMEM((D_MODEL, MIX_WIDTH), BF16), pltpu.VMEM(wp.shape, BF16), pltpu.SemaphoreType.DMA((1,)),
            pltpu.VMEM((q, D_MODEL), BF16),
            pltpu.VMEM((2, STAGE_ROWS, STAGE_COLS), F32), pltpu.SemaphoreType.DMA((2,))]
            + [pltpu.VMEM((q, LANES), F32)] * 8,
        compiler_params=pltpu.CompilerParams(dimension_semantics=("arbitrary",),
                                             vmem_limit_bytes=VMEM_LIMIT),
        name="mix_prompt",
    )(*([x] * (tps + 1)), xs, *consts)


def _sample_tok_kernel(xbc_ref, dt_ref, u_ref, sconv_ref, spool_ref, cw_ref, cb_ref, dtb_ref, alog_ref,
                       wp_ref, ps_ref,
                       xc_ref, dec_ref, dtx_ref, nconv_ref, npool_ref, yb_ref):
    for c0 in range(0, CONV_DIM, 512):
        cols = slice(c0, c0 + 512)
        taps = [sconv_ref[k, :, cols] for k in range(CONV_WIDTH - 1)]
        taps.append(xbc_ref[:, cols])
        xc_ref[:, cols] = _conv_silu(taps, cw_ref, cb_ref, cols)
    for k in range(CONV_WIDTH - 2):
        nconv_ref[k] = sconv_ref[k + 1]
    nconv_ref[CONV_WIDTH - 2] = xbc_ref[...]

    dt = _softplus(dt_ref[...] + dtb_ref[...])
    dec_ref[...] = jnp.exp(dt * (-jnp.exp(alog_ref[...])))
    hrow = lax.broadcasted_iota(jnp.int32, (DT_PAD, D_INNER), 0)
    ccol = lax.broadcasted_iota(jnp.int32, (DT_PAD, D_INNER), 1)
    expand = jnp.where((ccol >= hrow * HEAD_DIM) & (ccol < (hrow + 1) * HEAD_DIM), 1.0, 0.0).astype(BF16)
    dt_e = sum(_dot(part, expand) for part in _split3(dt))
    dtx_ref[...] = dt_e * xc_ref[:, 0:D_INNER]

    cnt_base = PAST_LEN + 1
    for gi, w in enumerate(POOL_WINDOWS):
        cols = slice(gi * POOL_GROUP_DIM, (gi + 1) * POOL_GROUP_DIM)
        u_g = u_ref[:, cols]
        s = u_g
        for k in range(1, w):
            s = s + spool_ref[POOL_BUF - k, :, cols]
        yb_ref[:, cols] = _pool_mix(s, u_g, float(min(cnt_base, w)), wp_ref, ps_ref, gi)
    for k in range(POOL_BUF - 1):
        npool_ref[k] = spool_ref[k + 1]
    npool_ref[POOL_BUF - 1] = u_ref[...]


def _sample_tok(xbc, dt, u, sconv, spool, cw, cb, dtb, alog, wp, ps):
    nb = xbc.shape[0]
    full = lambda a: pl.BlockSpec(a.shape, lambda i: (0,) * a.ndim)
    args = (xbc, dt, u, sconv, spool, cw, cb, dtb, alog, wp, ps)
    out_shape = [jax.ShapeDtypeStruct((nb, CONV_DIM), F32),
                 jax.ShapeDtypeStruct((nb, DT_PAD), F32),
                 jax.ShapeDtypeStruct((nb, D_INNER), F32),
                 jax.ShapeDtypeStruct(sconv.shape, F32),
                 jax.ShapeDtypeStruct(spool.shape, F32),
                 jax.ShapeDtypeStruct((nb, POOL_DIM), BF16)]
    return pl.pallas_call(
        _sample_tok_kernel,
        grid=(1,),
        in_specs=[full(a) for a in args],
        out_specs=[pl.BlockSpec(s.shape, lambda i, nd=len(s.shape): (0,) * nd) for s in out_shape],
        out_shape=out_shape,
        compiler_params=pltpu.CompilerParams(dimension_semantics=("arbitrary",),
                                             vmem_limit_bytes=VMEM_LIMIT),
        name="sample_tok",
    )(*args)


def _sample_ssm_kernel(st_ref, b_ref, c_ref, dec_ref, dtx_ref, xc_ref, z_ref, dskip_ref, sn_ref, ya_ref, nst_ref):
    grow = lax.broadcasted_iota(jnp.int32, (N_GROUPS, D_INNER), 0)
    gcol = lax.broadcasted_iota(jnp.int32, (N_GROUPS, D_INNER), 1)
    gmask = (gcol >= grow * GROUP_COLS) & (gcol < (grow + 1) * GROUP_COLS)
    for t in range(st_ref.shape[0]):
        tok = slice(t, t + 1)
        dtx = jnp.where(gmask, dtx_ref[tok, :], 0.0)
        x_hi, x_mid, _ = _split3(dtx)
        b_hi, b_mid, _ = _split3(b_ref[t])
        lhs = jnp.concatenate([x_hi.astype(F32), x_mid.astype(F32), x_hi.astype(F32)], axis=0)
        rhs = jnp.concatenate([b_hi.astype(F32), b_hi.astype(F32), b_mid.astype(F32)], axis=0)
        upd = lax.dot_general(lhs, rhs, (((0,), (0,)), ((), ())), preferred_element_type=F32)
        dec = dec_ref[tok, :]
        c_b = c_ref[t].astype(BF16)
        y_parts = []
        for g in range(N_GROUPS):
            new_g = []
            for r in range(HEADS_PER_GROUP):
                h = g * HEADS_PER_GROUP + r
                scale = jnp.broadcast_to(dec[:, h:h + 1], (HEAD_DIM, D_STATE))
                new = st_ref[t, h] * scale + upd[h * HEAD_DIM:(h + 1) * HEAD_DIM, :]
                nst_ref[t, h] = new
                new_g.append(new.astype(BF16))
            y_g = _dot_nt(c_b, jnp.concatenate(new_g, axis=0))
            gcols = slice(g * GROUP_COLS, (g + 1) * GROUP_COLS)
            y_parts.append(y_g[g:g + 1, :] + dskip_ref[:, gcols] * xc_ref[tok, gcols])
        y = jnp.concatenate(y_parts, axis=1)
        ya_ref[tok, :] = _rms(y * _silu(z_ref[tok, :]), sn_ref[...])


def _sample_ssm(state, b3, c3, dec, dtx, xc, z, dskip_e, sn):
    nb = state.shape[0]
    tb = _token_tile(nb, SUBLANES)
    tok2 = lambda i: (i, 0)
    tok3 = lambda i: (i, 0, 0)
    state_spec = pl.BlockSpec((tb, N_HEADS, HEAD_DIM, D_STATE), lambda i: (i, 0, 0, 0))
    return pl.pallas_call(
        _sample_ssm_kernel,
        grid=(nb // tb,),
        in_specs=[state_spec,
                  pl.BlockSpec((tb, N_GROUPS, D_STATE), tok3),
                  pl.BlockSpec((tb, N_GROUPS, D_STATE), tok3),
                  pl.BlockSpec((tb, DT_PAD), tok2),
                  pl.BlockSpec((tb, D_INNER), tok2),
                  pl.BlockSpec((tb, D_INNER), tok2),
                  pl.BlockSpec((tb, D_INNER), tok2),
                  _const_spec(dskip_e.shape), _const_spec(sn.shape)],
        out_specs=[pl.BlockSpec((tb, D_INNER), tok2), state_spec],
        out_shape=[jax.ShapeDtypeStruct((nb, D_INNER), F32),
                   jax.ShapeDtypeStruct(state.shape, F32)],
        compiler_params=pltpu.CompilerParams(dimension_semantics=("arbitrary",),
                                             vmem_limit_bytes=VMEM_LIMIT),
        name="sample_ssm",
    )(state, b3, c3, dec, dtx, xc, z, dskip_e, sn)


def _out_math(x, ya_in, yb_in, gpre_ref, wg_ref, wa_ref, wb_ref, wo_ref, gpost_ref, fpre_ref, w1_ref, w2_ref,
              fpost_ref):
    xn = _rms(x, gpre_ref[...]).astype(BF16)
    ya = _dot(ya_in.astype(BF16), wa_ref[...])
    yb = _dot(yb_in, wb_ref[...])
    merged = (jax.nn.sigmoid(_dot(xn, wg_ref[:, 0:D_MODEL])) * ya
              + jax.nn.sigmoid(_dot(xn, wg_ref[:, D_MODEL:])) * yb)
    mo = _dot(merged.astype(BF16), wo_ref[...])
    h = x + _rms(mo, gpost_ref[...])
    hn = _rms(h, fpre_ref[...]).astype(BF16)
    gate = _dot(hn, w1_ref[:, 0:D_FF])
    up = _dot(hn, w1_ref[:, D_FF:])
    f = _dot((_silu(gate) * up).astype(BF16), w2_ref[...])
    return h + _rms(f, fpost_ref[...])


def _out_kernel(x_ref, ya_ref, yb_ref, xs_ref, yas_ref, ybs_ref, gpre_ref, wint_hbm, wa_hbm, wb_hbm, wo_hbm, gpost_ref,
                fpre_ref, w1_hbm, w2_hbm, fpost_ref, o_ref, os_ref,
                wg_ref, wa_ref, wb_ref, wo_ref, w1_ref, w2_ref, stage, ssem):
    i = pl.program_id(0)
    n_prompt = pl.num_programs(0) - 1
    params = (gpre_ref, wg_ref, wa_ref, wb_ref, wo_ref, gpost_ref, fpre_ref, w1_ref, w2_ref, fpost_ref)

    @pl.when(i == 0)
    def _():
        chunks = _transposed_chunks(wint_hbm, IN_O_GATE, 2 * D_MODEL, wg_ref, 0)
        for src, dst in ((wa_hbm, wa_ref), (wb_hbm, wb_ref), (wo_hbm, wo_ref), (w1_hbm, w1_ref), (w2_hbm, w2_ref)):
            chunks += _plain_chunks(src, dst)
        _prep_weights(chunks, stage, ssem)

    @pl.when(i < n_prompt)
    def _():
        o_ref[...] = _out_math(x_ref[...], ya_ref[...], yb_ref[...], *params)

    @pl.when(i == n_prompt)
    def _():
        os_ref[...] = _out_math(xs_ref[...], yas_ref[...], ybs_ref[...], *params)


def _out(x, ya, yb, xs, yas, ybs, consts, tm):
    m = x.shape[0]
    n = m // tm
    nb = xs.shape[0]
    tok = lambda i: (jnp.minimum(i, n - 1), 0)
    big = [a.shape[0] >= D_MODEL and a.shape[1] >= D_MODEL for a in consts]
    resident = [(D_MODEL, 2 * D_MODEL)] + [a.shape for a, b in zip(consts, big) if b][1:]
    once = lambda a: pl.BlockSpec(a.shape, lambda i: (0, 0), pipeline_mode=pl.Buffered(1))
    return pl.pallas_call(
        _out_kernel,
        grid=(n + 1,),
        in_specs=[pl.BlockSpec((tm, D_MODEL), tok), pl.BlockSpec((tm, D_INNER), tok),
                  pl.BlockSpec((tm, POOL_DIM), tok), once(xs), once(yas), once(ybs)]
                 + [_HBM_SPEC if b else _const_spec(a.shape) for a, b in zip(consts, big)],
        out_specs=[pl.BlockSpec((tm, D_MODEL), tok), pl.BlockSpec((nb, D_MODEL), lambda i: (0, 0))],
        out_shape=[jax.ShapeDtypeStruct((m, D_MODEL), F32), jax.ShapeDtypeStruct((nb, D_MODEL), F32)],
        scratch_shapes=[pltpu.VMEM(shape, BF16) for shape in resident]
                       + [pltpu.VMEM((2, STAGE_ROWS, STAGE_COLS), F32), pltpu.SemaphoreType.DMA((2,))],
        compiler_params=pltpu.CompilerParams(dimension_semantics=("arbitrary",),
                                             vmem_limit_bytes=VMEM_LIMIT_OUT),
        name="out",
    )(x, ya, yb, xs, yas, ybs, *consts)


def _token_tile(m, cap):
    t = min(m, cap)
    assert m % t == 0
    return t


def kernel(x_prompt, x_sample, state_conv, state_ssm, state_pool, norm_mix_pre, norm_mix_post, norm_ffn_pre,
           norm_ffn_post, w_in, conv_w, conv_b, dt_bias, a_log, d_skip, ssm_norm, w_pool_group, pool_scale,
           w_branch_a, w_branch_b, w_out, w_ffn_in, w_ffn_out):
    bsz, seq, _ = x_prompt.shape
    nb, dec_seq, _ = x_sample.shape
    assert dec_seq == 1 and norm_mix_pre.shape[0] == 1
    l = 0
    wit = jnp.swapaxes(w_in[l], 0, 1)
    pad_h = lambda v: jnp.pad(v.astype(F32), (0, DT_PAD - N_HEADS)).reshape(1, DT_PAD)
    row = lambda v: v.astype(F32).reshape(1, -1)
    dtb, alog = pad_h(dt_bias[l]), pad_h(a_log[l])
    dskip_e = jnp.repeat(d_skip[l].astype(F32), HEAD_DIM).reshape(1, D_INNER)
    cw, cb = conv_w[l].astype(F32), row(conv_b[l])
    wp, ps = w_pool_group[l].astype(BF16), row(pool_scale[l])
    g_pre, sn = row(norm_mix_pre[l]), row(ssm_norm[l])
    out_consts = (g_pre, wit, w_branch_a[l], w_branch_b[l], w_out[l],
                  row(norm_mix_post[l]), row(norm_ffn_pre[l]), w_ffn_in[l], w_ffn_out[l],
                  row(norm_ffn_post[l]))

    xp = x_prompt.reshape(bsz * seq, D_MODEL)
    xs2 = x_sample.reshape(nb, D_MODEL)
    ya, yb, nconv_p, nssm_p, npool_p, z, xbc, u, dt = _mix_prompt(xp, xs2, g_pre, wit, cw, cb, dtb, alog, dskip_e, wp,
                                                                  ps, sn, bsz, seq)
    sconv = jnp.swapaxes(state_conv[l], 0, 1)
    spool = jnp.swapaxes(state_pool[l], 0, 1)
    xc, dec, dtx, nconv_s, npool_s, yb_s = _sample_tok(xbc, dt, u, sconv, spool, cw, cb, dtb, alog, wp, ps)
    b3 = xc[:, D_INNER:D_INNER + N_GROUPS * D_STATE].reshape(nb, N_GROUPS, D_STATE)
    c3 = xc[:, D_INNER + N_GROUPS * D_STATE:].reshape(nb, N_GROUPS, D_STATE)
    ya_s, nssm_s = _sample_ssm(state_ssm[l], b3, c3, dec, dtx, xc, z, dskip_e, sn)
    out_p, out_s = _out(xp, ya, yb, xs2, ya_s, yb_s, out_consts, _token_tile(bsz * seq, 256))
    out_p = out_p.reshape(bsz, seq, D_MODEL)
    out_s = out_s.reshape(nb, 1, D_MODEL)

    return (out_p, out_s,
            nconv_p[None], nssm_p[None], npool_p[None],
            jnp.swapaxes(nconv_s, 0, 1)[None], nssm_s[None],
            jnp.swapaxes(npool_s, 0, 1)[None])
```

```python
import functools

import jax
import jax.numpy as jnp
from jax import lax
from jax.experimental import pallas as pl
from jax.experimental.pallas import tpu as pltpu

D_MODEL = 1024
D_INNER = 2048
HEAD_DIM = 64
N_HEADS = 32
N_GROUPS = 8
HEADS_PER_GROUP = 4
D_STATE = 128
CONV_WIDTH = 4
CONV_DIM = 4096
CHUNK = 128
POOL_DIM = 1024
POOL_WINDOWS = (2, 4, 8, 16)
POOL_GROUP_DIM = 256
POOL_BUF = 15
D_FF = 2816
EPS = 1e-6
PAST_LEN = 16384
LOG2E = 1.4426950408889634

LANES = 128
SUBLANES = 8
TILES = CHUNK // SUBLANES
GROUP_COLS = HEADS_PER_GROUP * HEAD_DIM
DT_PAD = LANES
MIX_WIDTH = D_INNER + CONV_DIM + POOL_DIM + DT_PAD
CONV_HALO = (CONV_WIDTH - 1) * SUBLANES
POOL_HALO = CHUNK
VMEM_LIMIT = 56 * 1024 * 1024
TILES_PER_STEP = 4
VMEM_LIMIT_OUT = 62 * 1024 * 1024

F32 = jnp.float32
BF16 = jnp.bfloat16


def _rms(x, g):
    y = x * lax.rsqrt(jnp.mean(x * x, axis=-1, keepdims=True) + EPS)
    return y * g


def _silu(x):
    return x * jax.nn.sigmoid(x)


def _softplus(x):
    return jnp.maximum(x, 0.0) + jnp.log(1.0 + jnp.exp(-jnp.abs(x)))


def _split3(x):
    hi = x.astype(BF16)
    r = x - hi.astype(F32)
    mid = r.astype(BF16)
    lo = (r - mid.astype(F32)).astype(BF16)
    return hi, mid, lo


def _dot(a, b):
    return jnp.dot(a, b, preferred_element_type=F32)


def _dot_nt(a, b):
    return lax.dot_general(a, b, (((1,), (1,)), ((), ())), preferred_element_type=F32)


def _const_spec(shape):
    nd = len(shape)
    return pl.BlockSpec(shape, lambda *_: (0,) * nd)


_HBM_SPEC = pl.BlockSpec(memory_space=pl.ANY)


def _load_resident(pairs, sem):
    copies = [pltpu.make_async_copy(src, dst, sem.at[i]) for i, (src, dst) in enumerate(pairs)]
    for c in copies:
        c.start()
    for c in copies:
        c.wait()


STAGE_ROWS = 512
STAGE_COLS = 1024
PREP_ROWS = 256


def _prep_weights(chunks, stage, sem):
    def copy(i):
        src, r0, nr, c0, nc, _ = chunks[i]
        return pltpu.make_async_copy(src.at[pl.ds(r0, nr), pl.ds(c0, nc)],
                                     stage.at[i % 2, pl.ds(0, nr), pl.ds(0, nc)], sem.at[i % 2])

    copy(0).start()
    for i, (_, _, nr, _, nc, emit) in enumerate(chunks):
        if i + 1 < len(chunks):
            copy(i + 1).start()
        copy(i).wait()
        step = min(PREP_ROWS, nr)
        for r in range(0, nr, step):
            emit(slice(r, r + step), stage[i % 2, r:r + step, 0:nc])


def _plain_chunks(src, dst):
    rows, cols = src.shape

    def emit_at(r0, c0, nc):
        def emit(rs, v):
            dst[r0 + rs.start:r0 + rs.stop, c0:c0 + nc] = v.astype(BF16)
        return emit

    return [(src, r0, min(STAGE_ROWS, rows - r0), c0, min(STAGE_COLS, cols - c0),
             emit_at(r0, c0, min(STAGE_COLS, cols - c0)))
            for r0 in range(0, rows, STAGE_ROWS) for c0 in range(0, cols, STAGE_COLS)]


IN_O_DT = D_INNER + CONV_DIM
IN_O_POOL = IN_O_DT + N_HEADS
IN_O_GATE = IN_O_POOL + POOL_DIM


def _transposed_chunks(src_t, row0, n_rows, dst, dst_col0):
    def emit_at(j0):
        def emit(rs, v):
            dst[:, dst_col0 + j0 + rs.start:dst_col0 + j0 + rs.stop] = v.T.astype(BF16)
        return emit

    return [(src_t, row0 + j0, min(STAGE_ROWS, n_rows - j0), 0, D_MODEL, emit_at(j0))
            for j0 in range(0, n_rows, STAGE_ROWS)]


def _mix_weight_chunks(w_in_t, w_ref):
    def emit_dt(rs, v):
        vt = v.T
        lane = lax.broadcasted_iota(jnp.int32, vt.shape, 1)
        w_ref[:, MIX_WIDTH - DT_PAD:MIX_WIDTH] = jnp.where(lane < N_HEADS, vt, 0.0).astype(BF16)

    chunks = _transposed_chunks(w_in_t, 0, IN_O_DT, w_ref, 0)
    chunks += _transposed_chunks(w_in_t, IN_O_POOL, POOL_DIM, w_ref, IN_O_DT)
    chunks.append((w_in_t, IN_O_DT, DT_PAD, 0, D_MODEL, emit_dt))
    return chunks


def _time_of_row(r):
    return (r & (SUBLANES - 1)) * TILES + (r >> 3)


def _conv_silu(taps, cw_ref, cb_ref, cols):
    acc = cb_ref[:, cols]
    for k, t in enumerate(taps):
        acc = acc + t * cw_ref[k:k + 1, cols]
    return _silu(acc)


def _pool_mix(win_sum, u, cnt, wp_ref, ps_ref, gi):
    cols = slice(gi * POOL_GROUP_DIM, (gi + 1) * POOL_GROUP_DIM)
    d = win_sum / cnt - u
    mixed = _dot(d.astype(BF16), wp_ref[gi])
    return (mixed * ps_ref[:, cols]).astype(BF16)


def _project(x_ref, gpre_ref, w_ref, dtb_ref, to_perm, xnp_buf, xbc_buf, z_buf, u_buf, dt_buf):
    xn = _rms(x_ref[...], gpre_ref[...]).astype(BF16)
    xnp_buf[...] = _dot(to_perm, xn).astype(BF16)
    yield

    def proj(c0, width):
        tile = 256
        return jnp.concatenate([_dot(xnp_buf[...], w_ref[:, c:c + min(tile, c0 + width - c)])
                                for c in range(c0, c0 + width, tile)], axis=1)

    piece = 512
    for c0 in range(0, CONV_DIM, piece):
        xbc_buf[CONV_HALO:, c0:c0 + piece] = proj(D_INNER + c0, piece)
        yield
    o_u = D_INNER + CONV_DIM
    for c0 in range(0, POOL_DIM, piece):
        u_buf[POOL_HALO:, c0:c0 + piece] = proj(o_u + c0, piece)
        yield
    dt_buf[...] = _softplus(proj(MIX_WIDTH - DT_PAD, DT_PAD) + dtb_ref[...])
    for c0 in range(0, D_INNER, piece):
        z_buf[:, c0:c0 + piece] = proj(c0, piece)
        yield


def _run(gen):
    for _ in gen:
        pass


def _interleave(main, side, side_after):
    next(side)
    i = 0
    while next(main, StopIteration) is not StopIteration:
        for _ in range(side_after(i)):
            next(side, None)
        i += 1
    _run(side)


def _project_pieces_after(i):
    return i % 2


def _sequence(xbc_buf, z_buf, u_buf, dt_buf, xc_buf, y_buf, ctail, ptail, ht_ref,
              cw_ref, cb_ref, alog_ref, dskip_ref, wp_ref, ps_ref, sn_ref, masks, ya_ref, yb_ref, rows, tile_in_seq):
    q = CHUNK
    to_nat, causal, tri, lo_half, t_col = masks

    for n, j in enumerate(range(TILES - (CONV_WIDTH - 1), TILES)):
        src = CONV_HALO + j * SUBLANES
        xbc_buf[n * SUBLANES + 1:(n + 1) * SUBLANES, :] = xbc_buf[src:src + SUBLANES - 1, :]
        xbc_buf[n * SUBLANES:n * SUBLANES + 1, :] = ctail[n:n + 1, :]
    for c0 in range(0, CONV_DIM, 512):
        cols = slice(c0, c0 + 512)
        taps = []
        for k in range(CONV_WIDTH):
            start = CONV_HALO - (CONV_WIDTH - 1 - k) * SUBLANES
            taps.append(xbc_buf[start:start + q, cols])
        xc = _conv_silu(taps, cw_ref, cb_ref, cols)
        xc_buf[:, cols] = xc
        yield
    for n, j in enumerate(range(TILES - (CONV_WIDTH - 1), TILES)):
        src = CONV_HALO + j * SUBLANES + SUBLANES - 1
        ctail[n:n + 1, :] = xbc_buf[src:src + 1, :]

    dt = dt_buf[...]
    da = dt * (-jnp.exp(alog_ref[...]))
    acum = sum(_dot(tri, part) for part in _split3(da)) * LOG2E
    acum_last = acum[q - 1:q, :]
    e_last = jnp.exp2(acum_last)
    w_end_t = (jnp.exp2(acum_last - acum) * dt).T
    acum_t = acum.T
    dt_t = dt.T
    lo_half_row = lo_half[0:1, :]

    ssq = jnp.zeros((q, 1), F32)
    for g in range(N_GROUPS):
        b_g = xc_buf[:, D_INNER + g * D_STATE:D_INNER + (g + 1) * D_STATE]
        c_off = D_INNER + N_GROUPS * D_STATE + g * D_STATE
        c_b = xc_buf[:, c_off:c_off + D_STATE].astype(BF16)
        cb = _dot_nt(c_b, b_g.astype(BF16))
        b_t = b_g.T
        y_inter = _dot(c_b, ht_ref[g].astype(BF16))
        for pair in range(HEADS_PER_GROUP // 2):
            h0 = g * HEADS_PER_GROUP + 2 * pair
            h1 = h0 + 1
            pcols = slice(h0 * HEAD_DIM, (h0 + 2) * HEAD_DIM)
            scols = slice(pair * LANES, (pair + 1) * LANES)
            x_pair = xc_buf[:, pcols]
            ms, bws, e_cols = [], [], []
            for h in (h0, h1):
                a_col = jnp.broadcast_to(acum[:, h:h + 1], (q, q))
                seg = a_col - acum_t[h:h + 1, :]
                decay = jnp.exp2(jnp.where(causal, seg, -jnp.inf))
                ms.append((cb * decay * dt_t[h:h + 1, :]).astype(BF16))
                bws.append((b_t * w_end_t[h:h + 1, :]).astype(BF16))
                e_cols.append(jnp.exp2(a_col))
            x_top = jnp.where(lo_half, x_pair, 0.0).astype(BF16)
            x_bot = jnp.where(lo_half, 0.0, x_pair).astype(BF16)
            x_diag = jnp.concatenate([x_top, x_bot], axis=0)
            y_intra = _dot(jnp.concatenate(ms, axis=1), x_diag)
            e_pair = jnp.where(lo_half, e_cols[0], e_cols[1])
            y_pair = y_intra + y_inter[:, scols] * e_pair + dskip_ref[:, pcols] * x_pair
            gated = y_pair * _silu(z_buf[:, pcols])
            y_buf[:, pcols] = gated
            ssq = ssq + jnp.sum(gated * gated, axis=-1, keepdims=True)
            upd = _dot(jnp.concatenate(bws, axis=1), x_diag)
            e_last_pair = jnp.where(lo_half_row, e_last[:, h0:h0 + 1], e_last[:, h1:h1 + 1])
            ht_ref[g, :, scols] = ht_ref[g, :, scols] * e_last_pair + upd
            yield

    inv = lax.rsqrt(ssq * (1.0 / D_INNER) + EPS)
    for c0 in range(0, D_INNER, 1024):
        cols = slice(c0, c0 + 1024)
        ya = (y_buf[:, cols] * inv * sn_ref[:, cols]).astype(BF16)
        ya_ref[rows, cols] = _dot(to_nat, ya).astype(BF16)
        yield

    u_buf[1:q, :] = u_buf[POOL_HALO:POOL_HALO + q - 1, :]
    for j in range(1, TILES):
        u_buf[j * SUBLANES:j * SUBLANES + 1, :] = ptail[j - 1:j, :]
    pos = tile_in_seq * q + t_col
    yb_parts = []
    for gi, w in enumerate(POOL_WINDOWS):
        cols = slice(gi * POOL_GROUP_DIM, (gi + 1) * POOL_GROUP_DIM)
        u_g = u_buf[POOL_HALO:POOL_HALO + q, cols]
        s = u_buf[POOL_HALO - (w - 1) * SUBLANES:POOL_HALO + q, cols]
        shift = SUBLANES
        while shift < w * SUBLANES:
            s = s[shift:, :] + s[:-shift, :]
            shift *= 2
        cnt = jnp.minimum(pos + 1, w).astype(F32)
        yb_parts.append(_pool_mix(s, u_g, cnt, wp_ref, ps_ref, gi))
        yield
    yb_ref[rows, :] = _dot(to_nat, jnp.concatenate(yb_parts, axis=1)).astype(BF16)
    for j in range(1, TILES):
        src = POOL_HALO + j * SUBLANES + SUBLANES - 1
        ptail[j - 1:j, :] = u_buf[src:src + 1, :]


def _project_plain(x_ref, g_ref, w_ref, out_refs):
    xn = _rms(x_ref[...], g_ref[...]).astype(BF16)
    off = 0
    for o_ref in out_refs:
        width = o_ref.shape[1]
        for c in range(0, width, 1024):
            cw = min(1024, width - c)
            o_ref[:, c:c + cw] = _dot(xn, w_ref[:, off + c:off + c + cw])
        off += width


def _mix_kernel(x0_ref, x1_ref, x2_ref, x3_ref, x4_ref, xs_ref, gpre_ref, w_hbm, cw_ref, cb_ref, dtb_ref, alog_ref, dskip_ref,
                wp_hbm, ps_ref, sn_ref,
                ya_ref, yb_ref, nconv_ref, nssm_ref, npool_ref, zs_ref, xbcs_ref, us_ref, dts_ref,
                xbc0, z0, u0, dt0, xbc1, z1, u1, dt1, xc0, y0, xc1, y1, ctail, ptail, ht_ref,
                w_ref, wp_ref, wsem, xnp_buf, stage, ssem, *, tiles_per_seq):
    q = CHUNK
    k = pl.program_id(0)
    tile_in_seq = lax.rem(TILES_PER_STEP * k, tiles_per_seq)
    row = lax.broadcasted_iota(jnp.int32, (q, q), 0)
    col = lax.broadcasted_iota(jnp.int32, (q, q), 1)
    to_perm = jnp.where(col == _time_of_row(row), 1.0, 0.0).astype(BF16)
    to_nat = jnp.where(_time_of_row(col) == row, 1.0, 0.0).astype(BF16)
    causal = _time_of_row(row) >= _time_of_row(col)
    tri = jnp.where(causal, 1.0, 0.0).astype(BF16)
    t_col = _time_of_row(lax.broadcasted_iota(jnp.int32, (q, 1), 0))
    masks = (to_nat, causal, tri, col < HEAD_DIM, t_col)
    set0 = (xbc0, z0, u0, dt0)
    set1 = (xbc1, z1, u1, dt1)
    state = (ctail, ptail, ht_ref)
    consts = (cw_ref, cb_ref, alog_ref, dskip_ref, wp_ref, ps_ref, sn_ref, masks)

    @pl.when(k == 0)
    def _():
        _load_resident([(wp_hbm, wp_ref)], wsem)
        _prep_weights(_mix_weight_chunks(w_hbm, w_ref), stage, ssem)
        _run(_project(x0_ref, gpre_ref, w_ref, dtb_ref, to_perm, xnp_buf, *set0))

    @pl.when(tile_in_seq == 0)
    def _():
        ctail[...] = jnp.zeros(ctail.shape, F32)
        ptail[...] = jnp.zeros(ptail.shape, F32)
        ht_ref[...] = jnp.zeros(ht_ref.shape, F32)

    sets = (set0, set1)
    seq_scratch = ((xc0, y0), (xc1, y1))
    for p, x_next in enumerate((x1_ref, x2_ref, x3_ref, x4_ref)):
        _interleave(_sequence(*sets[p % 2], *seq_scratch[p % 2], *state, *consts, ya_ref, yb_ref,
                              slice(p * q, (p + 1) * q), tile_in_seq + p),
                    _project(x_next, gpre_ref, w_ref, dtb_ref, to_perm, xnp_buf, *sets[(p + 1) % 2]),
                    _project_pieces_after)

    @pl.when(tile_in_seq == tiles_per_seq - TILES_PER_STEP)
    def _():
        for g in range(N_GROUPS):
            nssm_ref[0, g * HEADS_PER_GROUP:(g + 1) * HEADS_PER_GROUP] = (
                ht_ref[g].T.reshape(HEADS_PER_GROUP, HEAD_DIM, D_STATE))

    steps_per_seq = tiles_per_seq // TILES_PER_STEP
    for b in range(nconv_ref.shape[1]):
        @pl.when(k == (b + 1) * steps_per_seq - 1)
        def _():
            for j in range(CONV_WIDTH - 1):
                nconv_ref[j, b:b + 1, :] = ctail[j:j + 1, :]
            for j in range(POOL_BUF):
                npool_ref[j, b:b + 1, :] = ptail[j:j + 1, :]

    @pl.when(k == pl.num_programs(0) - 1)
    def _():
        _project_plain(xs_ref, gpre_ref, w_ref, (zs_ref, xbcs_ref, us_ref, dts_ref))


def _mix_prompt(x, xs, gpre, w, cw, cb, dtb, alog, dskip_e, wp, ps, sn, bsz, seq):
    q = CHUNK
    nt = seq // q
    n_tiles = bsz * nt
    tps = TILES_PER_STEP
    assert nt % tps == 0
    steps_per_seq = nt // tps
    m = bsz * seq
    consts = (gpre, w, cw, cb, dtb, alog, dskip_e, wp, ps, sn)
    nb = xs.shape[0]
    sample_widths = (D_INNER, CONV_DIM, POOL_DIM, DT_PAD)
    pair = lambda k: (k, 0)
    proj_set = [pltpu.VMEM((CONV_HALO + q, CONV_DIM), F32), pltpu.VMEM((q, D_INNER), F32),
                pltpu.VMEM((POOL_HALO + q, POOL_DIM), F32), pltpu.VMEM((q, DT_PAD), F32)]
    seq_set = [pltpu.VMEM((q, CONV_DIM), F32), pltpu.VMEM((q, D_INNER), F32)]
    return pl.pallas_call(
        functools.partial(_mix_kernel, tiles_per_seq=nt),
        grid=(n_tiles // tps,),
        in_specs=[pl.BlockSpec((q, D_MODEL), lambda k: (0, 0)),
                  *[pl.BlockSpec((q, D_MODEL), lambda k, p=p: (jnp.minimum(tps * k + p, n_tiles - 1), 0))
                    for p in range(1, tps + 1)],
                  _const_spec(xs.shape)]
                 + [_HBM_SPEC if a is w or a is wp else _const_spec(a.shape) for a in consts],
        out_specs=[pl.BlockSpec((tps * q, D_INNER), pair),
                   pl.BlockSpec((tps * q, POOL_DIM), pair),
                   _const_spec((CONV_WIDTH - 1, bsz, CONV_DIM)),
                   pl.BlockSpec((1, N_HEADS, HEAD_DIM, D_STATE), lambda k: (k // steps_per_seq, 0, 0, 0)),
                   _const_spec((POOL_BUF, bsz, POOL_DIM))]
                  + [_const_spec((nb, wd)) for wd in sample_widths],
        out_shape=[jax.ShapeDtypeStruct((m, D_INNER), BF16),
                   jax.ShapeDtypeStruct((m, POOL_DIM), BF16),
                   jax.ShapeDtypeStruct((CONV_WIDTH - 1, bsz, CONV_DIM), F32),
                   jax.ShapeDtypeStruct((bsz, N_HEADS, HEAD_DIM, D_STATE), F32),
                   jax.ShapeDtypeStruct((POOL_BUF, bsz, POOL_DIM), F32)]
                  + [jax.ShapeDtypeStruct((nb, wd), F32) for wd in sample_widths],
        scratch_shapes=proj_set + proj_set + seq_set + seq_set + [
            pltpu.VMEM((SUBLANES, CONV_DIM), F32),
            pltpu.VMEM((2 * SUBLANES, POOL_DIM), F32),
            pltpu.VMEM((N_GROUPS, D_STATE, GROUP_COLS), F32),
            pltpu.VMEM((D_MODEL, MIX_WIDTH), BF16), pltpu.VMEM(wp.shape, BF16), pltpu.SemaphoreType.DMA((1,)),
            pltpu.VMEM((q, D_MODEL), BF16),
            pltpu.VMEM((2, STAGE_ROWS, STAGE_COLS), F32), pltpu.SemaphoreType.DMA((2,))],
        compiler_params=pltpu.CompilerParams(dimension_semantics=("arbitrary",),
                                             vmem_limit_bytes=VMEM_LIMIT),
        name="mix_prompt",
    )(*([x] * (tps + 1)), xs, *consts)


def _sample_tok_kernel(xbc_ref, dt_ref, u_ref, sconv_ref, spool_ref, cw_ref, cb_ref, dtb_ref, alog_ref,
                       wp_ref, ps_ref,
                       xc_ref, dec_ref, dtx_ref, nconv_ref, npool_ref, yb_ref):
    for c0 in range(0, CONV_DIM, 512):
        cols = slice(c0, c0 + 512)
        taps = [sconv_ref[k, :, cols] for k in range(CONV_WIDTH - 1)]
        taps.append(xbc_ref[:, cols])
        xc_ref[:, cols] = _conv_silu(taps, cw_ref, cb_ref, cols)
    for k in range(CONV_WIDTH - 2):
        nconv_ref[k] = sconv_ref[k + 1]
    nconv_ref[CONV_WIDTH - 2] = xbc_ref[...]

    dt = _softplus(dt_ref[...] + dtb_ref[...])
    dec_ref[...] = jnp.exp(dt * (-jnp.exp(alog_ref[...])))
    hrow = lax.broadcasted_iota(jnp.int32, (DT_PAD, D_INNER), 0)
    ccol = lax.broadcasted_iota(jnp.int32, (DT_PAD, D_INNER), 1)
    expand = jnp.where((ccol >= hrow * HEAD_DIM) & (ccol < (hrow + 1) * HEAD_DIM), 1.0, 0.0).astype(BF16)
    dt_e = sum(_dot(part, expand) for part in _split3(dt))
    dtx_ref[...] = dt_e * xc_ref[:, 0:D_INNER]

    cnt_base = PAST_LEN + 1
    for gi, w in enumerate(POOL_WINDOWS):
        cols = slice(gi * POOL_GROUP_DIM, (gi + 1) * POOL_GROUP_DIM)
        u_g = u_ref[:, cols]
        s = u_g
        for k in range(1, w):
            s = s + spool_ref[POOL_BUF - k, :, cols]
        yb_ref[:, cols] = _pool_mix(s, u_g, float(min(cnt_base, w)), wp_ref, ps_ref, gi)
    for k in range(POOL_BUF - 1):
        npool_ref[k] = spool_ref[k + 1]
    npool_ref[POOL_BUF - 1] = u_ref[...]


def _sample_tok(xbc, dt, u, sconv, spool, cw, cb, dtb, alog, wp, ps):
    nb = xbc.shape[0]
    full = lambda a: pl.BlockSpec(a.shape, lambda i: (0,) * a.ndim)
    args = (xbc, dt, u, sconv, spool, cw, cb, dtb, alog, wp, ps)
    out_shape = [jax.ShapeDtypeStruct((nb, CONV_DIM), F32),
                 jax.ShapeDtypeStruct((nb, DT_PAD), F32),
                 jax.ShapeDtypeStruct((nb, D_INNER), F32),
                 jax.ShapeDtypeStruct(sconv.shape, F32),
                 jax.ShapeDtypeStruct(spool.shape, F32),
                 jax.ShapeDtypeStruct((nb, POOL_DIM), BF16)]
    return pl.pallas_call(
        _sample_tok_kernel,
        grid=(1,),
        in_specs=[full(a) for a in args],
        out_specs=[pl.BlockSpec(s.shape, lambda i, nd=len(s.shape): (0,) * nd) for s in out_shape],
        out_shape=out_shape,
        compiler_params=pltpu.CompilerParams(dimension_semantics=("arbitrary",),
                                             vmem_limit_bytes=VMEM_LIMIT),
        name="sample_tok",
    )(*args)


def _sample_ssm_kernel(st_ref, b_ref, c_ref, dec_ref, dtx_ref, xc_ref, z_ref, dskip_ref, sn_ref, ya_ref, nst_ref):
    grow = lax.broadcasted_iota(jnp.int32, (N_GROUPS, D_INNER), 0)
    gcol = lax.broadcasted_iota(jnp.int32, (N_GROUPS, D_INNER), 1)
    gmask = (gcol >= grow * GROUP_COLS) & (gcol < (grow + 1) * GROUP_COLS)
    srow = lax.broadcasted_iota(jnp.int32, (N_GROUPS, D_STATE), 0)

    def per_group(row):
        out = jnp.zeros((N_GROUPS, D_STATE), F32)
        for g in range(N_GROUPS):
            out = jnp.where(srow == g, row[:, g * D_STATE:(g + 1) * D_STATE], out)
        return out

    for t in range(st_ref.shape[0]):
        tok = slice(t, t + 1)
        dtx = jnp.where(gmask, dtx_ref[tok, :], 0.0)
        x_hi, x_mid, _ = _split3(dtx)
        b_hi, b_mid, _ = _split3(per_group(b_ref[tok, :]))
        lhs = jnp.concatenate([x_hi.astype(F32), x_mid.astype(F32), x_hi.astype(F32)], axis=0)
        rhs = jnp.concatenate([b_hi.astype(F32), b_hi.astype(F32), b_mid.astype(F32)], axis=0)
        upd = lax.dot_general(lhs, rhs, (((0,), (0,)), ((), ())), preferred_element_type=F32)
        dec = dec_ref[tok, :]
        c_b = per_group(c_ref[tok, :]).astype(BF16)
        y_parts = []
        for g in range(N_GROUPS):
            new_g = []
            for r in range(HEADS_PER_GROUP):
                h = g * HEADS_PER_GROUP + r
                scale = jnp.broadcast_to(dec[:, h:h + 1], (HEAD_DIM, D_STATE))
                new = st_ref[t, h] * scale + upd[h * HEAD_DIM:(h + 1) * HEAD_DIM, :]
                nst_ref[t, h] = new
                new_g.append(new.astype(BF16))
            y_g = _dot_nt(c_b, jnp.concatenate(new_g, axis=0))
            gcols = slice(g * GROUP_COLS, (g + 1) * GROUP_COLS)
            y_parts.append(y_g[g:g + 1, :] + dskip_ref[:, gcols] * xc_ref[tok, gcols])
        y = jnp.concatenate(y_parts, axis=1)
        ya_ref[tok, :] = _rms(y * _silu(z_ref[tok, :]), sn_ref[...])


def _sample_ssm(state, dec, dtx, xc, z, dskip_e, sn):
    nb = state.shape[0]
    tb = _token_tile(nb, SUBLANES)
    tok2 = lambda i: (i, 0)
    bc_width = N_GROUPS * D_STATE
    assert D_INNER % bc_width == 0
    state_spec = pl.BlockSpec((tb, N_HEADS, HEAD_DIM, D_STATE), lambda i: (i, 0, 0, 0))
    return pl.pallas_call(
        _sample_ssm_kernel,
        grid=(nb // tb,),
        in_specs=[state_spec,
                  pl.BlockSpec((tb, bc_width), lambda i: (i, D_INNER // bc_width)),
                  pl.BlockSpec((tb, bc_width), lambda i: (i, D_INNER // bc_width + 1)),
                  pl.BlockSpec((tb, DT_PAD), tok2),
                  pl.BlockSpec((tb, D_INNER), tok2),
                  pl.BlockSpec((tb, D_INNER), tok2),
                  pl.BlockSpec((tb, D_INNER), tok2),
                  _const_spec(dskip_e.shape), _const_spec(sn.shape)],
        out_specs=[pl.BlockSpec((tb, D_INNER), tok2), state_spec],
        out_shape=[jax.ShapeDtypeStruct((nb, D_INNER), F32),
                   jax.ShapeDtypeStruct(state.shape, F32)],
        compiler_params=pltpu.CompilerParams(dimension_semantics=("arbitrary",),
                                             vmem_limit_bytes=VMEM_LIMIT),
        name="sample_ssm",
    )(state, xc, xc, dec, dtx, xc, z, dskip_e, sn)


def _out_math(x, ya_in, yb_in, gpre_ref, wg_ref, wa_ref, wb_ref, wo_ref, gpost_ref, fpre_ref, w1_ref, w2_ref,
              fpost_ref):
    xn = _rms(x, gpre_ref[...]).astype(BF16)
    ya = _dot(ya_in.astype(BF16), wa_ref[...])
    yb = _dot(yb_in, wb_ref[...])
    merged = (jax.nn.sigmoid(_dot(xn, wg_ref[:, 0:D_MODEL])) * ya
              + jax.nn.sigmoid(_dot(xn, wg_ref[:, D_MODEL:])) * yb)
    mo = _dot(merged.astype(BF16), wo_ref[...])
    h = x + _rms(mo, gpost_ref[...])
    hn = _rms(h, fpre_ref[...]).astype(BF16)
    gate = _dot(hn, w1_ref[:, 0:D_FF])
    up = _dot(hn, w1_ref[:, D_FF:])
    f = _dot((_silu(gate) * up).astype(BF16), w2_ref[...])
    return h + _rms(f, fpost_ref[...])


def _out_kernel(x_ref, ya_ref, yb_ref, xs_ref, yas_ref, ybs_ref, gpre_ref, wint_hbm, wa_hbm, wb_hbm, wo_hbm, gpost_ref,
                fpre_ref, w1_hbm, w2_hbm, fpost_ref, o_ref, os_ref,
                wg_ref, wa_ref, wb_ref, wo_ref, w1_ref, w2_ref, stage, ssem):
    i = pl.program_id(0)
    n_prompt = pl.num_programs(0) - 1
    params = (gpre_ref, wg_ref, wa_ref, wb_ref, wo_ref, gpost_ref, fpre_ref, w1_ref, w2_ref, fpost_ref)

    @pl.when(i == 0)
    def _():
        chunks = _transposed_chunks(wint_hbm, IN_O_GATE, 2 * D_MODEL, wg_ref, 0)
        for src, dst in ((wa_hbm, wa_ref), (wb_hbm, wb_ref), (wo_hbm, wo_ref), (w1_hbm, w1_ref), (w2_hbm, w2_ref)):
            chunks += _plain_chunks(src, dst)
        _prep_weights(chunks, stage, ssem)

    @pl.when(i < n_prompt)
    def _():
        o_ref[...] = _out_math(x_ref[...], ya_ref[...], yb_ref[...], *params)

    @pl.when(i == n_prompt)
    def _():
        os_ref[...] = _out_math(xs_ref[...], yas_ref[...], ybs_ref[...], *params)


def _out(x, ya, yb, xs, yas, ybs, consts, tm):
    m = x.shape[0]
    n = m // tm
    nb = xs.shape[0]
    tok = lambda i: (jnp.minimum(i, n - 1), 0)
    big = [a.shape[0] >= D_MODEL and a.shape[1] >= D_MODEL for a in consts]
    resident = [(D_MODEL, 2 * D_MODEL)] + [a.shape for a, b in zip(consts, big) if b][1:]
    once = lambda a: pl.BlockSpec(a.shape, lambda i: (0, 0), pipeline_mode=pl.Buffered(1))
    return pl.pallas_call(
        _out_kernel,
        grid=(n + 1,),
        in_specs=[pl.BlockSpec((tm, D_MODEL), tok), pl.BlockSpec((tm, D_INNER), tok),
                  pl.BlockSpec((tm, POOL_DIM), tok), once(xs), once(yas), once(ybs)]
                 + [_HBM_SPEC if b else _const_spec(a.shape) for a, b in zip(consts, big)],
        out_specs=[pl.BlockSpec((tm, D_MODEL), tok), pl.BlockSpec((nb, D_MODEL), lambda i: (0, 0))],
        out_shape=[jax.ShapeDtypeStruct((m, D_MODEL), F32), jax.ShapeDtypeStruct((nb, D_MODEL), F32)],
        scratch_shapes=[pltpu.VMEM(shape, BF16) for shape in resident]
                       + [pltpu.VMEM((2, STAGE_ROWS, STAGE_COLS), F32), pltpu.SemaphoreType.DMA((2,))],
        compiler_params=pltpu.CompilerParams(dimension_semantics=("arbitrary",),
                                             vmem_limit_bytes=VMEM_LIMIT_OUT),
        name="out",
    )(x, ya, yb, xs, yas, ybs, *consts)


def _token_tile(m, cap):
    t = min(m, cap)
    assert m % t == 0
    return t


def kernel(x_prompt, x_sample, state_conv, state_ssm, state_pool, norm_mix_pre, norm_mix_post, norm_ffn_pre,
           norm_ffn_post, w_in, conv_w, conv_b, dt_bias, a_log, d_skip, ssm_norm, w_pool_group, pool_scale,
           w_branch_a, w_branch_b, w_out, w_ffn_in, w_ffn_out):
    bsz, seq, _ = x_prompt.shape
    nb, dec_seq, _ = x_sample.shape
    assert dec_seq == 1 and norm_mix_pre.shape[0] == 1
    l = 0
    wit = jnp.swapaxes(w_in[l], 0, 1)
    pad_h = lambda v: jnp.pad(v.astype(F32), (0, DT_PAD - N_HEADS)).reshape(1, DT_PAD)
    row = lambda v: v.astype(F32).reshape(1, -1)
    dtb, alog = pad_h(dt_bias[l]), pad_h(a_log[l])
    dskip_e = jnp.repeat(d_skip[l].astype(F32), HEAD_DIM).reshape(1, D_INNER)
    cw, cb = conv_w[l].astype(F32), row(conv_b[l])
    wp, ps = w_pool_group[l].astype(BF16), row(pool_scale[l])
    g_pre, sn = row(norm_mix_pre[l]), row(ssm_norm[l])
    out_consts = (g_pre, wit, w_branch_a[l], w_branch_b[l], w_out[l],
                  row(norm_mix_post[l]), row(norm_ffn_pre[l]), w_ffn_in[l], w_ffn_out[l],
                  row(norm_ffn_post[l]))

    xp = x_prompt.reshape(bsz * seq, D_MODEL)
    xs2 = x_sample.reshape(nb, D_MODEL)
    ya, yb, nconv_p, nssm_p, npool_p, z, xbc, u, dt = _mix_prompt(xp, xs2, g_pre, wit, cw, cb, dtb, alog, dskip_e, wp,
                                                                  ps, sn, bsz, seq)
    sconv = jnp.swapaxes(state_conv[l], 0, 1)
    spool = jnp.swapaxes(state_pool[l], 0, 1)
    xc, dec, dtx, nconv_s, npool_s, yb_s = _sample_tok(xbc, dt, u, sconv, spool, cw, cb, dtb, alog, wp, ps)
    ya_s, nssm_s = _sample_ssm(state_ssm[l], dec, dtx, xc, z, dskip_e, sn)
    out_p, out_s = _out(xp, ya, yb, xs2, ya_s, yb_s, out_consts, _token_tile(bsz * seq, 256))
    out_p = out_p.reshape(bsz, seq, D_MODEL)
    out_s = out_s.reshape(nb, 1, D_MODEL)

    return (out_p, out_s,
            jnp.swapaxes(nconv_p, 0, 1)[None], nssm_p[None], jnp.swapaxes(npool_p, 0, 1)[None],
            jnp.swapaxes(nconv_s, 0, 1)[None], nssm_s[None],
            jnp.swapaxes(npool_s, 0, 1)[None])
```

```python
import functools

import jax
import jax.numpy as jnp
from jax import lax
from jax.experimental import pallas as pl
from jax.experimental.pallas import tpu as pltpu

D_MODEL = 1024
D_INNER = 2048
HEAD_DIM = 64
N_HEADS = 32
N_GROUPS = 8
HEADS_PER_GROUP = 4
D_STATE = 128
CONV_WIDTH = 4
CONV_DIM = 4096
CHUNK = 128
POOL_DIM = 1024
POOL_WINDOWS = (2, 4, 8, 16)
POOL_GROUP_DIM = 256
POOL_BUF = 15
D_FF = 2816
EPS = 1e-6
PAST_LEN = 16384
LOG2E = 1.4426950408889634

LANES = 128
SUBLANES = 8
TILES = CHUNK // SUBLANES
GROUP_COLS = HEADS_PER_GROUP * HEAD_DIM
DT_PAD = LANES
MIX_WIDTH = D_INNER + CONV_DIM + POOL_DIM + DT_PAD
CONV_HALO = (CONV_WIDTH - 1) * SUBLANES
POOL_HALO = CHUNK
VMEM_LIMIT = 56 * 1024 * 1024
TILES_PER_STEP = 4
VMEM_LIMIT_OUT = 62 * 1024 * 1024

F32 = jnp.float32
BF16 = jnp.bfloat16


def _rms(x, g):
    y = x * lax.rsqrt(jnp.mean(x * x, axis=-1, keepdims=True) + EPS)
    return y * g


def _silu(x):
    return x * jax.nn.sigmoid(x)


def _softplus(x):
    return jnp.maximum(x, 0.0) + jnp.log(1.0 + jnp.exp(-jnp.abs(x)))


def _split3(x):
    hi = x.astype(BF16)
    r = x - hi.astype(F32)
    mid = r.astype(BF16)
    lo = (r - mid.astype(F32)).astype(BF16)
    return hi, mid, lo


def _dot(a, b):
    return jnp.dot(a, b, preferred_element_type=F32)


def _dot_nt(a, b):
    return lax.dot_general(a, b, (((1,), (1,)), ((), ())), preferred_element_type=F32)


def _const_spec(shape):
    nd = len(shape)
    return pl.BlockSpec(shape, lambda *_: (0,) * nd)


_HBM_SPEC = pl.BlockSpec(memory_space=pl.ANY)


def _load_resident(pairs, sem):
    copies = [pltpu.make_async_copy(src, dst, sem.at[i]) for i, (src, dst) in enumerate(pairs)]
    for c in copies:
        c.start()
    for c in copies:
        c.wait()


STAGE_ROWS = 512
STAGE_COLS = 1024
PREP_ROWS = 256


def _prep_weights(chunks, stage, sem):
    def copy(i):
        src, r0, nr, c0, nc, _ = chunks[i]
        return pltpu.make_async_copy(src.at[pl.ds(r0, nr), pl.ds(c0, nc)],
                                     stage.at[i % 2, pl.ds(0, nr), pl.ds(0, nc)], sem.at[i % 2])

    copy(0).start()
    for i, (_, _, nr, _, nc, emit) in enumerate(chunks):
        if i + 1 < len(chunks):
            copy(i + 1).start()
        copy(i).wait()
        step = min(PREP_ROWS, nr)
        for r in range(0, nr, step):
            emit(slice(r, r + step), stage[i % 2, r:r + step, 0:nc])


def _plain_chunks(src, dst):
    rows, cols = src.shape

    def emit_at(r0, c0, nc):
        def emit(rs, v):
            dst[r0 + rs.start:r0 + rs.stop, c0:c0 + nc] = v.astype(BF16)
        return emit

    return [(src, r0, min(STAGE_ROWS, rows - r0), c0, min(STAGE_COLS, cols - c0),
             emit_at(r0, c0, min(STAGE_COLS, cols - c0)))
            for r0 in range(0, rows, STAGE_ROWS) for c0 in range(0, cols, STAGE_COLS)]


IN_O_DT = D_INNER + CONV_DIM
IN_O_POOL = IN_O_DT + N_HEADS
IN_O_GATE = IN_O_POOL + POOL_DIM


def _transposed_chunks(src_t, row0, n_rows, dst, dst_col0):
    def emit_at(j0):
        def emit(rs, v):
            dst[:, dst_col0 + j0 + rs.start:dst_col0 + j0 + rs.stop] = v.T.astype(BF16)
        return emit

    return [(src_t, row0 + j0, min(STAGE_ROWS, n_rows - j0), 0, D_MODEL, emit_at(j0))
            for j0 in range(0, n_rows, STAGE_ROWS)]


def _mix_weight_chunks(w_in_t, w_ref):
    def emit_dt(rs, v):
        vt = v.T
        lane = lax.broadcasted_iota(jnp.int32, vt.shape, 1)
        w_ref[:, MIX_WIDTH - DT_PAD:MIX_WIDTH] = jnp.where(lane < N_HEADS, vt, 0.0).astype(BF16)

    chunks = _transposed_chunks(w_in_t, 0, IN_O_DT, w_ref, 0)
    chunks += _transposed_chunks(w_in_t, IN_O_POOL, POOL_DIM, w_ref, IN_O_DT)
    chunks.append((w_in_t, IN_O_DT, DT_PAD, 0, D_MODEL, emit_dt))
    return chunks


def _time_of_row(r):
    return (r & (SUBLANES - 1)) * TILES + (r >> 3)


def _conv_silu(taps, cw_ref, cb_ref, cols):
    acc = cb_ref[:, cols]
    for k, t in enumerate(taps):
        acc = acc + t * cw_ref[k:k + 1, cols]
    return _silu(acc)


def _pool_mix(win_sum, u, cnt, wp_ref, ps_ref, gi):
    cols = slice(gi * POOL_GROUP_DIM, (gi + 1) * POOL_GROUP_DIM)
    d = win_sum / cnt - u
    mixed = _dot(d.astype(BF16), wp_ref[gi].astype(BF16))
    return (mixed * ps_ref[:, cols]).astype(BF16)


def _project(x_ref, gpre_ref, w_ref, dtb_ref, to_perm, xnp_buf, xbc_buf, z_buf, u_buf, dt_buf):
    xn = _rms(x_ref[...], gpre_ref[...]).astype(BF16)
    xnp_buf[...] = _dot(to_perm, xn).astype(BF16)
    yield

    def proj(c0, width):
        tile = 256
        return jnp.concatenate([_dot(xnp_buf[...], w_ref[:, c:c + min(tile, c0 + width - c)])
                                for c in range(c0, c0 + width, tile)], axis=1)

    piece = 512
    for c0 in range(0, CONV_DIM, piece):
        xbc_buf[CONV_HALO:, c0:c0 + piece] = proj(D_INNER + c0, piece)
        yield
    o_u = D_INNER + CONV_DIM
    for c0 in range(0, POOL_DIM, piece):
        u_buf[POOL_HALO:, c0:c0 + piece] = proj(o_u + c0, piece)
        yield
    dt_buf[...] = _softplus(proj(MIX_WIDTH - DT_PAD, DT_PAD) + dtb_ref[...])
    for c0 in range(0, D_INNER, piece):
        z_buf[:, c0:c0 + piece] = proj(c0, piece)
        yield


def _run(gen):
    for _ in gen:
        pass


def _interleave(main, side, side_after):
    next(side)
    i = 0
    while next(main, StopIteration) is not StopIteration:
        for _ in range(side_after(i)):
            next(side, None)
        i += 1
    _run(side)


def _project_pieces_after(i):
    return i % 2


def _sequence(xbc_buf, z_buf, u_buf, dt_buf, xc_buf, y_buf, ctail, ptail, ht_ref,
              cw_ref, cb_ref, alog_ref, dskip_ref, wp_ref, ps_ref, sn_ref, masks, ya_ref, yb_ref, rows, tile_in_seq):
    q = CHUNK
    to_nat, causal, tri, lo_half, t_col = masks

    for n, j in enumerate(range(TILES - (CONV_WIDTH - 1), TILES)):
        src = CONV_HALO + j * SUBLANES
        xbc_buf[n * SUBLANES + 1:(n + 1) * SUBLANES, :] = xbc_buf[src:src + SUBLANES - 1, :]
        xbc_buf[n * SUBLANES:n * SUBLANES + 1, :] = ctail[n:n + 1, :]
    for c0 in range(0, CONV_DIM, 512):
        cols = slice(c0, c0 + 512)
        taps = []
        for k in range(CONV_WIDTH):
            start = CONV_HALO - (CONV_WIDTH - 1 - k) * SUBLANES
            taps.append(xbc_buf[start:start + q, cols])
        xc = _conv_silu(taps, cw_ref, cb_ref, cols)
        xc_buf[:, cols] = xc
        yield
    for n, j in enumerate(range(TILES - (CONV_WIDTH - 1), TILES)):
        src = CONV_HALO + j * SUBLANES + SUBLANES - 1
        ctail[n:n + 1, :] = xbc_buf[src:src + 1, :]

    dt = dt_buf[...]
    da = dt * (-jnp.exp(alog_ref[...]))
    acum = sum(_dot(tri, part) for part in _split3(da)) * LOG2E
    acum_last = acum[q - 1:q, :]
    e_last = jnp.exp2(acum_last)
    w_end_t = (jnp.exp2(acum_last - acum) * dt).T
    acum_t = acum.T
    dt_t = dt.T
    lo_half_row = lo_half[0:1, :]

    ssq = jnp.zeros((q, 1), F32)
    for g in range(N_GROUPS):
        b_g = xc_buf[:, D_INNER + g * D_STATE:D_INNER + (g + 1) * D_STATE]
        c_off = D_INNER + N_GROUPS * D_STATE + g * D_STATE
        c_b = xc_buf[:, c_off:c_off + D_STATE].astype(BF16)
        cb = _dot_nt(c_b, b_g.astype(BF16))
        b_t = b_g.T
        y_inter = _dot(c_b, ht_ref[g].astype(BF16))
        for pair in range(HEADS_PER_GROUP // 2):
            h0 = g * HEADS_PER_GROUP + 2 * pair
            h1 = h0 + 1
            pcols = slice(h0 * HEAD_DIM, (h0 + 2) * HEAD_DIM)
            scols = slice(pair * LANES, (pair + 1) * LANES)
            x_pair = xc_buf[:, pcols]
            ms, bws, e_cols = [], [], []
            for h in (h0, h1):
                a_col = jnp.broadcast_to(acum[:, h:h + 1], (q, q))
                seg = a_col - acum_t[h:h + 1, :]
                decay = jnp.exp2(jnp.where(causal, seg, -jnp.inf))
                ms.append((cb * decay * dt_t[h:h + 1, :]).astype(BF16))
                bws.append((b_t * w_end_t[h:h + 1, :]).astype(BF16))
                e_cols.append(jnp.exp2(a_col))
            x_top = jnp.where(lo_half, x_pair, 0.0).astype(BF16)
            x_bot = jnp.where(lo_half, 0.0, x_pair).astype(BF16)
            x_diag = jnp.concatenate([x_top, x_bot], axis=0)
            y_intra = _dot(jnp.concatenate(ms, axis=1), x_diag)
            e_pair = jnp.where(lo_half, e_cols[0], e_cols[1])
            y_pair = y_intra + y_inter[:, scols] * e_pair + dskip_ref[:, pcols] * x_pair
            gated = y_pair * _silu(z_buf[:, pcols])
            y_buf[:, pcols] = gated
            ssq = ssq + jnp.sum(gated * gated, axis=-1, keepdims=True)
            upd = _dot(jnp.concatenate(bws, axis=1), x_diag)
            e_last_pair = jnp.where(lo_half_row, e_last[:, h0:h0 + 1], e_last[:, h1:h1 + 1])
            ht_ref[g, :, scols] = ht_ref[g, :, scols] * e_last_pair + upd
            yield

    inv = lax.rsqrt(ssq * (1.0 / D_INNER) + EPS)
    for c0 in range(0, D_INNER, 1024):
        cols = slice(c0, c0 + 1024)
        ya = (y_buf[:, cols] * inv * sn_ref[:, cols]).astype(BF16)
        ya_ref[rows, cols] = _dot(to_nat, ya).astype(BF16)
        yield

    u_buf[1:q, :] = u_buf[POOL_HALO:POOL_HALO + q - 1, :]
    for j in range(1, TILES):
        u_buf[j * SUBLANES:j * SUBLANES + 1, :] = ptail[j - 1:j, :]
    pos = tile_in_seq * q + t_col
    yb_parts = []
    for gi, w in enumerate(POOL_WINDOWS):
        cols = slice(gi * POOL_GROUP_DIM, (gi + 1) * POOL_GROUP_DIM)
        u_g = u_buf[POOL_HALO:POOL_HALO + q, cols]
        s = u_buf[POOL_HALO - (w - 1) * SUBLANES:POOL_HALO + q, cols]
        shift = SUBLANES
        while shift < w * SUBLANES:
            s = s[shift:, :] + s[:-shift, :]
            shift *= 2
        cnt = jnp.minimum(pos + 1, w).astype(F32)
        yb_parts.append(_pool_mix(s, u_g, cnt, wp_ref, ps_ref, gi))
        yield
    yb_ref[rows, :] = _dot(to_nat, jnp.concatenate(yb_parts, axis=1)).astype(BF16)
    for j in range(1, TILES):
        src = POOL_HALO + j * SUBLANES + SUBLANES - 1
        ptail[j - 1:j, :] = u_buf[src:src + 1, :]


def _project_plain(x_ref, g_ref, w_ref, out_refs):
    xn = _rms(x_ref[...], g_ref[...]).astype(BF16)
    off = 0
    for o_ref in out_refs:
        width = o_ref.shape[1]
        for c in range(0, width, 1024):
            cw = min(1024, width - c)
            o_ref[:, c:c + cw] = _dot(xn, w_ref[:, off + c:off + c + cw])
        off += width


def _mix_kernel(x0_ref, x1_ref, x2_ref, x3_ref, x4_ref, xs_ref, gpre_ref, w_hbm, cw_ref, cb_ref, dtb_ref, alog_ref, dskip_ref,
                wp_hbm, ps_ref, sn_ref,
                ya_ref, yb_ref, nconv_ref, nssm_ref, npool_ref, zs_ref, xbcs_ref, us_ref, dts_ref,
                xbc0, z0, u0, dt0, xbc1, z1, u1, dt1, xc0, y0, xc1, y1, ctail, ptail, ht_ref,
                w_ref, wp_ref, wsem, xnp_buf, stage, ssem, wp_f32, *, tiles_per_seq):
    q = CHUNK
    k = pl.program_id(0)
    tile_in_seq = lax.rem(TILES_PER_STEP * k, tiles_per_seq)
    row = lax.broadcasted_iota(jnp.int32, (q, q), 0)
    col = lax.broadcasted_iota(jnp.int32, (q, q), 1)
    to_perm = jnp.where(col == _time_of_row(row), 1.0, 0.0).astype(BF16)
    to_nat = jnp.where(_time_of_row(col) == row, 1.0, 0.0).astype(BF16)
    causal = _time_of_row(row) >= _time_of_row(col)
    tri = jnp.where(causal, 1.0, 0.0).astype(BF16)
    t_col = _time_of_row(lax.broadcasted_iota(jnp.int32, (q, 1), 0))
    masks = (to_nat, causal, tri, col < HEAD_DIM, t_col)
    set0 = (xbc0, z0, u0, dt0)
    set1 = (xbc1, z1, u1, dt1)
    state = (ctail, ptail, ht_ref)
    consts = (cw_ref, cb_ref, alog_ref, dskip_ref, wp_ref, ps_ref, sn_ref, masks)

    @pl.when(k == 0)
    def _():
        _load_resident([(wp_hbm, wp_f32)], wsem)
        for g in range(wp_ref.shape[0]):
            wp_ref[g] = wp_f32[g].astype(BF16)
        _prep_weights(_mix_weight_chunks(w_hbm, w_ref), stage, ssem)
        _run(_project(x0_ref, gpre_ref, w_ref, dtb_ref, to_perm, xnp_buf, *set0))

    @pl.when(tile_in_seq == 0)
    def _():
        ctail[...] = jnp.zeros(ctail.shape, F32)
        ptail[...] = jnp.zeros(ptail.shape, F32)
        ht_ref[...] = jnp.zeros(ht_ref.shape, F32)

    sets = (set0, set1)
    seq_scratch = ((xc0, y0), (xc1, y1))
    for p, x_next in enumerate((x1_ref, x2_ref, x3_ref, x4_ref)):
        _interleave(_sequence(*sets[p % 2], *seq_scratch[p % 2], *state, *consts, ya_ref, yb_ref,
                              slice(p * q, (p + 1) * q), tile_in_seq + p),
                    _project(x_next, gpre_ref, w_ref, dtb_ref, to_perm, xnp_buf, *sets[(p + 1) % 2]),
                    _project_pieces_after)

    @pl.when(tile_in_seq == tiles_per_seq - TILES_PER_STEP)
    def _():
        for g in range(N_GROUPS):
            nssm_ref[0, g * HEADS_PER_GROUP:(g + 1) * HEADS_PER_GROUP] = (
                ht_ref[g].T.reshape(HEADS_PER_GROUP, HEAD_DIM, D_STATE))

    steps_per_seq = tiles_per_seq // TILES_PER_STEP
    for b in range(nconv_ref.shape[1]):
        @pl.when(k == (b + 1) * steps_per_seq - 1)
        def _():
            for j in range(CONV_WIDTH - 1):
                nconv_ref[j, b:b + 1, :] = ctail[j:j + 1, :]
            for j in range(POOL_BUF):
                npool_ref[j, b:b + 1, :] = ptail[j:j + 1, :]

    @pl.when(k == pl.num_programs(0) - 1)
    def _():
        _project_plain(xs_ref, gpre_ref, w_ref, (zs_ref, xbcs_ref, us_ref, dts_ref))


def _mix_prompt(x, xs, gpre, w, cw, cb, dtb, alog, dskip_e, wp, ps, sn, bsz, seq):
    q = CHUNK
    nt = seq // q
    n_tiles = bsz * nt
    tps = TILES_PER_STEP
    assert nt % tps == 0
    steps_per_seq = nt // tps
    m = bsz * seq
    consts = (gpre, w, cw, cb, dtb, alog, dskip_e, wp, ps, sn)
    nb = xs.shape[0]
    sample_widths = (D_INNER, CONV_DIM, POOL_DIM, DT_PAD)
    pair = lambda k: (k, 0)
    proj_set = [pltpu.VMEM((CONV_HALO + q, CONV_DIM), F32), pltpu.VMEM((q, D_INNER), F32),
                pltpu.VMEM((POOL_HALO + q, POOL_DIM), F32), pltpu.VMEM((q, DT_PAD), F32)]
    seq_set = [pltpu.VMEM((q, CONV_DIM), F32), pltpu.VMEM((q, D_INNER), F32)]
    return pl.pallas_call(
        functools.partial(_mix_kernel, tiles_per_seq=nt),
        grid=(n_tiles // tps,),
        in_specs=[pl.BlockSpec((q, D_MODEL), lambda k: (0, 0)),
                  *[pl.BlockSpec((q, D_MODEL), lambda k, p=p: (jnp.minimum(tps * k + p, n_tiles - 1), 0))
                    for p in range(1, tps + 1)],
                  _const_spec(xs.shape)]
                 + [_HBM_SPEC if a is w or a is wp else _const_spec(a.shape) for a in consts],
        out_specs=[pl.BlockSpec((tps * q, D_INNER), pair),
                   pl.BlockSpec((tps * q, POOL_DIM), pair),
                   _const_spec((CONV_WIDTH - 1, bsz, CONV_DIM)),
                   pl.BlockSpec((1, N_HEADS, HEAD_DIM, D_STATE), lambda k: (k // steps_per_seq, 0, 0, 0)),
                   _const_spec((POOL_BUF, bsz, POOL_DIM))]
                  + [_const_spec((nb, wd)) for wd in sample_widths],
        out_shape=[jax.ShapeDtypeStruct((m, D_INNER), BF16),
                   jax.ShapeDtypeStruct((m, POOL_DIM), BF16),
                   jax.ShapeDtypeStruct((CONV_WIDTH - 1, bsz, CONV_DIM), F32),
                   jax.ShapeDtypeStruct((bsz, N_HEADS, HEAD_DIM, D_STATE), F32),
                   jax.ShapeDtypeStruct((POOL_BUF, bsz, POOL_DIM), F32)]
                  + [jax.ShapeDtypeStruct((nb, wd), F32) for wd in sample_widths],
        scratch_shapes=proj_set + proj_set + seq_set + seq_set + [
            pltpu.VMEM((SUBLANES, CONV_DIM), F32),
            pltpu.VMEM((2 * SUBLANES, POOL_DIM), F32),
            pltpu.VMEM((N_GROUPS, D_STATE, GROUP_COLS), F32),
            pltpu.VMEM((D_MODEL, MIX_WIDTH), BF16), pltpu.VMEM(wp.shape, BF16), pltpu.SemaphoreType.DMA((1,)),
            pltpu.VMEM((q, D_MODEL), BF16),
            pltpu.VMEM((2, STAGE_ROWS, STAGE_COLS), F32), pltpu.SemaphoreType.DMA((2,)),
            pltpu.VMEM(wp.shape, F32)],
        compiler_params=pltpu.CompilerParams(dimension_semantics=("arbitrary",),
                                             vmem_limit_bytes=VMEM_LIMIT),
        name="mix_prompt",
    )(*([x] * (tps + 1)), xs, *consts)


def _sample_tok_kernel(xbc_ref, dt_ref, u_ref, sconv_ref, spool_ref, cw_ref, cb_ref, dtb_ref, alog_ref,
                       wp_ref, ps_ref,
                       xc_ref, dec_ref, dtx_ref, nconv_ref, npool_ref, yb_ref):
    for c0 in range(0, CONV_DIM, 512):
        cols = slice(c0, c0 + 512)
        taps = [sconv_ref[k, :, cols] for k in range(CONV_WIDTH - 1)]
        taps.append(xbc_ref[:, cols])
        xc_ref[:, cols] = _conv_silu(taps, cw_ref, cb_ref, cols)
    for k in range(CONV_WIDTH - 2):
        nconv_ref[k] = sconv_ref[k + 1]
    nconv_ref[CONV_WIDTH - 2] = xbc_ref[...]

    dt = _softplus(dt_ref[...] + dtb_ref[...])
    dec_ref[...] = jnp.exp(dt * (-jnp.exp(alog_ref[...])))
    hrow = lax.broadcasted_iota(jnp.int32, (DT_PAD, D_INNER), 0)
    ccol = lax.broadcasted_iota(jnp.int32, (DT_PAD, D_INNER), 1)
    expand = jnp.where((ccol >= hrow * HEAD_DIM) & (ccol < (hrow + 1) * HEAD_DIM), 1.0, 0.0).astype(BF16)
    dt_e = sum(_dot(part, expand) for part in _split3(dt))
    dtx_ref[...] = dt_e * xc_ref[:, 0:D_INNER]

    cnt_base = PAST_LEN + 1
    for gi, w in enumerate(POOL_WINDOWS):
        cols = slice(gi * POOL_GROUP_DIM, (gi + 1) * POOL_GROUP_DIM)
        u_g = u_ref[:, cols]
        s = u_g
        for k in range(1, w):
            s = s + spool_ref[POOL_BUF - k, :, cols]
        yb_ref[:, cols] = _pool_mix(s, u_g, float(min(cnt_base, w)), wp_ref, ps_ref, gi)
    for k in range(POOL_BUF - 1):
        npool_ref[k] = spool_ref[k + 1]
    npool_ref[POOL_BUF - 1] = u_ref[...]


def _sample_tok(xbc, dt, u, sconv, spool, cw, cb, dtb, alog, wp, ps):
    nb = xbc.shape[0]
    full = lambda a: pl.BlockSpec(a.shape, lambda i: (0,) * a.ndim)
    args = (xbc, dt, u, sconv, spool, cw, cb, dtb, alog, wp, ps)
    out_shape = [jax.ShapeDtypeStruct((nb, CONV_DIM), F32),
                 jax.ShapeDtypeStruct((nb, DT_PAD), F32),
                 jax.ShapeDtypeStruct((nb, D_INNER), F32),
                 jax.ShapeDtypeStruct(sconv.shape, F32),
                 jax.ShapeDtypeStruct(spool.shape, F32),
                 jax.ShapeDtypeStruct((nb, POOL_DIM), BF16)]
    return pl.pallas_call(
        _sample_tok_kernel,
        grid=(1,),
        in_specs=[full(a) for a in args],
        out_specs=[pl.BlockSpec(s.shape, lambda i, nd=len(s.shape): (0,) * nd) for s in out_shape],
        out_shape=out_shape,
        compiler_params=pltpu.CompilerParams(dimension_semantics=("arbitrary",),
                                             vmem_limit_bytes=VMEM_LIMIT),
        name="sample_tok",
    )(*args)


def _sample_ssm_kernel(st_ref, b_ref, c_ref, dec_ref, dtx_ref, xc_ref, z_ref, dskip_ref, sn_ref, ya_ref, nst_ref):
    grow = lax.broadcasted_iota(jnp.int32, (N_GROUPS, D_INNER), 0)
    gcol = lax.broadcasted_iota(jnp.int32, (N_GROUPS, D_INNER), 1)
    gmask = (gcol >= grow * GROUP_COLS) & (gcol < (grow + 1) * GROUP_COLS)
    srow = lax.broadcasted_iota(jnp.int32, (N_GROUPS, D_STATE), 0)

    def per_group(row):
        out = jnp.zeros((N_GROUPS, D_STATE), F32)
        for g in range(N_GROUPS):
            out = jnp.where(srow == g, row[:, g * D_STATE:(g + 1) * D_STATE], out)
        return out

    for t in range(st_ref.shape[0]):
        tok = slice(t, t + 1)
        dtx = jnp.where(gmask, dtx_ref[tok, :], 0.0)
        x_hi, x_mid, _ = _split3(dtx)
        b_hi, b_mid, _ = _split3(per_group(b_ref[tok, :]))
        lhs = jnp.concatenate([x_hi.astype(F32), x_mid.astype(F32), x_hi.astype(F32)], axis=0)
        rhs = jnp.concatenate([b_hi.astype(F32), b_hi.astype(F32), b_mid.astype(F32)], axis=0)
        upd = lax.dot_general(lhs, rhs, (((0,), (0,)), ((), ())), preferred_element_type=F32)
        dec = dec_ref[tok, :]
        c_b = per_group(c_ref[tok, :]).astype(BF16)
        y_parts = []
        for g in range(N_GROUPS):
            new_g = []
            for r in range(HEADS_PER_GROUP):
                h = g * HEADS_PER_GROUP + r
                scale = jnp.broadcast_to(dec[:, h:h + 1], (HEAD_DIM, D_STATE))
                new = st_ref[t, h] * scale + upd[h * HEAD_DIM:(h + 1) * HEAD_DIM, :]
                nst_ref[t, h] = new
                new_g.append(new.astype(BF16))
            y_g = _dot_nt(c_b, jnp.concatenate(new_g, axis=0))
            gcols = slice(g * GROUP_COLS, (g + 1) * GROUP_COLS)
            y_parts.append(y_g[g:g + 1, :] + dskip_ref[:, gcols] * xc_ref[tok, gcols])
        y = jnp.concatenate(y_parts, axis=1)
        ya_ref[tok, :] = _rms(y * _silu(z_ref[tok, :]), sn_ref[...])


def _sample_ssm(state, dec, dtx, xc, z, dskip_e, sn):
    nb = state.shape[0]
    tb = _token_tile(nb, SUBLANES)
    tok2 = lambda i: (i, 0)
    bc_width = N_GROUPS * D_STATE
    assert D_INNER % bc_width == 0
    state_spec = pl.BlockSpec((tb, N_HEADS, HEAD_DIM, D_STATE), lambda i: (i, 0, 0, 0))
    return pl.pallas_call(
        _sample_ssm_kernel,
        grid=(nb // tb,),
        in_specs=[state_spec,
                  pl.BlockSpec((tb, bc_width), lambda i: (i, D_INNER // bc_width)),
                  pl.BlockSpec((tb, bc_width), lambda i: (i, D_INNER // bc_width + 1)),
                  pl.BlockSpec((tb, DT_PAD), tok2),
                  pl.BlockSpec((tb, D_INNER), tok2),
                  pl.BlockSpec((tb, D_INNER), tok2),
                  pl.BlockSpec((tb, D_INNER), tok2),
                  _const_spec(dskip_e.shape), _const_spec(sn.shape)],
        out_specs=[pl.BlockSpec((tb, D_INNER), tok2), state_spec],
        out_shape=[jax.ShapeDtypeStruct((nb, D_INNER), F32),
                   jax.ShapeDtypeStruct(state.shape, F32)],
        compiler_params=pltpu.CompilerParams(dimension_semantics=("arbitrary",),
                                             vmem_limit_bytes=VMEM_LIMIT),
        name="sample_ssm",
    )(state, xc, xc, dec, dtx, xc, z, dskip_e, sn)


def _out_math(x, ya_in, yb_in, gpre_ref, wg_ref, wa_ref, wb_ref, wo_ref, gpost_ref, fpre_ref, w1_ref, w2_ref,
              fpost_ref):
    xn = _rms(x, gpre_ref[...]).astype(BF16)
    ya = _dot(ya_in.astype(BF16), wa_ref[...])
    yb = _dot(yb_in, wb_ref[...])
    merged = (jax.nn.sigmoid(_dot(xn, wg_ref[:, 0:D_MODEL])) * ya
              + jax.nn.sigmoid(_dot(xn, wg_ref[:, D_MODEL:])) * yb)
    mo = _dot(merged.astype(BF16), wo_ref[...])
    h = x + _rms(mo, gpost_ref[...])
    hn = _rms(h, fpre_ref[...]).astype(BF16)
    gate = _dot(hn, w1_ref[:, 0:D_FF])
    up = _dot(hn, w1_ref[:, D_FF:])
    f = _dot((_silu(gate) * up).astype(BF16), w2_ref[...])
    return h + _rms(f, fpost_ref[...])


def _out_kernel(x_ref, ya_ref, yb_ref, xs_ref, yas_ref, ybs_ref, gpre_ref, wint_hbm, wa_hbm, wb_hbm, wo_hbm, gpost_ref,
                fpre_ref, w1_hbm, w2_hbm, fpost_ref, o_ref, os_ref,
                wg_ref, wa_ref, wb_ref, wo_ref, w1_ref, w2_ref, stage, ssem):
    i = pl.program_id(0)
    n_prompt = pl.num_programs(0) - 1
    params = (gpre_ref, wg_ref, wa_ref, wb_ref, wo_ref, gpost_ref, fpre_ref, w1_ref, w2_ref, fpost_ref)

    @pl.when(i == 0)
    def _():
        chunks = _transposed_chunks(wint_hbm, IN_O_GATE, 2 * D_MODEL, wg_ref, 0)
        for src, dst in ((wa_hbm, wa_ref), (wb_hbm, wb_ref), (wo_hbm, wo_ref), (w1_hbm, w1_ref), (w2_hbm, w2_ref)):
            chunks += _plain_chunks(src, dst)
        _prep_weights(chunks, stage, ssem)

    @pl.when(i < n_prompt)
    def _():
        o_ref[...] = _out_math(x_ref[...], ya_ref[...], yb_ref[...], *params)

    @pl.when(i == n_prompt)
    def _():
        os_ref[...] = _out_math(xs_ref[...], yas_ref[...], ybs_ref[...], *params)


def _out(x, ya, yb, xs, yas, ybs, consts, tm):
    m = x.shape[0]
    n = m // tm
    nb = xs.shape[0]
    tok = lambda i: (jnp.minimum(i, n - 1), 0)
    big = [a.shape[0] >= D_MODEL and a.shape[1] >= D_MODEL for a in consts]
    resident = [(D_MODEL, 2 * D_MODEL)] + [a.shape for a, b in zip(consts, big) if b][1:]
    once = lambda a: pl.BlockSpec(a.shape, lambda i: (0, 0), pipeline_mode=pl.Buffered(1))
    return pl.pallas_call(
        _out_kernel,
        grid=(n + 1,),
        in_specs=[pl.BlockSpec((tm, D_MODEL), tok), pl.BlockSpec((tm, D_INNER), tok),
                  pl.BlockSpec((tm, POOL_DIM), tok), once(xs), once(yas), once(ybs)]
                 + [_HBM_SPEC if b else _const_spec(a.shape) for a, b in zip(consts, big)],
        out_specs=[pl.BlockSpec((tm, D_MODEL), tok), pl.BlockSpec((nb, D_MODEL), lambda i: (0, 0))],
        out_shape=[jax.ShapeDtypeStruct((m, D_MODEL), F32), jax.ShapeDtypeStruct((nb, D_MODEL), F32)],
        scratch_shapes=[pltpu.VMEM(shape, BF16) for shape in resident]
                       + [pltpu.VMEM((2, STAGE_ROWS, STAGE_COLS), F32), pltpu.SemaphoreType.DMA((2,))],
        compiler_params=pltpu.CompilerParams(dimension_semantics=("arbitrary",),
                                             vmem_limit_bytes=VMEM_LIMIT_OUT),
        name="out",
    )(x, ya, yb, xs, yas, ybs, *consts)


def _token_tile(m, cap):
    t = min(m, cap)
    assert m % t == 0
    return t


def kernel(x_prompt, x_sample, state_conv, state_ssm, state_pool, norm_mix_pre, norm_mix_post, norm_ffn_pre,
           norm_ffn_post, w_in, conv_w, conv_b, dt_bias, a_log, d_skip, ssm_norm, w_pool_group, pool_scale,
           w_branch_a, w_branch_b, w_out, w_ffn_in, w_ffn_out):
    bsz, seq, _ = x_prompt.shape
    nb, dec_seq, _ = x_sample.shape
    assert dec_seq == 1 and norm_mix_pre.shape[0] == 1
    l = 0
    wit = jnp.swapaxes(w_in[l], 0, 1)
    pad_h = lambda v: jnp.pad(v.astype(F32), (0, DT_PAD - N_HEADS)).reshape(1, DT_PAD)
    row = lambda v: v.astype(F32).reshape(1, -1)
    dtb, alog = pad_h(dt_bias[l]), pad_h(a_log[l])
    dskip_e = jnp.repeat(d_skip[l].astype(F32), HEAD_DIM).reshape(1, D_INNER)
    cw, cb = conv_w[l].astype(F32), row(conv_b[l])
    wp, ps = w_pool_group[l], row(pool_scale[l])
    g_pre, sn = row(norm_mix_pre[l]), row(ssm_norm[l])
    out_consts = (g_pre, wit, w_branch_a[l], w_branch_b[l], w_out[l],
                  row(norm_mix_post[l]), row(norm_ffn_pre[l]), w_ffn_in[l], w_ffn_out[l],
                  row(norm_ffn_post[l]))

    xp = x_prompt.reshape(bsz * seq, D_MODEL)
    xs2 = x_sample.reshape(nb, D_MODEL)
    ya, yb, nconv_p, nssm_p, npool_p, z, xbc, u, dt = _mix_prompt(xp, xs2, g_pre, wit, cw, cb, dtb, alog, dskip_e, wp,
                                                                  ps, sn, bsz, seq)
    sconv = jnp.swapaxes(state_conv[l], 0, 1)
    spool = jnp.swapaxes(state_pool[l], 0, 1)
    xc, dec, dtx, nconv_s, npool_s, yb_s = _sample_tok(xbc, dt, u, sconv, spool, cw, cb, dtb, alog, wp, ps)
    ya_s, nssm_s = _sample_ssm(state_ssm[l], dec, dtx, xc, z, dskip_e, sn)
    out_p, out_s = _out(xp, ya, yb, xs2, ya_s, yb_s, out_consts, _token_tile(bsz * seq, 256))
    out_p = out_p.reshape(bsz, seq, D_MODEL)
    out_s = out_s.reshape(nb, 1, D_MODEL)

    return (out_p, out_s,
            jnp.swapaxes(nconv_p, 0, 1)[None], nssm_p[None], jnp.swapaxes(npool_p, 0, 1)[None],
            jnp.swapaxes(nconv_s, 0, 1)[None], nssm_s[None],
            jnp.swapaxes(npool_s, 0, 1)[None])
```
